```python
import math
import jax
import jax.numpy as jnp
from jax import lax
import numpy as np

D_MODEL = 2048
BATCH = 2
SEQ = 4096
DEPTH = 1

D_MIX = D_MODEL
D_SSM = D_MIX // 2
D_ATTN = D_MIX - D_SSM
SSM_HEAD_DIM = 64
SSM_HEADS = D_SSM // SSM_HEAD_DIM
SSM_GROUPS = 2
SSM_STATE = 128
CONV_WIDTH = 4
CONV_CH = D_SSM + 2 * SSM_GROUPS * SSM_STATE
SSD_CHUNK = 128
ATTN_HEAD_DIM = 128
ATTN_HEADS = D_ATTN // ATTN_HEAD_DIM
IDX_HEADS = 16
IDX_DIM = 64
INDEX_TOPK = 256
Q_BLOCK = 128
ROPE_THETA = 10000.0
N_EXPERTS = 64
N_EXPERT_GROUPS = 8
EXPERTS_PER_GROUP = N_EXPERTS // N_EXPERT_GROUPS
TOPK_GROUPS = 4
TOP_K = 8
D_EXPERT = 512
D_SHARED = 512
ROUTED_SCALE = 2.5
ALPHA = (2.0 * DEPTH) ** 0.25
BETA = (8.0 * DEPTH) ** -0.25
SEG_SIZES = (D_SSM, CONV_CH, SSM_HEADS, D_ATTN, D_ATTN, D_ATTN, IDX_HEADS * IDX_DIM, IDX_DIM, IDX_HEADS)
SPLIT_POINTS = tuple(int(v) for v in np.cumsum(SEG_SIZES)[:-1])
D_IN_PROJ = int(sum(SEG_SIZES))

kernel_name = 'hybrid_ssd_dsa_moe_block'


def layer_norm(x, g, b, eps=1e-5):
    xf = x.astype(jnp.float32)
    mu = jnp.mean(xf, axis=-1, keepdims=True)
    var = jnp.mean(jnp.square(xf - mu), axis=-1, keepdims=True)
    return ((xf - mu) * lax.rsqrt(var + eps) * g.astype(jnp.float32) + b.astype(jnp.float32)).astype(x.dtype)


def gated_rms_norm(y, z, w, eps=1e-5):
    yf = (y * jax.nn.silu(z)).astype(jnp.float32)
    return (yf * lax.rsqrt(jnp.mean(jnp.square(yf), axis=-1, keepdims=True) + eps) * w.astype(jnp.float32)).astype(y.dtype)


def rope_tables(positions, dim):
    inv_freq = 1.0 / (ROPE_THETA ** (jnp.arange(0, dim, 2, dtype=jnp.float32) / dim))
    ang = positions.astype(jnp.float32)[..., None] * inv_freq
    return jnp.cos(ang), jnp.sin(ang)


def apply_rope(x, cos, sin):
    half = x.shape[-1] // 2
    x1, x2 = x[..., :half], x[..., half:]
    cos = cos[:, :, None, :].astype(x.dtype)
    sin = sin[:, :, None, :].astype(x.dtype)
    return jnp.concatenate([x1 * cos - x2 * sin, x2 * cos + x1 * sin], axis=-1)


def causal_depthwise_conv(x, w, b):
    k_w = w.shape[0]
    out = lax.conv_general_dilated(
        x, w[:, None, :].astype(x.dtype), window_strides=(1,), padding=[(k_w - 1, 0)],
        dimension_numbers=('NWC', 'WIO', 'NWC'), feature_group_count=x.shape[-1])
    return out + b.astype(x.dtype)


def ssd_chunked(xs, dt, a, b_mat, c_mat):
    bsz, s_len, n_h, p_dim = xs.shape
    n_g, n_st = b_mat.shape[2], b_mat.shape[3]
    q_len = SSD_CHUNK
    n_c = s_len // q_len
    hpg = n_h // n_g
    dtype = xs.dtype
    x_c = xs.reshape(bsz, n_c, q_len, n_g, hpg, p_dim)
    dt_c = dt.reshape(bsz, n_c, q_len, n_g, hpg)
    xdt = x_c * dt_c[..., None].astype(dtype)
    b_c = b_mat.reshape(bsz, n_c, q_len, n_g, n_st)
    c_c = c_mat.reshape(bsz, n_c, q_len, n_g, n_st)
    log_a = dt_c.astype(jnp.float32) * a.astype(jnp.float32).reshape(n_g, hpg)
    cs = jnp.cumsum(log_a, axis=2)
    causal = jnp.tril(jnp.ones((q_len, q_len), dtype=bool))[:, :, None, None]
    seg = cs[:, :, :, None] - cs[:, :, None, :]
    decay = jnp.exp(jnp.where(causal, seg, -jnp.inf)).astype(dtype)
    cb = jnp.einsum('bclgn,bcsgn->bclsg', c_c, b_c)
    y_diag = jnp.einsum('bclsg,bclsgh,bcsghp->bclghp', cb, decay, xdt)
    decay_to_end = jnp.exp(cs[:, :, -1:] - cs).astype(dtype)
    states = jnp.einsum('bcsgn,bcsgh,bcsghp->bcghpn', b_c, decay_to_end, xdt)
    chunk_decay = jnp.exp(cs[:, :, -1]).astype(dtype)

    def carry_state(h, inp):
        st, dec = inp
        return dec[..., None, None] * h + st, h

    h0 = jnp.zeros((bsz, n_g, hpg, p_dim, n_st), dtype)
    _, h_in = lax.scan(carry_state, h0, (jnp.moveaxis(states, 1, 0), jnp.moveaxis(chunk_decay, 1, 0)))
    h_in = jnp.moveaxis(h_in, 0, 1)
    y_off = jnp.einsum('bclgn,bcghpn,bclgh->bclghp', c_c, h_in, jnp.exp(cs).astype(dtype))
    return (y_diag + y_off).reshape(bsz, s_len, n_h, p_dim)


def dsa_attention(q, k, v, q_idx, k_idx, w_idx):
    bsz, s_len, n_h, d_h = q.shape
    topk = min(INDEX_TOPK, s_len // 4)
    n_blk = s_len // Q_BLOCK
    key_pos = jnp.arange(s_len, dtype=jnp.int32)
    gather_rows = jax.vmap(lambda table, idx: table[idx])

    def block(i):
        start = i * Q_BLOCK
        qb = lax.dynamic_slice_in_dim(q, start, Q_BLOCK, axis=1)
        qib = lax.dynamic_slice_in_dim(q_idx, start, Q_BLOCK, axis=1)
        wib = lax.dynamic_slice_in_dim(w_idx, start, Q_BLOCK, axis=1)
        t_pos = start + jnp.arange(Q_BLOCK, dtype=jnp.int32)
        causal = key_pos[None, :] <= t_pos[:, None]
        rel = jax.nn.relu(jnp.einsum('bqhd,bsd->bqhs', qib, k_idx))
        iscore = jnp.einsum('bqhs,bqh->bqs', rel, wib).astype(jnp.float32)
        iscore = jnp.where(causal[None], iscore, -jnp.inf)
        _, sel = lax.top_k(iscore, topk)
        valid = sel <= t_pos[None, :, None]
        k_sel = gather_rows(k, sel)
        v_sel = gather_rows(v, sel)
        logits = jnp.einsum('bqhd,bqkhd->bqhk', qb, k_sel).astype(jnp.float32) * (d_h ** -0.5)
        logits = jnp.where(valid[:, :, None, :], logits, -jnp.inf)
        probs = jax.nn.softmax(logits, axis=-1).astype(v.dtype)
        return jnp.einsum('bqhk,bqkhd->bqhd', probs, v_sel)

    out = lax.map(block, jnp.arange(n_blk, dtype=jnp.int32))
    return jnp.moveaxis(out, 0, 1).reshape(bsz, s_len, n_h * d_h)


def hybrid_mixer(u, cos_a, sin_a, cos_i, sin_i, w_in, conv_w, conv_b, dt_bias, a_log, d_skip,
                 ssd_norm_w, idx_k_ln_g, idx_k_ln_b, w_out):
    bsz, s_len, _ = u.shape
    proj = jnp.einsum('bsd,de->bse', u, w_in)
    z, xbc, dt_raw, q, k, v, q_idx, k_idx, w_idx = jnp.split(proj, SPLIT_POINTS, axis=-1)
    xbc = jax.nn.silu(causal_depthwise_conv(xbc, conv_w, conv_b))
    xs, b_mat, c_mat = jnp.split(xbc, [D_SSM, D_SSM + SSM_GROUPS * SSM_STATE], axis=-1)
    xs = xs.reshape(bsz, s_len, SSM_HEADS, SSM_HEAD_DIM)
    dt = jax.nn.softplus(dt_raw + dt_bias)
    a = -jnp.exp(a_log)
    y = ssd_chunked(xs, dt, a,
                    b_mat.reshape(bsz, s_len, SSM_GROUPS, SSM_STATE),
                    c_mat.reshape(bsz, s_len, SSM_GROUPS, SSM_STATE))
    y = y + d_skip[:, None] * xs
    y_ssm = gated_rms_norm(y.reshape(bsz, s_len, D_SSM), z, ssd_norm_w)
    q = apply_rope(q.reshape(bsz, s_len, ATTN_HEADS, ATTN_HEAD_DIM), cos_a, sin_a)
    k = apply_rope(k.reshape(bsz, s_len, ATTN_HEADS, ATTN_HEAD_DIM), cos_a, sin_a)
    v = v.reshape(bsz, s_len, ATTN_HEADS, ATTN_HEAD_DIM)
    q_idx = apply_rope(q_idx.reshape(bsz, s_len, IDX_HEADS, IDX_DIM), cos_i, sin_i)
    k_idx = apply_rope(layer_norm(k_idx, idx_k_ln_g, idx_k_ln_b)[:, :, None, :], cos_i, sin_i)[:, :, 0]
    w_idx = w_idx * (IDX_HEADS ** -0.5 * IDX_DIM ** -0.5)
    y_att = dsa_attention(q, k, v, q_idx, k_idx, w_idx)
    return jnp.einsum('bse,ed->bsd', jnp.concatenate([y_ssm, y_att], axis=-1), w_out)


def moe_ffn(h, w_router, router_bias, w_gate_e, w_up_e, w_down_e, w_gate_s, w_up_s, w_down_s):
    bsz, s_len, d = h.shape
    t = h.reshape(-1, d)
    n_tok = t.shape[0]
    scores = jax.nn.sigmoid(jnp.einsum('td,de->te', t, w_router).astype(jnp.float32))
    biased = scores + router_bias.astype(jnp.float32)
    grouped = biased.reshape(n_tok, N_EXPERT_GROUPS, EXPERTS_PER_GROUP)
    group_score = jnp.sum(lax.top_k(grouped, 2)[0], axis=-1)
    _, top_groups = lax.top_k(group_score, TOPK_GROUPS)
    group_keep = jnp.sum(jax.nn.one_hot(top_groups, N_EXPERT_GROUPS, dtype=jnp.float32), axis=1)
    expert_keep = jnp.repeat(group_keep, EXPERTS_PER_GROUP, axis=-1) > 0
    _, top_e = lax.top_k(jnp.where(expert_keep, biased, -jnp.inf), TOP_K)
    w = jnp.take_along_axis(scores, top_e, axis=-1)
    w = w / jnp.sum(w, axis=-1, keepdims=True) * ROUTED_SCALE
    gates = jnp.sum(jax.nn.one_hot(top_e, N_EXPERTS, dtype=jnp.float32) * w[..., None], axis=1).astype(h.dtype)
    routed = jnp.zeros_like(t)
    for g in range(N_EXPERT_GROUPS):
        sl = slice(g * EXPERTS_PER_GROUP, (g + 1) * EXPERTS_PER_GROUP)
        hid = jax.nn.silu(jnp.einsum('td,edf->tef', t, w_gate_e[sl])) * jnp.einsum('td,edf->tef', t, w_up_e[sl])
        routed = routed + jnp.einsum('tef,efd->td', hid * gates[:, sl, None], w_down_e[sl])
    shared = jnp.einsum('tf,fd->td', jax.nn.silu(t @ w_gate_s) * (t @ w_up_s), w_down_s)
    return (routed + shared).reshape(bsz, s_len, d)


def setup_inputs(seed: int = 0) -> dict:
    key = jax.random.key(seed)
    ks = jax.random.split(key, 32)
    f32 = jnp.float32

    def nrm(k, shape, scale):
        return jax.random.normal(k, shape, f32) * scale

    n_l = DEPTH
    x = nrm(ks[0], (BATCH, SEQ, D_MODEL), 1.0)
    c = nrm(ks[1], (BATCH, D_MODEL), 1.0)
    offset = jax.random.randint(ks[2], (BATCH, 1), 0, 1024, dtype=jnp.int32)
    positions = jnp.arange(SEQ, dtype=jnp.int32)[None, :] + offset
    w_ada = nrm(ks[3], (n_l, D_MODEL, 6 * D_MODEL), D_MODEL ** -0.5)
    b_ada = nrm(ks[4], (n_l, 6 * D_MODEL), 0.02)
    w_in = nrm(ks[5], (n_l, D_MODEL, D_IN_PROJ), D_MODEL ** -0.5)
    conv_w = nrm(ks[6], (n_l, CONV_WIDTH, CONV_CH), CONV_WIDTH ** -0.5)
    conv_b = nrm(ks[7], (n_l, CONV_CH), 0.02)
    dt0 = jnp.exp(jax.random.uniform(ks[8], (n_l, SSM_HEADS), f32, math.log(1e-3), math.log(1e-1)))
    dt_bias = dt0 + jnp.log(-jnp.expm1(-dt0))
    a_log = jnp.log(jax.random.uniform(ks[9], (n_l, SSM_HEADS), f32, 1.0, 16.0))
    d_skip = 1.0 + nrm(ks[10], (n_l, SSM_HEADS), 0.1)
    ssd_norm_w = 1.0 + nrm(ks[11], (n_l, D_SSM), 0.1)
    idx_k_ln_g = 1.0 + nrm(ks[12], (n_l, IDX_DIM), 0.1)
    idx_k_ln_b = nrm(ks[13], (n_l, IDX_DIM), 0.02)
    w_out = nrm(ks[14], (n_l, D_MIX, D_MODEL), BETA * D_MIX ** -0.5)
    ln1_g = 1.0 + nrm(ks[15], (n_l, D_MODEL), 0.1)
    ln1_b = nrm(ks[16], (n_l, D_MODEL), 0.02)
    w_router = nrm(ks[17], (n_l, D_MODEL, N_EXPERTS), D_MODEL ** -0.5)
    router_bias = nrm(ks[18], (n_l, N_EXPERTS), 0.01)
    w_gate_e = nrm(ks[19], (n_l, N_EXPERTS, D_MODEL, D_EXPERT), D_MODEL ** -0.5)
    w_up_e = nrm(ks[20], (n_l, N_EXPERTS, D_MODEL, D_EXPERT), D_MODEL ** -0.5)
    w_down_e = nrm(ks[21], (n_l, N_EXPERTS, D_EXPERT, D_MODEL), BETA * D_EXPERT ** -0.5)
    w_gate_s = nrm(ks[22], (n_l, D_MODEL, D_SHARED), D_MODEL ** -0.5)
    w_up_s = nrm(ks[23], (n_l, D_MODEL, D_SHARED), D_MODEL ** -0.5)
    w_down_s = nrm(ks[24], (n_l, D_SHARED, D_MODEL), BETA * D_SHARED ** -0.5)
    ln2_g = 1.0 + nrm(ks[25], (n_l, D_MODEL), 0.1)
    ln2_b = nrm(ks[26], (n_l, D_MODEL), 0.02)
    return {'x': x, 'c': c, 'positions': positions, 'w_ada': w_ada, 'b_ada': b_ada, 'w_in': w_in,
            'conv_w': conv_w, 'conv_b': conv_b, 'dt_bias': dt_bias, 'a_log': a_log, 'd_skip': d_skip,
            'ssd_norm_w': ssd_norm_w, 'idx_k_ln_g': idx_k_ln_g, 'idx_k_ln_b': idx_k_ln_b, 'w_out': w_out,
            'ln1_g': ln1_g, 'ln1_b': ln1_b, 'w_router': w_router, 'router_bias': router_bias,
            'w_gate_e': w_gate_e, 'w_up_e': w_up_e, 'w_down_e': w_down_e, 'w_gate_s': w_gate_s,
            'w_up_s': w_up_s, 'w_down_s': w_down_s, 'ln2_g': ln2_g, 'ln2_b': ln2_b}


def reference(x, c, positions, w_ada, b_ada, w_in, conv_w, conv_b, dt_bias, a_log, d_skip,
              ssd_norm_w, idx_k_ln_g, idx_k_ln_b, w_out, ln1_g, ln1_b, w_router, router_bias,
              w_gate_e, w_up_e, w_down_e, w_gate_s, w_up_s, w_down_s, ln2_g, ln2_b):
    cos_a, sin_a = rope_tables(positions, ATTN_HEAD_DIM)
    cos_i, sin_i = rope_tables(positions, IDX_DIM)
    c_act = jax.nn.silu(c)
    for l in range(DEPTH):
        mod = jnp.einsum('bd,de->be', c_act, w_ada[l]) + b_ada[l]
        shift1, scale1, gate1, shift2, scale2, gate2 = [m[:, None, :] for m in jnp.split(mod, 6, axis=-1)]
        u = x * (1 + scale1) + shift1
        mix = hybrid_mixer(u, cos_a, sin_a, cos_i, sin_i, w_in[l], conv_w[l], conv_b[l], dt_bias[l],
                           a_log[l], d_skip[l], ssd_norm_w[l], idx_k_ln_g[l], idx_k_ln_b[l], w_out[l])
        x = layer_norm(ALPHA * x + gate1 * mix, ln1_g[l], ln1_b[l])
        u = x * (1 + scale2) + shift2
        ffn = moe_ffn(u, w_router[l], router_bias[l], w_gate_e[l], w_up_e[l], w_down_e[l],
                      w_gate_s[l], w_up_s[l], w_down_s[l])
        x = layer_norm(ALPHA * x + gate2 * ffn, ln2_g[l], ln2_b[l])
    return x
```

```python
import functools
import math

import jax
import jax.numpy as jnp
import numpy as np
from jax import lax
from jax.experimental import pallas as pl
from jax.experimental.pallas import tpu as pltpu

F32 = jnp.float32
BF16 = jnp.bfloat16
I32 = jnp.int32
HIGHEST = lax.Precision.HIGHEST

D_MODEL = 2048
D_SSM = 1024
D_ATTN = 1024
SSM_HEAD_DIM = 64
SSM_HEADS = 16
SSM_GROUPS = 2
SSM_STATE = 128
CONV_WIDTH = 4
SSD_CHUNK = 128
ATTN_HEAD_DIM = 128
ATTN_HEADS = 8
IDX_HEADS = 16
IDX_DIM = 64
INDEX_TOPK = 256
ROPE_THETA = 10000.0
N_EXPERTS = 64
N_EXPERT_GROUPS = 8
EXPERTS_PER_GROUP = 8
TOPK_GROUPS = 4
TOP_K = 8
D_EXPERT = 512
D_SHARED = 512
ROUTED_SCALE = 2.5
DEPTH = 1
ALPHA = (2.0 * DEPTH) ** 0.25
LN_EPS = 1e-5

LANES = 128
SUBLANES = 8
VMEM_BYTES_V7X = 64 * 1024 * 1024
INT_MIN = -(2 ** 31)
KEY_LOWEST_FINITE = INT_MIN + 0x00800000

COL_Z, COL_XS, COL_Q, COL_K, COL_V, COL_QI, COL_BC = 0, 1024, 2048, 3072, 4096, 5120, 6144
N_MAIN = 6656
N_TAIL = 3 * LANES


def _cparams(sem, vmem_mb):
    return pltpu.CompilerParams(dimension_semantics=sem, vmem_limit_bytes=vmem_mb * 1024 * 1024)


def _silu(v):
    return v * (1.0 / (1.0 + jnp.exp(-v)))


def _dot(a, b, precision=None):
    return jnp.dot(a, b, preferred_element_type=F32, precision=precision)


def _dot_nt(a, b, precision=None):
    return lax.dot_general(a, b, (((1,), (1,)), ((), ())), preferred_element_type=F32, precision=precision)


def _layer_norm_rows(h, g, b):
    mu = jnp.mean(h, axis=-1, keepdims=True)
    d = h - mu
    var = jnp.mean(d * d, axis=-1, keepdims=True)
    return d * lax.rsqrt(var + LN_EPS) * g + b


def _ada_kernel(cb_ref, w_ref, b_ref, o_ref):
    tn = w_ref.shape[1]
    for bi in range(cb_ref.shape[0]):
        cb = _silu(cb_ref[bi])
        cols = [jnp.sum(w_ref[:, j * LANES:(j + 1) * LANES] * cb, axis=0, keepdims=True)
                for j in range(tn // LANES)]
        o_ref[bi:bi + 1, :] = jnp.concatenate(cols, axis=1) + b_ref[...]


def _ada(c, w_ada, b_ada):
    bsz, d = c.shape
    n = w_ada.shape[1]
    tn = 1024
    cb = jnp.broadcast_to(c[:, :, None], (bsz, d, LANES))
    return pl.pallas_call(
        _ada_kernel,
        grid=(n // tn,),
        in_specs=[pl.BlockSpec((bsz, d, LANES), lambda j: (0, 0, 0)),
                  pl.BlockSpec((d, tn), lambda j: (0, j)),
                  pl.BlockSpec((1, tn), lambda j: (0, j))],
        out_specs=pl.BlockSpec((bsz, tn), lambda j: (0, j)),
        out_shape=jax.ShapeDtypeStruct((bsz, n), F32),
        compiler_params=_cparams(("parallel",), 40),
        name="ada",
    )(cb, w_ada, b_ada.reshape(1, n))


def _inproj_kernel(x_ref, sc_ref, sh_ref, w_ref, wth_ref, wtl_ref, o_ref, t_ref, u_ref):
    @pl.when(pl.program_id(1) == 0)
    def _():
        u = x_ref[...] * (1.0 + sc_ref[0]) + sh_ref[0]
        uh = u.astype(BF16)
        ul = (u - uh.astype(F32)).astype(BF16)
        u_ref[...] = uh
        t_ref[...] = (_dot(uh, wth_ref[...]) + _dot(uh, wtl_ref[...]) + _dot(ul, wth_ref[...]))

    o_ref[...] = _dot(u_ref[...], w_ref[...])


def _inproj(x2, scale1, shift1, w_main, wt_hi, wt_lo, seq):
    t, d = x2.shape
    tm, tn = 1024, 512
    tpb = seq // tm
    return pl.pallas_call(
        _inproj_kernel,
        grid=(t // tm, N_MAIN // tn),
        in_specs=[pl.BlockSpec((tm, d), lambda i, j: (i, 0)),
                  pl.BlockSpec((1, 1, d), lambda i, j: (i // tpb, 0, 0)),
                  pl.BlockSpec((1, 1, d), lambda i, j: (i // tpb, 0, 0)),
                  pl.BlockSpec((d, tn), lambda i, j: (0, j)),
                  pl.BlockSpec((d, N_TAIL), lambda i, j: (0, 0)),
                  pl.BlockSpec((d, N_TAIL), lambda i, j: (0, 0))],
        out_specs=[pl.BlockSpec((tm, tn), lambda i, j: (i, j)),
                   pl.BlockSpec((tm, N_TAIL), lambda i, j: (i, 0))],
        out_shape=[jax.ShapeDtypeStruct((t, N_MAIN), F32),
                   jax.ShapeDtypeStruct((t, N_TAIL), F32)],
        scratch_shapes=[pltpu.VMEM((tm, d), BF16)],
        compiler_params=_cparams(("parallel", "arbitrary"), 48),
        name="inproj",
    )(x2, scale1, shift1, w_main, wt_hi, wt_lo)


def _ssm_kernel(z_ref, xs_ref, bc_ref, dt_ref, cwx_ref, cbx_ref, cwb_ref, cbb_ref, dtb_ref, alog_ref,
                dsk_ref, nw_ref, e_ref, e2_ref, tril_ref, o_ref, px_ref, pb_ref, st_ref):
    q = SSD_CHUNK
    hpg = SSM_HEADS // SSM_GROUPS
    gw = hpg * SSM_HEAD_DIM

    @pl.when(pl.program_id(1) == 0)
    def _():
        px_ref[0:SUBLANES, :] = jnp.zeros((SUBLANES, px_ref.shape[1]), F32)
        pb_ref[0:SUBLANES, :] = jnp.zeros((SUBLANES, pb_ref.shape[1]), F32)
        st_ref[...] = jnp.zeros(st_ref.shape, F32)

    def conv_silu(raw_ref, pad_ref, w_ref, b_ref):
        pad_ref[SUBLANES:SUBLANES + q, :] = raw_ref[...]
        acc = b_ref[...] + w_ref[0:1, :] * pad_ref[SUBLANES - 3:SUBLANES - 3 + q, :]
        for k in range(1, CONV_WIDTH):
            acc = acc + w_ref[k:k + 1, :] * pad_ref[SUBLANES - 3 + k:SUBLANES - 3 + k + q, :]
        pad_ref[0:SUBLANES, :] = raw_ref[q - SUBLANES:q, :]
        return _silu(acc)

    xs = conv_silu(xs_ref, px_ref, cwx_ref, cbx_ref)
    bc = conv_silu(bc_ref, pb_ref, cwb_ref, cbb_ref)

    dtr = dt_ref[...] + dtb_ref[...]
    dt = jnp.maximum(dtr, 0.0) + jnp.log(1.0 + jnp.exp(-jnp.abs(dtr)))
    log_a = dt * (-jnp.exp(alog_ref[...]))
    cs = _dot(tril_ref[...], log_a, HIGHEST)
    cs_e = _dot(cs, e_ref[...], HIGHEST)
    dt_e = _dot(dt, e_ref[...], HIGHEST)
    cs_col = _dot(cs, e2_ref[...], HIGHEST)
    cs_t = cs.T
    cs_last = cs_e[q - 1:q, :]

    xdt = xs * dt_e
    rows = lax.broadcasted_iota(I32, (q, q), 0)
    cols = lax.broadcasted_iota(I32, (q, q), 1)
    causal = rows >= cols
    first_half = lax.broadcasted_iota(I32, (q, LANES), 1) < SSM_HEAD_DIM

    y_pairs = []
    for g in range(SSM_GROUPS):
        b_g = bc[:, g * SSM_STATE:(g + 1) * SSM_STATE]
        c_g = bc[:, (SSM_GROUPS + g) * SSM_STATE:(SSM_GROUPS + g + 1) * SSM_STATE]
        cb = _dot_nt(c_g, b_g, HIGHEST)
        for hp in range(hpg // 2):
            pair = g * (hpg // 2) + hp
            x_pair = xdt[:, pair * LANES:(pair + 1) * LANES]
            ys = []
            for sub in range(2):
                h = 2 * pair + sub
                seg = cs_col[:, h * LANES:(h + 1) * LANES] - cs_t[h:h + 1, :]
                dec = jnp.exp(jnp.where(causal, seg, -jnp.inf))
                ys.append(_dot(cb * dec, x_pair, HIGHEST))
            y_pairs.append(jnp.where(first_half, ys[0], ys[1]))
    y_diag = jnp.concatenate(y_pairs, axis=1)

    xw = xdt * jnp.exp(cs_last - cs_e)
    y_off, new_states = [], []
    for g in range(SSM_GROUPS):
        b_g = bc[:, g * SSM_STATE:(g + 1) * SSM_STATE]
        c_g = bc[:, (SSM_GROUPS + g) * SSM_STATE:(SSM_GROUPS + g + 1) * SSM_STATE]
        h_in = st_ref[:, g * gw:(g + 1) * gw]
        y_off.append(_dot(c_g, h_in, HIGHEST))
        new_states.append(_dot(b_g.T, xw[:, g * gw:(g + 1) * gw], HIGHEST))
    y_off = jnp.concatenate(y_off, axis=1) * jnp.exp(cs_e)
    st_ref[...] = jnp.exp(cs_last) * st_ref[...] + jnp.concatenate(new_states, axis=1)

    y = y_diag + y_off + dsk_ref[...] * xs
    yf = y * _silu(z_ref[...])
    ms = jnp.mean(yf * yf, axis=-1, keepdims=True)
    o_ref[...] = (yf * lax.rsqrt(ms + LN_EPS) * nw_ref[...]).astype(o_ref.dtype)


def _ssm(proj, tail, conv_w, conv_b, dt_bias, a_log, d_skip, ssd_norm_w, bsz, seq):
    t = proj.shape[0]
    q = SSD_CHUNK
    n_c = seq // q
    nbc = 2 * SSM_GROUPS * SSM_STATE
    cw_x, cw_b = conv_w[:, :D_SSM], conv_w[:, D_SSM:]
    cb_x, cb_b = conv_b[None, :D_SSM], conv_b[None, D_SSM:]
    pad16 = lambda v: jnp.pad(v, (0, LANES - SSM_HEADS))[None, :]
    head_of_lane = np.arange(D_SSM) // SSM_HEAD_DIM
    e_mat = jnp.asarray((np.arange(LANES)[:, None] == head_of_lane[None, :]).astype(np.float32))
    e2_mat = jnp.asarray((np.arange(LANES)[:, None] == (np.arange(SSM_HEADS * LANES) // LANES)[None, :])
                         .astype(np.float32))
    tril = jnp.asarray(np.tril(np.ones((q, q), np.float32)))
    row = lambda b, c: b * n_c + c
    full = lambda shape: pl.BlockSpec(shape, lambda b, c: (0,) * len(shape))
    return pl.pallas_call(
        _ssm_kernel,
        grid=(bsz, n_c),
        in_specs=[pl.BlockSpec((q, D_SSM), lambda b, c: (row(b, c), COL_Z // D_SSM)),
                  pl.BlockSpec((q, D_SSM), lambda b, c: (row(b, c), COL_XS // D_SSM)),
                  pl.BlockSpec((q, nbc), lambda b, c: (row(b, c), COL_BC // nbc)),
                  pl.BlockSpec((q, LANES), lambda b, c: (row(b, c), 0)),
                  full((CONV_WIDTH, D_SSM)), full((1, D_SSM)), full((CONV_WIDTH, nbc)), full((1, nbc)),
                  full((1, LANES)), full((1, LANES)), full((1, D_SSM)), full((1, D_SSM)),
                  full((LANES, D_SSM)), full((LANES, SSM_HEADS * LANES)), full((q, q))],
        out_specs=pl.BlockSpec((q, D_SSM), lambda b, c: (row(b, c), 0)),
        out_shape=jax.ShapeDtypeStruct((t, D_SSM), BF16),
        scratch_shapes=[pltpu.VMEM((SUBLANES + q, D_SSM), F32),
                        pltpu.VMEM((SUBLANES + q, nbc), F32),
                        pltpu.VMEM((SSM_STATE, D_SSM), F32)],
        compiler_params=_cparams(("parallel", "arbitrary"), 40),
        name="ssm",
    )(proj, proj, proj, tail, cw_x, cb_x, cw_b, cb_b, pad16(dt_bias), pad16(a_log),
      jnp.repeat(d_skip, SSM_HEAD_DIM)[None, :], ssd_norm_w[None, :], e_mat, e2_mat, tril)


def _prep_kernel(q_ref, k_ref, v_ref, qi_ref, ki_ref, wi_ref, ca_ref, sa_ref, ci_ref, si_ref, lg_ref, lb_ref,
                 qo_ref, ko_ref, vo_ref, qio_ref, ki0_ref, ki1_ref, wo_ref):
    ca, sa, ci, si = ca_ref[...], sa_ref[...], ci_ref[...], si_ref[...]
    lane = lax.broadcasted_iota(I32, ca.shape, 1)
    first32 = (lane % IDX_DIM) < (IDX_DIM // 2)
    q_scale = ATTN_HEAD_DIM ** -0.5

    def rope_attn(v):
        return v * ca + pltpu.roll(v, ATTN_HEAD_DIM // 2, 1) * sa

    def rope_idx(v):
        rot = jnp.where(first32, pltpu.roll(v, LANES - IDX_DIM // 2, 1), pltpu.roll(v, IDX_DIM // 2, 1))
        return v * ci + rot * si

    for h in range(ATTN_HEADS):
        sl = slice(h * LANES, (h + 1) * LANES)
        qo_ref[:, sl] = (rope_attn(q_ref[:, sl]) * q_scale).astype(BF16)
        ko_ref[:, sl] = rope_attn(k_ref[:, sl]).astype(BF16)
    vo_ref[0] = v_ref[...].T.astype(BF16)
    for p in range(IDX_HEADS * IDX_DIM // LANES):
        sl = slice(p * LANES, (p + 1) * LANES)
        qio_ref[:, sl] = rope_idx(qi_ref[:, sl]).astype(BF16)

    kraw = ki_ref[...]
    valid = lane < IDX_DIM
    mu = jnp.sum(kraw, axis=-1, keepdims=True) * (1.0 / IDX_DIM)
    dk = jnp.where(valid, kraw - mu, 0.0)
    var = jnp.sum(dk * dk, axis=-1, keepdims=True) * (1.0 / IDX_DIM)
    kn = jnp.where(valid, dk * lax.rsqrt(var + LN_EPS) * lg_ref[...] + lb_ref[...], 0.0)
    kr = jnp.where(valid, rope_idx(kn), 0.0)
    ki0_ref[...] = kr.astype(BF16)
    ki1_ref[...] = pltpu.roll(kr, IDX_DIM, 1).astype(BF16)
    wo_ref[...] = wi_ref[...] * (IDX_HEADS ** -0.5 * IDX_DIM ** -0.5)


def _prep(proj, tail, cos_a, sin_a, cos_i, sin_i, ln_g, ln_b):
    t = proj.shape[0]
    tm = ATT_KC
    w = D_ATTN
    pad64 = lambda v: jnp.pad(v, (0, LANES - IDX_DIM))[None, :]
    col = lambda c: pl.BlockSpec((tm, w), lambda i: (i, c // w))
    lane_blk = lambda c: pl.BlockSpec((tm, LANES), lambda i: (i, c))
    full = lambda: pl.BlockSpec((1, LANES), lambda i: (0, 0))
    return pl.pallas_call(
        _prep_kernel,
        grid=(t // tm,),
        in_specs=[col(COL_Q), col(COL_K), col(COL_V), col(COL_QI), lane_blk(1), lane_blk(2),
                  lane_blk(0), lane_blk(0), lane_blk(0), lane_blk(0), full(), full()],
        out_specs=[pl.BlockSpec((tm, w), lambda i: (i, 0))] * 2 + [pl.BlockSpec((1, w, tm), lambda i: (i, 0, 0))]
        + [pl.BlockSpec((tm, w), lambda i: (i, 0))] + [pl.BlockSpec((tm, LANES), lambda i: (i, 0))] * 3,
        out_shape=[jax.ShapeDtypeStruct((t, w), BF16)] * 2 + [jax.ShapeDtypeStruct((t // tm, w, tm), BF16)]
        + [jax.ShapeDtypeStruct((t, w), BF16)]
        + [jax.ShapeDtypeStruct((t, LANES), BF16)] * 2 + [jax.ShapeDtypeStruct((t, LANES), F32)],
        compiler_params=_cparams(("parallel",), 48),
        name="prep",
    )(proj, proj, proj, proj, tail, tail, cos_a, sin_a, cos_i, sin_i, pad64(ln_g), pad64(ln_b))


ATT_QB = 128
ATT_KC = 512


def _attn_kernel(q_ref, k_ref, vt_ref, qi_ref, ki0_ref, ki1_ref, w_ref, o_ref, sc_ref, m_ref, l_ref, acc_ref,
                 *, topk):
    i = pl.program_id(1)
    n_ch = (i * ATT_QB + ATT_QB + ATT_KC - 1) // ATT_KC
    key_off = lax.broadcasted_iota(I32, (ATT_KC, ATT_QB), 0)
    q_pos = i * ATT_QB + lax.broadcasted_iota(I32, (ATT_KC, ATT_QB), 1)
    w_t = w_ref[...].T

    def score_chunk(c, carry):
        start = pl.multiple_of(c * ATT_KC, ATT_KC)
        ki0 = ki0_ref[pl.ds(start, ATT_KC), :]
        ki1 = ki1_ref[pl.ds(start, ATT_KC), :]
        acc = jnp.zeros((ATT_KC, ATT_QB), F32)
        for h in range(IDX_HEADS):
            pair = qi_ref[:, (h // 2) * LANES:(h // 2 + 1) * LANES]
            rel = _dot_nt(ki0 if h % 2 == 0 else ki1, pair)
            acc = acc + jnp.maximum(rel, 0.0) * w_t[h:h + 1, :]
        sc_ref[c] = jnp.where(start + key_off <= q_pos, acc, -jnp.inf)
        return carry

    lax.fori_loop(0, n_ch, score_chunk, 0)

    def key_to_float(key):
        return pltpu.bitcast(jnp.where(key < 0, key ^ jnp.int32(0x7FFFFFFF), key), F32)

    def search_bit(it, ans):
        cand = ans ^ lax.shift_left(jnp.int32(1), jnp.int32(31) - it)
        cand_f = key_to_float(cand)

        def count_chunk(c, cnt):
            part = jnp.where(sc_ref[c] >= cand_f, 1, 0).astype(I32)
            return cnt + jnp.sum(part.reshape(ATT_KC // SUBLANES, SUBLANES, ATT_QB), axis=0)

        cnt = lax.fori_loop(0, n_ch, count_chunk, jnp.zeros((SUBLANES, ATT_QB), I32))
        total = jnp.sum(cnt, axis=0, keepdims=True)
        return jnp.where(total >= topk, cand, ans)

    kth = lax.fori_loop(0, 32, search_bit, jnp.full((1, ATT_QB), INT_MIN, I32))
    lowest_finite = jnp.float32(np.finfo(np.float32).min)
    thr = jnp.where(kth < KEY_LOWEST_FINITE, lowest_finite, key_to_float(kth))

    m_ref[...] = jnp.full(m_ref.shape, -1e30, F32)
    l_ref[...] = jnp.zeros(l_ref.shape, F32)
    acc_ref[...] = jnp.zeros(acc_ref.shape, F32)

    def att_chunk(c, carry):
        start = pl.multiple_of(c * ATT_KC, ATT_KC)
        sel = sc_ref[c] >= thr
        for h in range(ATTN_HEADS):
            sl = slice(h * LANES, (h + 1) * LANES)
            s = _dot_nt(k_ref[pl.ds(start, ATT_KC), sl], q_ref[:, sl])
            s = jnp.where(sel, s, -jnp.inf)
            m_old = m_ref[h:h + 1, :]
            m_new = jnp.maximum(m_old, jnp.max(s, axis=0, keepdims=True))
            p = jnp.exp(s - m_new)
            alpha = jnp.exp(m_old - m_new)
            l_ref[h:h + 1, :] = alpha * l_ref[h:h + 1, :] + jnp.sum(p, axis=0, keepdims=True)
            m_ref[h:h + 1, :] = m_new
            acc_ref[h] = alpha * acc_ref[h] + _dot(vt_ref[c, sl, :], p.astype(BF16))
        return carry

    lax.fori_loop(0, n_ch, att_chunk, 0)
    for h in range(ATTN_HEADS):
        o_ref[:, h * LANES:(h + 1) * LANES] = (acc_ref[h] / l_ref[h:h + 1, :]).T.astype(o_ref.dtype)


def _attn(q_r, k_r, v_t, qi_r, ki0, ki1, w_s, bsz, seq):
    t = q_r.shape[0]
    nq = seq // ATT_QB
    n_kc = seq // ATT_KC
    topk = min(INDEX_TOPK, seq // 4)
    once = pl.Buffered(1)
    qblk = lambda width: pl.BlockSpec((ATT_QB, width), lambda b, i: (b * nq + i, 0))
    per_batch = lambda width: pl.BlockSpec((seq, width), lambda b, i: (b, 0), pipeline_mode=once)
    return pl.pallas_call(
        functools.partial(_attn_kernel, topk=topk),
        grid=(bsz, nq),
        in_specs=[qblk(D_ATTN), per_batch(D_ATTN),
                  pl.BlockSpec((n_kc, D_ATTN, ATT_KC), lambda b, i: (b, 0, 0), pipeline_mode=once),
                  qblk(IDX_HEADS * IDX_DIM), per_batch(LANES), per_batch(LANES), qblk(LANES)],
        out_specs=qblk(D_ATTN),
        out_shape=jax.ShapeDtypeStruct((t, D_ATTN), BF16),
        scratch_shapes=[pltpu.VMEM((n_kc, ATT_KC, ATT_QB), F32),
                        pltpu.VMEM((ATTN_HEADS, ATT_QB), F32),
                        pltpu.VMEM((ATTN_HEADS, ATT_QB), F32),
                        pltpu.VMEM((ATTN_HEADS, ATTN_HEAD_DIM, ATT_QB), F32)],
        compiler_params=_cparams(("parallel", "arbitrary"), 48),
        name="attn",
    )(q_r, k_r, v_t, qi_r, ki0, ki1, w_s)


def _first_index_of_max(v, iota, n):
    m = jnp.max(v, axis=0, keepdims=True)
    idx = jnp.min(jnp.where(v == m, iota, n), axis=0, keepdims=True)
    return m, idx


def _route(scores, biased):
    tm = scores.shape[1]
    neg_inf = jnp.float32(-jnp.inf)
    iota8 = lax.broadcasted_iota(I32, (EXPERTS_PER_GROUP, tm), 0)
    group_scores = []
    for g in range(N_EXPERT_GROUPS):
        v = biased[g * EXPERTS_PER_GROUP:(g + 1) * EXPERTS_PER_GROUP, :]
        m1, i1 = _first_index_of_max(v, iota8, EXPERTS_PER_GROUP)
        m2 = jnp.max(jnp.where(iota8 == i1, neg_inf, v), axis=0, keepdims=True)
        group_scores.append(m1 + m2)
    gs = jnp.concatenate(group_scores, axis=0)
    keep = jnp.zeros(gs.shape, jnp.bool_)
    for _ in range(TOPK_GROUPS):
        _, gi = _first_index_of_max(gs, iota8, N_EXPERT_GROUPS)
        hit = iota8 == gi
        keep = jnp.logical_or(keep, hit)
        gs = jnp.where(hit, neg_inf, gs)
    keep_f = jnp.where(keep, 1.0, 0.0)
    masked = jnp.concatenate(
        [jnp.where(keep_f[g:g + 1, :] > 0.0, biased[g * EXPERTS_PER_GROUP:(g + 1) * EXPERTS_PER_GROUP, :], neg_inf)
         for g in range(N_EXPERT_GROUPS)], axis=0)
    iota64 = lax.broadcasted_iota(I32, (N_EXPERTS, tm), 0)
    sel_w = jnp.zeros((N_EXPERTS, tm), F32)
    for _ in range(TOP_K):
        _, ei = _first_index_of_max(masked, iota64, N_EXPERTS)
        hit = iota64 == ei
        sel_w = jnp.where(hit, scores, sel_w)
        masked = jnp.where(hit, neg_inf, masked)
    denom = jnp.sum(sel_w, axis=0, keepdims=True)
    return sel_w / denom * ROUTED_SCALE


def _outproj_kernel(ys_ref, ya_ref, wa_ref, wb_ref, x_ref, g1_ref, sc2_ref, sh2_ref, lg_ref, lb_ref,
                    wrh_ref, wrl_ref, rb_ref, x1_ref, u2_ref, gt_ref):
    mix = _dot(ys_ref[...], wa_ref[...]) + _dot(ya_ref[...], wb_ref[...])
    h = ALPHA * x_ref[...] + g1_ref[0] * mix
    x1 = _layer_norm_rows(h, lg_ref[...], lb_ref[...])
    x1_ref[...] = x1
    u2 = x1 * (1.0 + sc2_ref[0]) + sh2_ref[0]
    uh = u2.astype(BF16)
    ul = (u2 - uh.astype(F32)).astype(BF16)
    u2_ref[...] = uh
    wrh = wrh_ref[...]
    logits = _dot_nt(wrh, uh) + _dot_nt(wrh, ul) + _dot_nt(wrl_ref[...], uh)
    scores = 1.0 / (1.0 + jnp.exp(-logits))
    gates = _route(scores, scores + rb_ref[:, 0:1])
    gt_ref[...] = jnp.concatenate([gates, jnp.zeros_like(gates)], axis=0).T


def _outproj(y_ssm, y_att, wo_a, wo_b, x2, gate1, scale2, shift2, ln_g, ln_b, wr_hi, wr_lo, rbias, seq):
    t, d = x2.shape
    tm = 512
    tpb = seq // tm
    half = y_ssm.shape[1]
    rows = lambda width: pl.BlockSpec((tm, width), lambda i: (i, 0))
    full = lambda shape: pl.BlockSpec(shape, lambda i: (0,) * len(shape))
    mod = lambda: pl.BlockSpec((1, 1, d), lambda i: (i // tpb, 0, 0))
    return pl.pallas_call(
        _outproj_kernel,
        grid=(t // tm,),
        in_specs=[rows(half), rows(half), full((half, d)), full((half, d)), rows(d), mod(), mod(), mod(),
                  full((1, d)), full((1, d)), full((N_EXPERTS, d)), full((N_EXPERTS, d)), full((N_EXPERTS, LANES))],
        out_specs=[rows(d), rows(d), rows(LANES)],
        out_shape=[jax.ShapeDtypeStruct((t, d), F32), jax.ShapeDtypeStruct((t, d), BF16),
                   jax.ShapeDtypeStruct((t, LANES), F32)],
        compiler_params=_cparams(("parallel",), 48),
        name="outproj",
    )(y_ssm, y_att, wo_a, wo_b, x2, gate1, scale2, shift2, ln_g[None, :], ln_b[None, :], wr_hi, wr_lo, rbias)


def _moe_kernel(u_ref, g_ref, wg_ref, wu_ref, wd_ref, o_ref):
    e = pl.program_id(1)

    @pl.when(e == 0)
    def _():
        o_ref[...] = jnp.zeros(o_ref.shape, F32)

    u = u_ref[...]
    hg = _dot(u, wg_ref[0])
    hu = _dot(u, wu_ref[0])
    g = g_ref[...]
    lane = lax.broadcasted_iota(I32, g.shape, 1)
    gate = jnp.sum(jnp.where(lane == e, g, 0.0), axis=1, keepdims=True)
    hid = (_silu(hg) * hu * gate).astype(BF16)
    o_ref[...] += _dot(hid, wd_ref[0])


def _moe(u2, gates, wg, wu, wd):
    t, d = u2.shape
    tm = 1024
    f = wg.shape[2]
    return pl.pallas_call(
        _moe_kernel,
        grid=(t // tm, N_EXPERTS),
        in_specs=[pl.BlockSpec((tm, d), lambda i, e: (i, 0)),
                  pl.BlockSpec((tm, LANES), lambda i, e: (i, 0)),
                  pl.BlockSpec((1, d, f), lambda i, e: (e, 0, 0)),
                  pl.BlockSpec((1, d, f), lambda i, e: (e, 0, 0)),
                  pl.BlockSpec((1, f, d), lambda i, e: (e, 0, 0))],
        out_specs=pl.BlockSpec((tm, d), lambda i, e: (i, 0)),
        out_shape=jax.ShapeDtypeStruct((t, d), F32),
        compiler_params=_cparams(("parallel", "arbitrary"), 56),
        name="moe",
    )(u2, gates, wg, wu, wd)


def _final_kernel(u_ref, wg_ref, wu_ref, wd_ref, r_ref, x1_ref, g2_ref, lg_ref, lb_ref, o_ref):
    u = u_ref[...]
    hid = (_silu(_dot(u, wg_ref[...])) * _dot(u, wu_ref[...])).astype(BF16)
    ffn = r_ref[...] + _dot(hid, wd_ref[...])
    h = ALPHA * x1_ref[...] + g2_ref[0] * ffn
    o_ref[...] = _layer_norm_rows(h, lg_ref[...], lb_ref[...])


def _final(u2, wg_s, wu_s, wd_s, routed, x1, gate2, ln_g, ln_b, seq):
    t, d = x1.shape
    tm = 512
    tpb = seq // tm
    f = wg_s.shape[1]
    rows = lambda: pl.BlockSpec((tm, d), lambda i: (i, 0))
    full = lambda shape: pl.BlockSpec(shape, lambda i: (0,) * len(shape))
    return pl.pallas_call(
        _final_kernel,
        grid=(t // tm,),
        in_specs=[rows(), full((d, f)), full((d, f)), full((f, d)), rows(), rows(),
                  pl.BlockSpec((1, 1, d), lambda i: (i // tpb, 0, 0)), full((1, d)), full((1, d))],
        out_specs=rows(),
        out_shape=jax.ShapeDtypeStruct((t, d), F32),
        compiler_params=_cparams(("parallel",), 48),
        name="final",
    )(u2, wg_s, wu_s, wd_s, routed, x1, gate2, ln_g[None, :], ln_b[None, :])


def _split_hi_lo(w):
    hi = w.astype(BF16)
    lo = (w - hi.astype(F32)).astype(BF16)
    return hi, lo


def _rope_tables(positions, dim, reps):
    inv_freq = 1.0 / (ROPE_THETA ** (jnp.arange(0, dim, 2, dtype=F32) / dim))
    ang = positions.astype(F32).reshape(-1)[:, None] * inv_freq
    cos, sin = jnp.cos(ang), jnp.sin(ang)
    return (jnp.tile(jnp.concatenate([cos, cos], axis=1), (1, reps)),
            jnp.tile(jnp.concatenate([-sin, sin], axis=1), (1, reps)))


def _layer(x2, c, positions, bsz, seq, w_ada, b_ada, w_in, conv_w, conv_b, dt_bias, a_log, d_skip, ssd_norm_w,
           idx_k_ln_g, idx_k_ln_b, w_out, ln1_g, ln1_b, w_router, router_bias, w_gate_e, w_up_e, w_down_e,
           w_gate_s, w_up_s, w_down_s, ln2_g, ln2_b, tables):
    d = D_MODEL
    mod = _ada(c, w_ada, b_ada).reshape(bsz, 6, 1, d)
    shift1, scale1, gate1, shift2, scale2, gate2 = [mod[:, k] for k in range(6)]

    o_z, o_xbc, o_dt, o_q = 0, D_SSM, 2 * D_SSM + 2 * SSM_GROUPS * SSM_STATE, 2 * D_SSM + 2 * SSM_GROUPS * SSM_STATE + SSM_HEADS
    o_k, o_v, o_qi = o_q + D_ATTN, o_q + 2 * D_ATTN, o_q + 3 * D_ATTN
    o_ki = o_qi + IDX_HEADS * IDX_DIM
    o_wi = o_ki + IDX_DIM
    w_main = jnp.concatenate([w_in[:, o_z:o_z + D_SSM], w_in[:, o_xbc:o_xbc + D_SSM], w_in[:, o_q:o_ki],
                              w_in[:, o_xbc + D_SSM:o_dt]], axis=1).astype(BF16)
    padc = lambda w: jnp.pad(w, ((0, 0), (0, LANES - w.shape[1])))
    w_tail = jnp.concatenate([padc(w_in[:, o_dt:o_q]), padc(w_in[:, o_ki:o_wi]), padc(w_in[:, o_wi:])], axis=1)
    wt_hi, wt_lo = _split_hi_lo(w_tail)

    proj, tail = _inproj(x2, scale1, shift1, w_main, wt_hi, wt_lo, seq)
    y_ssm = _ssm(proj, tail, conv_w, conv_b, dt_bias, a_log, d_skip, ssd_norm_w, bsz, seq)
    q_r, k_r, v_t, qi_r, ki0, ki1, w_s = _prep(proj, tail, *tables, idx_k_ln_g, idx_k_ln_b)
    y_att = _attn(q_r, k_r, v_t, qi_r, ki0, ki1, w_s, bsz, seq)

    wo = w_out.astype(BF16)
    wr_hi, wr_lo = _split_hi_lo(w_router.T)
    rbias = jnp.broadcast_to(router_bias[:, None], (N_EXPERTS, LANES))
    x1, u2, gates = _outproj(y_ssm, y_att, wo[:D_SSM], wo[D_SSM:], x2, gate1, scale2, shift2, ln1_g, ln1_b,
                             wr_hi, wr_lo, rbias, seq)
    routed = _moe(u2, gates, w_gate_e.astype(BF16), w_up_e.astype(BF16), w_down_e.astype(BF16))
    return _final(u2, w_gate_s.astype(BF16), w_up_s.astype(BF16), w_down_s.astype(BF16), routed, x1, gate2,
                  ln2_g, ln2_b, seq)


def kernel(x, c, positions, w_ada, b_ada, w_in, conv_w, conv_b, dt_bias, a_log, d_skip, ssd_norm_w, idx_k_ln_g, idx_k_ln_b, w_out, ln1_g, ln1_b, w_router, router_bias, w_gate_e, w_up_e, w_down_e, w_gate_s, w_up_s, w_down_s, ln2_g, ln2_b):
    bsz, seq, d = x.shape
    tables = _rope_tables(positions, ATTN_HEAD_DIM, 1) + _rope_tables(positions, IDX_DIM, 2)
    x2 = x.reshape(bsz * seq, d)
    for l in range(w_ada.shape[0]):
        x2 = _layer(x2, c, positions, bsz, seq, w_ada[l], b_ada[l], w_in[l], conv_w[l], conv_b[l], dt_bias[l],
                    a_log[l], d_skip[l], ssd_norm_w[l], idx_k_ln_g[l], idx_k_ln_b[l], w_out[l], ln1_g[l], ln1_b[l],
                    w_router[l], router_bias[l], w_gate_e[l], w_up_e[l], w_down_e[l], w_gate_s[l], w_up_s[l],
                    w_down_s[l], ln2_g[l], ln2_b[l], tables)
    return x2.reshape(bsz, seq, d)
```

```python
import functools
import math

import jax
import jax.numpy as jnp
import numpy as np
from jax import lax
from jax.experimental import pallas as pl
from jax.experimental.pallas import tpu as pltpu

F32 = jnp.float32
BF16 = jnp.bfloat16
I32 = jnp.int32
HIGHEST = lax.Precision.HIGHEST

D_MODEL = 2048
D_SSM = 1024
D_ATTN = 1024
SSM_HEAD_DIM = 64
SSM_HEADS = 16
SSM_GROUPS = 2
SSM_STATE = 128
CONV_WIDTH = 4
SSD_CHUNK = 128
ATTN_HEAD_DIM = 128
ATTN_HEADS = 8
IDX_HEADS = 16
IDX_DIM = 64
INDEX_TOPK = 256
ROPE_THETA = 10000.0
N_EXPERTS = 64
N_EXPERT_GROUPS = 8
EXPERTS_PER_GROUP = 8
TOPK_GROUPS = 4
TOP_K = 8
D_EXPERT = 512
D_SHARED = 512
ROUTED_SCALE = 2.5
DEPTH = 1
ALPHA = (2.0 * DEPTH) ** 0.25
LN_EPS = 1e-5

LANES = 128
SUBLANES = 8
VMEM_BYTES_V7X = 64 * 1024 * 1024
INT_MIN = -(2 ** 31)
KEY_LOWEST_FINITE = INT_MIN + 0x00800000

COL_Z, COL_XS, COL_Q, COL_K, COL_V, COL_QI, COL_BC = 0, 1024, 2048, 3072, 4096, 5120, 6144
N_MAIN = 6656
N_TAIL = 3 * LANES


def _cparams(sem, vmem_mb):
    return pltpu.CompilerParams(dimension_semantics=sem, vmem_limit_bytes=vmem_mb * 1024 * 1024)


def _silu(v):
    return v * (1.0 / (1.0 + jnp.exp(-v)))


def _dot(a, b, precision=None):
    return jnp.dot(a, b, preferred_element_type=F32, precision=precision)


def _dot_nt(a, b, precision=None):
    return lax.dot_general(a, b, (((1,), (1,)), ((), ())), preferred_element_type=F32, precision=precision)


def _layer_norm_rows(h, g, b):
    mu = jnp.mean(h, axis=-1, keepdims=True)
    d = h - mu
    var = jnp.mean(d * d, axis=-1, keepdims=True)
    return d * lax.rsqrt(var + LN_EPS) * g + b


def _ada_kernel(cb_ref, w_ref, b_ref, o_ref):
    tn = w_ref.shape[1]
    for bi in range(cb_ref.shape[0]):
        cb = _silu(cb_ref[bi])
        cols = [jnp.sum(w_ref[:, j * LANES:(j + 1) * LANES] * cb, axis=0, keepdims=True)
                for j in range(tn // LANES)]
        o_ref[bi:bi + 1, :] = jnp.concatenate(cols, axis=1) + b_ref[...]


def _ada(c, w_ada, b_ada):
    bsz, d = c.shape
    n = w_ada.shape[1]
    tn = 1024
    cb = jnp.broadcast_to(c[:, :, None], (bsz, d, LANES))
    return pl.pallas_call(
        _ada_kernel,
        grid=(n // tn,),
        in_specs=[pl.BlockSpec((bsz, d, LANES), lambda j: (0, 0, 0)),
                  pl.BlockSpec((d, tn), lambda j: (0, j)),
                  pl.BlockSpec((1, tn), lambda j: (0, j))],
        out_specs=pl.BlockSpec((bsz, tn), lambda j: (0, j)),
        out_shape=jax.ShapeDtypeStruct((bsz, n), F32),
        compiler_params=_cparams(("parallel",), 40),
        name="ada",
    )(cb, w_ada, b_ada.reshape(1, n))


def _inproj_kernel(x_ref, sc_ref, sh_ref, w_ref, wth_ref, wtl_ref, o_ref, t_ref, u_ref):
    @pl.when(pl.program_id(1) == 0)
    def _():
        u = x_ref[...] * (1.0 + sc_ref[0]) + sh_ref[0]
        uh = u.astype(BF16)
        ul = (u - uh.astype(F32)).astype(BF16)
        u_ref[...] = uh
        t_ref[...] = (_dot(uh, wth_ref[...]) + _dot(uh, wtl_ref[...]) + _dot(ul, wth_ref[...]))

    o_ref[...] = _dot(u_ref[...], w_ref[...])


def _inproj(x2, scale1, shift1, w_main, wt_hi, wt_lo, seq):
    t, d = x2.shape
    tm, tn = 1024, 512
    tpb = seq // tm
    return pl.pallas_call(
        _inproj_kernel,
        grid=(t // tm, N_MAIN // tn),
        in_specs=[pl.BlockSpec((tm, d), lambda i, j: (i, 0)),
                  pl.BlockSpec((1, 1, d), lambda i, j: (i // tpb, 0, 0)),
                  pl.BlockSpec((1, 1, d), lambda i, j: (i // tpb, 0, 0)),
                  pl.BlockSpec((d, tn), lambda i, j: (0, j)),
                  pl.BlockSpec((d, N_TAIL), lambda i, j: (0, 0)),
                  pl.BlockSpec((d, N_TAIL), lambda i, j: (0, 0))],
        out_specs=[pl.BlockSpec((tm, tn), lambda i, j: (i, j)),
                   pl.BlockSpec((tm, N_TAIL), lambda i, j: (i, 0))],
        out_shape=[jax.ShapeDtypeStruct((t, N_MAIN), F32),
                   jax.ShapeDtypeStruct((t, N_TAIL), F32)],
        scratch_shapes=[pltpu.VMEM((tm, d), BF16)],
        compiler_params=_cparams(("parallel", "arbitrary"), 48),
        name="inproj",
    )(x2, scale1, shift1, w_main, wt_hi, wt_lo)


def _ssm_kernel(z_ref, xs_ref, bc_ref, dt_ref, cwx_ref, cbx_ref, cwb_ref, cbb_ref, dtb_ref, alog_ref,
                dsk_ref, nw_ref, e_ref, e2_ref, tril_ref, o_ref, px_ref, pb_ref, st_ref):
    q = SSD_CHUNK
    hpg = SSM_HEADS // SSM_GROUPS
    gw = hpg * SSM_HEAD_DIM

    @pl.when(pl.program_id(1) == 0)
    def _():
        px_ref[0:SUBLANES, :] = jnp.zeros((SUBLANES, px_ref.shape[1]), F32)
        pb_ref[0:SUBLANES, :] = jnp.zeros((SUBLANES, pb_ref.shape[1]), F32)
        st_ref[...] = jnp.zeros(st_ref.shape, F32)

    def conv_silu(raw_ref, pad_ref, w_ref, b_ref):
        pad_ref[SUBLANES:SUBLANES + q, :] = raw_ref[...]
        acc = b_ref[...] + w_ref[0:1, :] * pad_ref[SUBLANES - 3:SUBLANES - 3 + q, :]
        for k in range(1, CONV_WIDTH):
            acc = acc + w_ref[k:k + 1, :] * pad_ref[SUBLANES - 3 + k:SUBLANES - 3 + k + q, :]
        pad_ref[0:SUBLANES, :] = raw_ref[q - SUBLANES:q, :]
        return _silu(acc)

    xs = conv_silu(xs_ref, px_ref, cwx_ref, cbx_ref)
    bc = conv_silu(bc_ref, pb_ref, cwb_ref, cbb_ref)

    dtr = dt_ref[...] + dtb_ref[...]
    dt = jnp.maximum(dtr, 0.0) + jnp.log(1.0 + jnp.exp(-jnp.abs(dtr)))
    log_a = dt * (-jnp.exp(alog_ref[...]))
    cs = _dot(tril_ref[...], log_a, HIGHEST)
    cs_e = _dot(cs, e_ref[...], HIGHEST)
    dt_e = _dot(dt, e_ref[...], HIGHEST)
    cs_col = _dot(cs, e2_ref[...], HIGHEST)
    cs_t = cs.T
    cs_last = cs_e[q - 1:q, :]

    xdt = xs * dt_e
    rows = lax.broadcasted_iota(I32, (q, q), 0)
    cols = lax.broadcasted_iota(I32, (q, q), 1)
    causal = rows >= cols
    first_half = lax.broadcasted_iota(I32, (q, LANES), 1) < SSM_HEAD_DIM

    y_pairs = []
    for g in range(SSM_GROUPS):
        b_g = bc[:, g * SSM_STATE:(g + 1) * SSM_STATE]
        c_g = bc[:, (SSM_GROUPS + g) * SSM_STATE:(SSM_GROUPS + g + 1) * SSM_STATE]
        cb = _dot_nt(c_g, b_g, HIGHEST)
        for hp in range(hpg // 2):
            pair = g * (hpg // 2) + hp
            x_pair = xdt[:, pair * LANES:(pair + 1) * LANES]
            ys = []
            for sub in range(2):
                h = 2 * pair + sub
                seg = cs_col[:, h * LANES:(h + 1) * LANES] - cs_t[h:h + 1, :]
                dec = jnp.exp(jnp.where(causal, seg, -jnp.inf))
                ys.append(_dot(cb * dec, x_pair, HIGHEST))
            y_pairs.append(jnp.where(first_half, ys[0], ys[1]))
    y_diag = jnp.concatenate(y_pairs, axis=1)

    xw = xdt * jnp.exp(cs_last - cs_e)
    y_off, new_states = [], []
    for g in range(SSM_GROUPS):
        b_g = bc[:, g * SSM_STATE:(g + 1) * SSM_STATE]
        c_g = bc[:, (SSM_GROUPS + g) * SSM_STATE:(SSM_GROUPS + g + 1) * SSM_STATE]
        h_in = st_ref[:, g * gw:(g + 1) * gw]
        y_off.append(_dot(c_g, h_in, HIGHEST))
        new_states.append(_dot(b_g.T, xw[:, g * gw:(g + 1) * gw], HIGHEST))
    y_off = jnp.concatenate(y_off, axis=1) * jnp.exp(cs_e)
    st_ref[...] = jnp.exp(cs_last) * st_ref[...] + jnp.concatenate(new_states, axis=1)

    y = y_diag + y_off + dsk_ref[...] * xs
    yf = y * _silu(z_ref[...])
    ms = jnp.mean(yf * yf, axis=-1, keepdims=True)
    o_ref[...] = (yf * lax.rsqrt(ms + LN_EPS) * nw_ref[...]).astype(o_ref.dtype)


def _ssm(proj, tail, conv_w, conv_b, dt_bias, a_log, d_skip, ssd_norm_w, bsz, seq):
    t = proj.shape[0]
    q = SSD_CHUNK
    n_c = seq // q
    nbc = 2 * SSM_GROUPS * SSM_STATE
    cw_x, cw_b = conv_w[:, :D_SSM], conv_w[:, D_SSM:]
    cb_x, cb_b = conv_b[None, :D_SSM], conv_b[None, D_SSM:]
    pad16 = lambda v: jnp.pad(v, (0, LANES - SSM_HEADS))[None, :]
    head_of_lane = np.arange(D_SSM) // SSM_HEAD_DIM
    e_mat = jnp.asarray((np.arange(LANES)[:, None] == head_of_lane[None, :]).astype(np.float32))
    e2_mat = jnp.asarray((np.arange(LANES)[:, None] == (np.arange(SSM_HEADS * LANES) // LANES)[None, :])
                         .astype(np.float32))
    tril = jnp.asarray(np.tril(np.ones((q, q), np.float32)))
    row = lambda b, c: b * n_c + c
    full = lambda shape: pl.BlockSpec(shape, lambda b, c: (0,) * len(shape))
    return pl.pallas_call(
        _ssm_kernel,
        grid=(bsz, n_c),
        in_specs=[pl.BlockSpec((q, D_SSM), lambda b, c: (row(b, c), COL_Z // D_SSM)),
                  pl.BlockSpec((q, D_SSM), lambda b, c: (row(b, c), COL_XS // D_SSM)),
                  pl.BlockSpec((q, nbc), lambda b, c: (row(b, c), COL_BC // nbc)),
                  pl.BlockSpec((q, LANES), lambda b, c: (row(b, c), 0)),
                  full((CONV_WIDTH, D_SSM)), full((1, D_SSM)), full((CONV_WIDTH, nbc)), full((1, nbc)),
                  full((1, LANES)), full((1, LANES)), full((1, D_SSM)), full((1, D_SSM)),
                  full((LANES, D_SSM)), full((LANES, SSM_HEADS * LANES)), full((q, q))],
        out_specs=pl.BlockSpec((q, D_SSM), lambda b, c: (row(b, c), 0)),
        out_shape=jax.ShapeDtypeStruct((t, D_SSM), BF16),
        scratch_shapes=[pltpu.VMEM((SUBLANES + q, D_SSM), F32),
                        pltpu.VMEM((SUBLANES + q, nbc), F32),
                        pltpu.VMEM((SSM_STATE, D_SSM), F32)],
        compiler_params=_cparams(("parallel", "arbitrary"), 40),
        name="ssm",
    )(proj, proj, proj, tail, cw_x, cb_x, cw_b, cb_b, pad16(dt_bias), pad16(a_log),
      jnp.repeat(d_skip, SSM_HEAD_DIM)[None, :], ssd_norm_w[None, :], e_mat, e2_mat, tril)


def _prep_kernel(q_ref, k_ref, v_ref, qi_ref, ki_ref, wi_ref, ca_ref, sa_ref, ci_ref, si_ref, lg_ref, lb_ref,
                 qo_ref, ko_ref, vo_ref, qio_ref, ki0_ref, ki1_ref, wo_ref):
    ca, sa, ci, si = ca_ref[...], sa_ref[...], ci_ref[...], si_ref[...]
    lane = lax.broadcasted_iota(I32, ca.shape, 1)
    first32 = (lane % IDX_DIM) < (IDX_DIM // 2)
    q_scale = ATTN_HEAD_DIM ** -0.5

    def rope_attn(v):
        return v * ca + pltpu.roll(v, ATTN_HEAD_DIM // 2, 1) * sa

    def rope_idx(v):
        rot = jnp.where(first32, pltpu.roll(v, LANES - IDX_DIM // 2, 1), pltpu.roll(v, IDX_DIM // 2, 1))
        return v * ci + rot * si

    for h in range(ATTN_HEADS):
        sl = slice(h * LANES, (h + 1) * LANES)
        qo_ref[:, sl] = (rope_attn(q_ref[:, sl]) * q_scale).astype(BF16)
        ko_ref[:, sl] = rope_attn(k_ref[:, sl]).astype(BF16)
    vo_ref[0] = v_ref[...].T.astype(BF16)
    for p in range(IDX_HEADS * IDX_DIM // LANES):
        sl = slice(p * LANES, (p + 1) * LANES)
        qio_ref[:, sl] = rope_idx(qi_ref[:, sl]).astype(BF16)

    kraw = ki_ref[...]
    valid = lane < IDX_DIM
    mu = jnp.sum(kraw, axis=-1, keepdims=True) * (1.0 / IDX_DIM)
    dk = jnp.where(valid, kraw - mu, 0.0)
    var = jnp.sum(dk * dk, axis=-1, keepdims=True) * (1.0 / IDX_DIM)
    kn = jnp.where(valid, dk * lax.rsqrt(var + LN_EPS) * lg_ref[...] + lb_ref[...], 0.0)
    kr = jnp.where(valid, rope_idx(kn), 0.0)
    ki0_ref[...] = kr.astype(BF16)
    ki1_ref[...] = pltpu.roll(kr, IDX_DIM, 1).astype(BF16)
    wo_ref[...] = wi_ref[...] * (IDX_HEADS ** -0.5 * IDX_DIM ** -0.5)


def _prep(proj, tail, cos_a, sin_a, cos_i, sin_i, ln_g, ln_b):
    t = proj.shape[0]
    tm = ATT_KC
    w = D_ATTN
    pad64 = lambda v: jnp.pad(v, (0, LANES - IDX_DIM))[None, :]
    col = lambda c: pl.BlockSpec((tm, w), lambda i: (i, c // w))
    lane_blk = lambda c: pl.BlockSpec((tm, LANES), lambda i: (i, c))
    full = lambda: pl.BlockSpec((1, LANES), lambda i: (0, 0))
    return pl.pallas_call(
        _prep_kernel,
        grid=(t // tm,),
        in_specs=[col(COL_Q), col(COL_K), col(COL_V), col(COL_QI), lane_blk(1), lane_blk(2),
                  lane_blk(0), lane_blk(0), lane_blk(0), lane_blk(0), full(), full()],
        out_specs=[pl.BlockSpec((tm, w), lambda i: (i, 0))] * 2 + [pl.BlockSpec((1, w, tm), lambda i: (i, 0, 0))]
        + [pl.BlockSpec((tm, w), lambda i: (i, 0))] + [pl.BlockSpec((tm, LANES), lambda i: (i, 0))] * 3,
        out_shape=[jax.ShapeDtypeStruct((t, w), BF16)] * 2 + [jax.ShapeDtypeStruct((t // tm, w, tm), BF16)]
        + [jax.ShapeDtypeStruct((t, w), BF16)]
        + [jax.ShapeDtypeStruct((t, LANES), BF16)] * 2 + [jax.ShapeDtypeStruct((t, LANES), F32)],
        compiler_params=_cparams(("parallel",), 48),
        name="prep",
    )(proj, proj, proj, proj, tail, tail, cos_a, sin_a, cos_i, sin_i, pad64(ln_g), pad64(ln_b))


ATT_QB = 128
ATT_KC = 512


def _attn_kernel(q_ref, k_ref, vt_ref, qi_ref, ki0_ref, ki1_ref, w_ref, o_ref, sc_ref, m_ref, l_ref, acc_ref,
                 *, topk):
    i = pl.program_id(1)
    n_ch = (i * ATT_QB + ATT_QB + ATT_KC - 1) // ATT_KC
    key_off = lax.broadcasted_iota(I32, (ATT_KC, ATT_QB), 0)
    q_pos = i * ATT_QB + lax.broadcasted_iota(I32, (ATT_KC, ATT_QB), 1)
    w_t = w_ref[...].T

    def score_chunk(c, carry):
        start = pl.multiple_of(c * ATT_KC, ATT_KC)
        ki0 = ki0_ref[pl.ds(start, ATT_KC), :]
        ki1 = ki1_ref[pl.ds(start, ATT_KC), :]
        acc = jnp.zeros((ATT_KC, ATT_QB), F32)
        for h in range(IDX_HEADS):
            pair = qi_ref[:, (h // 2) * LANES:(h // 2 + 1) * LANES]
            rel = _dot_nt(ki0 if h % 2 == 0 else ki1, pair)
            acc = acc + jnp.maximum(rel, 0.0) * w_t[h:h + 1, :]
        sc_ref[c] = jnp.where(start + key_off <= q_pos, acc, -jnp.inf)
        return carry

    lax.fori_loop(0, n_ch, score_chunk, 0)

    def key_to_float(key):
        return pltpu.bitcast(jnp.where(key < 0, key ^ jnp.int32(0x7FFFFFFF), key), F32)

    def search_bit(it, ans):
        cand = ans ^ lax.shift_left(jnp.int32(1), jnp.int32(31) - it)
        cand_f = key_to_float(cand)

        def count_chunk(c, cnt):
            part = jnp.where(sc_ref[c] >= cand_f, 1, 0).astype(I32)
            return cnt + jnp.sum(part.reshape(ATT_KC // SUBLANES, SUBLANES, ATT_QB), axis=0)

        cnt = lax.fori_loop(0, n_ch, count_chunk, jnp.zeros((SUBLANES, ATT_QB), I32))
        total = jnp.sum(cnt, axis=0, keepdims=True)
        return jnp.where(total >= topk, cand, ans)

    kth = lax.fori_loop(0, 32, search_bit, jnp.full((1, ATT_QB), INT_MIN, I32))
    lowest_finite = jnp.float32(np.finfo(np.float32).min)
    thr = jnp.where(kth < KEY_LOWEST_FINITE, lowest_finite, key_to_float(kth))

    m_ref[...] = jnp.full(m_ref.shape, -1e30, F32)
    l_ref[...] = jnp.zeros(l_ref.shape, F32)
    acc_ref[...] = jnp.zeros(acc_ref.shape, F32)

    def att_chunk(c, carry):
        start = pl.multiple_of(c * ATT_KC, ATT_KC)
        sel = sc_ref[c] >= thr
        for h in range(ATTN_HEADS):
            sl = slice(h * LANES, (h + 1) * LANES)
            s = _dot_nt(k_ref[pl.ds(start, ATT_KC), sl], q_ref[:, sl])
            s = jnp.where(sel, s, -jnp.inf)
            m_old = m_ref[h:h + 1, :]
            m_new = jnp.maximum(m_old, jnp.max(s, axis=0, keepdims=True))
            p = jnp.exp(s - m_new)
            alpha = jnp.exp(m_old - m_new)
            l_ref[h:h + 1, :] = alpha * l_ref[h:h + 1, :] + jnp.sum(p, axis=0, keepdims=True)
            m_ref[h:h + 1, :] = m_new
            acc_ref[h] = alpha * acc_ref[h] + _dot(vt_ref[c, sl, :], p.astype(BF16))
        return carry

    lax.fori_loop(0, n_ch, att_chunk, 0)
    for h in range(ATTN_HEADS):
        o_ref[:, h * LANES:(h + 1) * LANES] = (acc_ref[h] / l_ref[h:h + 1, :]).T.astype(o_ref.dtype)


def _attn(q_r, k_r, v_t, qi_r, ki0, ki1, w_s, bsz, seq):
    t = q_r.shape[0]
    nq = seq // ATT_QB
    n_kc = seq // ATT_KC
    topk = min(INDEX_TOPK, seq // 4)
    once = pl.Buffered(1)
    qblk = lambda width: pl.BlockSpec((ATT_QB, width), lambda b, i: (b * nq + i, 0))
    per_batch = lambda width: pl.BlockSpec((seq, width), lambda b, i: (b, 0), pipeline_mode=once)
    return pl.pallas_call(
        functools.partial(_attn_kernel, topk=topk),
        grid=(bsz, nq),
        in_specs=[qblk(D_ATTN), per_batch(D_ATTN),
                  pl.BlockSpec((n_kc, D_ATTN, ATT_KC), lambda b, i: (b, 0, 0), pipeline_mode=once),
                  qblk(IDX_HEADS * IDX_DIM), per_batch(LANES), per_batch(LANES), qblk(LANES)],
        out_specs=qblk(D_ATTN),
        out_shape=jax.ShapeDtypeStruct((t, D_ATTN), BF16),
        scratch_shapes=[pltpu.VMEM((n_kc, ATT_KC, ATT_QB), F32),
                        pltpu.VMEM((ATTN_HEADS, ATT_QB), F32),
                        pltpu.VMEM((ATTN_HEADS, ATT_QB), F32),
                        pltpu.VMEM((ATTN_HEADS, ATTN_HEAD_DIM, ATT_QB), F32)],
        compiler_params=_cparams(("parallel", "arbitrary"), 48),
        name="attn",
    )(q_r, k_r, v_t, qi_r, ki0, ki1, w_s)


def _first_index_of_max(v, iota, n):
    m = jnp.max(v, axis=0, keepdims=True)
    idx = jnp.min(jnp.where(v == m, iota, n), axis=0, keepdims=True)
    return m, idx


def _route(scores, biased):
    tm = scores.shape[1]
    neg_inf = jnp.float32(-jnp.inf)
    iota8 = lax.broadcasted_iota(I32, (EXPERTS_PER_GROUP, tm), 0)
    group_scores = []
    for g in range(N_EXPERT_GROUPS):
        v = biased[g * EXPERTS_PER_GROUP:(g + 1) * EXPERTS_PER_GROUP, :]
        m1, i1 = _first_index_of_max(v, iota8, EXPERTS_PER_GROUP)
        m2 = jnp.max(jnp.where(iota8 == i1, neg_inf, v), axis=0, keepdims=True)
        group_scores.append(m1 + m2)
    gs = jnp.concatenate(group_scores, axis=0)
    keep = jnp.zeros(gs.shape, jnp.bool_)
    for _ in range(TOPK_GROUPS):
        _, gi = _first_index_of_max(gs, iota8, N_EXPERT_GROUPS)
        hit = iota8 == gi
        keep = jnp.logical_or(keep, hit)
        gs = jnp.where(hit, neg_inf, gs)
    keep_f = jnp.where(keep, 1.0, 0.0)
    masked = jnp.concatenate(
        [jnp.where(keep_f[g:g + 1, :] > 0.0, biased[g * EXPERTS_PER_GROUP:(g + 1) * EXPERTS_PER_GROUP, :], neg_inf)
         for g in range(N_EXPERT_GROUPS)], axis=0)
    iota64 = lax.broadcasted_iota(I32, (N_EXPERTS, tm), 0)
    sel_w = jnp.zeros((N_EXPERTS, tm), F32)
    for _ in range(TOP_K):
        _, ei = _first_index_of_max(masked, iota64, N_EXPERTS)
        hit = iota64 == ei
        sel_w = jnp.where(hit, scores, sel_w)
        masked = jnp.where(hit, neg_inf, masked)
    denom = jnp.sum(sel_w, axis=0, keepdims=True)
    return sel_w / denom * ROUTED_SCALE


def _outproj_kernel(ys_ref, ya_ref, wa_ref, wb_ref, x_ref, g1_ref, sc2_ref, sh2_ref, lg_ref, lb_ref,
                    wrh_ref, wrl_ref, rb_ref, x1_ref, u2_ref, gt_ref):
    mix = _dot(ys_ref[...], wa_ref[...]) + _dot(ya_ref[...], wb_ref[...])
    h = ALPHA * x_ref[...] + g1_ref[0] * mix
    x1 = _layer_norm_rows(h, lg_ref[...], lb_ref[...])
    x1_ref[...] = x1
    u2 = x1 * (1.0 + sc2_ref[0]) + sh2_ref[0]
    uh = u2.astype(BF16)
    ul = (u2 - uh.astype(F32)).astype(BF16)
    u2_ref[...] = uh
    wrh = wrh_ref[...]
    logits = _dot_nt(wrh, uh) + _dot_nt(wrh, ul) + _dot_nt(wrl_ref[...], uh)
    scores = 1.0 / (1.0 + jnp.exp(-logits))
    gt_ref[...] = _route(scores, scores + rb_ref[:, 0:1])


def _outproj(y_ssm, y_att, wo_a, wo_b, x2, gate1, scale2, shift2, ln_g, ln_b, wr_hi, wr_lo, rbias, seq):
    t, d = x2.shape
    tm = MOE_TILE
    tpb = seq // tm
    half = y_ssm.shape[1]
    rows = lambda width: pl.BlockSpec((tm, width), lambda i: (i, 0))
    full = lambda shape: pl.BlockSpec(shape, lambda i: (0,) * len(shape))
    mod = lambda: pl.BlockSpec((1, 1, d), lambda i: (i // tpb, 0, 0))
    return pl.pallas_call(
        _outproj_kernel,
        grid=(t // tm,),
        in_specs=[rows(half), rows(half), full((half, d)), full((half, d)), rows(d), mod(), mod(), mod(),
                  full((1, d)), full((1, d)), full((N_EXPERTS, d)), full((N_EXPERTS, d)), full((N_EXPERTS, LANES))],
        out_specs=[rows(d), rows(d), pl.BlockSpec((N_EXPERTS, tm), lambda i: (0, i))],
        out_shape=[jax.ShapeDtypeStruct((t, d), F32), jax.ShapeDtypeStruct((t, d), BF16),
                   jax.ShapeDtypeStruct((N_EXPERTS, t), F32)],
        compiler_params=_cparams(("parallel",), 48),
        name="outproj",
    )(y_ssm, y_att, wo_a, wo_b, x2, gate1, scale2, shift2, ln_g[None, :], ln_b[None, :], wr_hi, wr_lo, rbias)


MOE_TILE = 512
MOE_UNIT = 16
MOE_BM = 256
MOE_RCAP = TOP_K * MOE_TILE + N_EXPERTS * MOE_UNIT
MOE_RCH = 512
MOE_GW = 3 * LANES
MOE_UNSET = 1e9


def _moe_plan(gates_t, n_tiles):
    upb = MOE_BM // MOE_UNIT
    cnt = jnp.sum((gates_t > 0.0).reshape(N_EXPERTS, n_tiles, MOE_TILE), axis=2, dtype=I32).T
    nun = (cnt + MOE_UNIT - 1) // MOE_UNIT
    loff = jnp.cumsum(nun, axis=1) - nun
    tot = jnp.sum(nun, axis=0)
    totpad = (tot + upb - 1) // upb * upb
    base = jnp.cumsum(totpad) - totpad
    goff = base[None, :] + jnp.cumsum(nun, axis=0) - nun
    cum_nb = jnp.cumsum(totpad // upb)
    n_blocks = (TOP_K * n_tiles * MOE_TILE // MOE_UNIT + n_tiles * N_EXPERTS + N_EXPERTS * (upb - 1) + upb - 1) // upb
    blk_e = jnp.minimum(jnp.searchsorted(cum_nb, jnp.arange(n_blocks, dtype=I32), side="right"),
                        N_EXPERTS - 1).astype(I32)
    loff_v = jnp.broadcast_to((loff * MOE_UNIT).astype(F32)[:, :, None], (n_tiles, N_EXPERTS, LANES))
    flat = lambda v: v.reshape(-1).astype(I32)
    return dict(nun=flat(nun), loff=flat(loff), goff=flat(goff), padstart=flat(base + tot), padn=flat(totpad - tot),
                blk_e=blk_e, nb_used=cum_nb[-1:].astype(I32), loff_v=loff_v, n_blocks=n_blocks)


def _unit_rows(unit):
    return pl.ds(pl.multiple_of(unit * MOE_UNIT, MOE_UNIT), MOE_UNIT)


def _one_hot_rows(row_id, targets, axis):
    p = jnp.zeros(row_id.shape, F32)
    for k in range(TOP_K):
        tgt = targets[k:k + 1, :] if axis == 0 else targets[:, k:k + 1]
        p = jnp.where(row_id == tgt, 1.0, p)
    return p.astype(BF16)


def _dispatch_kernel(nun_ref, loff_ref, goff_ref, pst_ref, pnn_ref, u_ref, g_ref, lv_ref, tri_ref,
                     xs_ref, lkt_ref, sorted_ref, xaug_ref, zero_ref, sem):
    j = pl.program_id(0)
    g = g_ref[...]
    sel = g > 0.0
    rank = _dot(jnp.where(sel, 1.0, 0.0).astype(BF16), tri_ref[...])
    loff = jnp.concatenate([lv_ref[0]] * (MOE_TILE // LANES), axis=1)
    rem = jnp.where(sel, loff + rank, MOE_UNSET)
    rows = []
    for _ in range(TOP_K):
        cur = jnp.min(rem, axis=0, keepdims=True)
        rows.append(cur)
        rem = jnp.where(rem == cur, MOE_UNSET, rem)
    lk = jnp.concatenate(rows, axis=0)
    lkt_ref[0] = jnp.concatenate([lk, jnp.full((LANES - TOP_K, MOE_TILE), MOE_UNSET, F32)], axis=0).T

    gt = jnp.concatenate([g, jnp.zeros_like(g)], axis=0).T
    hi = gt.astype(BF16)
    r1 = gt - hi.astype(F32)
    mid = r1.astype(BF16)
    xaug_ref[:, :D_MODEL] = u_ref[...]
    xaug_ref[:, D_MODEL:D_MODEL + LANES] = hi
    xaug_ref[:, D_MODEL + LANES:D_MODEL + 2 * LANES] = mid
    xaug_ref[:, D_MODEL + 2 * LANES:] = (r1 - mid.astype(F32)).astype(BF16)

    def sort_chunk(rc, carry):
        r0 = pl.multiple_of(rc * MOE_RCH, MOE_RCH)
        row_id = (lax.broadcasted_iota(I32, (MOE_RCH, MOE_TILE), 0) + r0).astype(F32)
        sorted_ref[pl.ds(r0, MOE_RCH), :] = _dot(_one_hot_rows(row_id, lk, 0), xaug_ref[...]).astype(BF16)
        return carry

    lax.fori_loop(0, MOE_RCAP // MOE_RCH, sort_chunk, 0)

    def unit_copy(src_unit, dst_unit):
        return pltpu.make_async_copy(sorted_ref.at[_unit_rows(src_unit)], xs_ref.at[_unit_rows(dst_unit)], sem)

    def send_segment(e, carry):
        seg = j * N_EXPERTS + e
        src0, dst0 = loff_ref[seg], goff_ref[seg]

        def send_unit(uu, c):
            unit_copy(src0 + uu, dst0 + uu).start()
            return c

        lax.fori_loop(0, nun_ref[seg], send_unit, 0)
        return carry

    lax.fori_loop(0, N_EXPERTS, send_segment, 0)
    last = j * N_EXPERTS + N_EXPERTS - 1

    def drain_unit(uu, c):
        unit_copy(0, 0).wait()
        return c

    lax.fori_loop(0, loff_ref[last] + nun_ref[last], drain_unit, 0)

    @pl.when(j == pl.num_programs(0) - 1)
    def _():
        zero_ref[...] = jnp.zeros(zero_ref.shape, BF16)

        def pad_expert(e, carry):
            def pad_copy(uu):
                return pltpu.make_async_copy(zero_ref, xs_ref.at[_unit_rows(pst_ref[e] + uu)], sem)

            def start(uu, c):
                pad_copy(uu).start()
                return c

            def wait(uu, c):
                pad_copy(uu).wait()
                return c

            lax.fori_loop(0, pnn_ref[e], start, 0)
            lax.fori_loop(0, pnn_ref[e], wait, 0)
            return carry

        lax.fori_loop(0, N_EXPERTS, pad_expert, 0)


def _dispatch(u2, gates_t, plan):
    t, d = u2.shape
    n_tiles = t // MOE_TILE
    xw = d + MOE_GW
    tri = jnp.asarray(np.triu(np.ones((MOE_TILE, MOE_TILE), np.float32), k=1), BF16)
    grid_spec = pltpu.PrefetchScalarGridSpec(
        num_scalar_prefetch=5,
        grid=(n_tiles,),
        in_specs=[pl.BlockSpec((MOE_TILE, d), lambda j, *_: (j, 0)),
                  pl.BlockSpec((N_EXPERTS, MOE_TILE), lambda j, *_: (0, j)),
                  pl.BlockSpec((1, N_EXPERTS, LANES), lambda j, *_: (j, 0, 0)),
                  pl.BlockSpec((MOE_TILE, MOE_TILE), lambda j, *_: (0, 0))],
        out_specs=[pl.BlockSpec(memory_space=pl.ANY),
                   pl.BlockSpec((1, MOE_TILE, LANES), lambda j, *_: (j, 0, 0))],
        scratch_shapes=[pltpu.VMEM((MOE_RCAP, xw), BF16), pltpu.VMEM((MOE_TILE, xw), BF16),
                        pltpu.VMEM((MOE_UNIT, xw), BF16), pltpu.SemaphoreType.DMA(())])
    return pl.pallas_call(
        _dispatch_kernel,
        grid_spec=grid_spec,
        out_shape=[jax.ShapeDtypeStruct((plan["n_blocks"] * MOE_BM, xw), BF16),
                   jax.ShapeDtypeStruct((n_tiles, MOE_TILE, LANES), F32)],
        compiler_params=_cparams(("arbitrary",), 56),
        name="dispatch",
    )(plan["nun"], plan["loff"], plan["goff"], plan["padstart"], plan["padn"], u2, gates_t, plan["loff_v"], tri)


def _experts_kernel(be_ref, nbu_ref, x_ref, wg_ref, wu_ref, wd_ref, y_ref, wgb_ref, wub_ref, wdb_ref):
    b = pl.program_id(0)

    @pl.when(b < nbu_ref[0])
    def _():
        e = be_ref[b]

        @pl.when(jnp.logical_or(b == 0, e != be_ref[jnp.maximum(b - 1, 0)]))
        def _():
            wgb_ref[...] = wg_ref[0].astype(BF16)
            wub_ref[...] = wu_ref[0].astype(BF16)
            wdb_ref[...] = wd_ref[0].astype(BF16)

        x = x_ref[:, :D_MODEL]
        g3 = ((x_ref[:, D_MODEL:D_MODEL + LANES].astype(F32) + x_ref[:, D_MODEL + LANES:D_MODEL + 2 * LANES].astype(F32))
              + x_ref[:, D_MODEL + 2 * LANES:].astype(F32))
        lane = lax.broadcasted_iota(I32, g3.shape, 1)
        gate = jnp.sum(jnp.where(lane == e, g3, 0.0), axis=1, keepdims=True)
        hid = (_silu(_dot(x, wgb_ref[...])) * _dot(x, wub_ref[...]) * gate).astype(BF16)
        y_ref[...] = _dot(hid, wdb_ref[...]).astype(y_ref.dtype)


def _experts(xs, plan, wg, wu, wd):
    ns, xw = xs.shape
    _, d, f = wg.shape
    blk = lambda b, be, nbu: jnp.minimum(b, nbu[0] - 1)
    grid_spec = pltpu.PrefetchScalarGridSpec(
        num_scalar_prefetch=2,
        grid=(ns // MOE_BM,),
        in_specs=[pl.BlockSpec((MOE_BM, xw), lambda b, be, nbu: (blk(b, be, nbu), 0)),
                  pl.BlockSpec((1, d, f), lambda b, be, nbu: (be[blk(b, be, nbu)], 0, 0)),
                  pl.BlockSpec((1, d, f), lambda b, be, nbu: (be[blk(b, be, nbu)], 0, 0)),
                  pl.BlockSpec((1, f, d), lambda b, be, nbu: (be[blk(b, be, nbu)], 0, 0))],
        out_specs=pl.BlockSpec((MOE_BM, d), lambda b, be, nbu: (blk(b, be, nbu), 0)),
        scratch_shapes=[pltpu.VMEM((d, f), BF16), pltpu.VMEM((d, f), BF16), pltpu.VMEM((f, d), BF16)])
    return pl.pallas_call(
        _experts_kernel,
        grid_spec=grid_spec,
        out_shape=jax.ShapeDtypeStruct((ns, d), BF16),
        compiler_params=_cparams(("arbitrary",), 48),
        name="experts",
    )(plan["blk_e"], plan["nb_used"], xs, wg, wu, wd)


def _final_kernel(nun_ref, loff_ref, goff_ref, lkt_ref, u_ref, wg_ref, wu_ref, wd_ref, x1_ref, g2_ref, lg_ref, lb_ref,
                  ys_ref, o_ref, ybuf_ref, sem):
    j = pl.program_id(0)

    @pl.when(j == 0)
    def _():
        ybuf_ref[...] = jnp.zeros(ybuf_ref.shape, ybuf_ref.dtype)

    def unit_copy(src_unit, dst_unit):
        return pltpu.make_async_copy(ys_ref.at[_unit_rows(src_unit)], ybuf_ref.at[_unit_rows(dst_unit)], sem)

    def fetch_segment(e, carry):
        seg = j * N_EXPERTS + e
        src0, dst0 = goff_ref[seg], loff_ref[seg]

        def fetch_unit(uu, c):
            unit_copy(src0 + uu, dst0 + uu).start()
            return c

        lax.fori_loop(0, nun_ref[seg], fetch_unit, 0)
        return carry

    lax.fori_loop(0, N_EXPERTS, fetch_segment, 0)

    u = u_ref[...]
    hid = (_silu(_dot(u, wg_ref[...])) * _dot(u, wu_ref[...])).astype(BF16)
    o_ref[...] = _dot(hid, wd_ref[...])

    last = j * N_EXPERTS + N_EXPERTS - 1

    def drain_unit(uu, c):
        unit_copy(0, 0).wait()
        return c

    lax.fori_loop(0, loff_ref[last] + nun_ref[last], drain_unit, 0)

    lkt = lkt_ref[0]

    def combine_chunk(rc, carry):
        r0 = pl.multiple_of(rc * MOE_RCH, MOE_RCH)
        row_id = (lax.broadcasted_iota(I32, (MOE_TILE, MOE_RCH), 1) + r0).astype(F32)
        o_ref[...] += _dot(_one_hot_rows(row_id, lkt, 1), ybuf_ref[pl.ds(r0, MOE_RCH), :])
        return carry

    lax.fori_loop(0, MOE_RCAP // MOE_RCH, combine_chunk, 0)
    h = ALPHA * x1_ref[...] + g2_ref[0] * o_ref[...]
    o_ref[...] = _layer_norm_rows(h, lg_ref[...], lb_ref[...])


def _final(u2, wg_s, wu_s, wd_s, ys, lkt, plan, x1, gate2, ln_g, ln_b, seq):
    t, d = x1.shape
    tm = MOE_TILE
    tpb = seq // tm
    f = wg_s.shape[1]
    once = pl.Buffered(1)
    rows = lambda: pl.BlockSpec((tm, d), lambda j, *_: (j, 0))
    full = lambda shape: pl.BlockSpec(shape, lambda j, *_: (0,) * len(shape), pipeline_mode=once)
    grid_spec = pltpu.PrefetchScalarGridSpec(
        num_scalar_prefetch=3,
        grid=(t // tm,),
        in_specs=[pl.BlockSpec((1, tm, LANES), lambda j, *_: (j, 0, 0)), rows(),
                  full((d, f)), full((d, f)), full((f, d)), rows(),
                  pl.BlockSpec((1, 1, d), lambda j, *_: (j // tpb, 0, 0)), full((1, d)), full((1, d)),
                  pl.BlockSpec(memory_space=pl.ANY)],
        out_specs=rows(),
        scratch_shapes=[pltpu.VMEM((MOE_RCAP, d), BF16), pltpu.SemaphoreType.DMA(())])
    return pl.pallas_call(
        _final_kernel,
        grid_spec=grid_spec,
        out_shape=jax.ShapeDtypeStruct((t, d), F32),
        compiler_params=_cparams(("arbitrary",), 58),
        name="final",
    )(plan["nun"], plan["loff"], plan["goff"], lkt, u2, wg_s, wu_s, wd_s, x1, gate2, ln_g[None, :], ln_b[None, :], ys)


def _split_hi_lo(w):
    hi = w.astype(BF16)
    lo = (w - hi.astype(F32)).astype(BF16)
    return hi, lo


def _rope_tables(positions, dim, reps):
    inv_freq = 1.0 / (ROPE_THETA ** (jnp.arange(0, dim, 2, dtype=F32) / dim))
    ang = positions.astype(F32).reshape(-1)[:, None] * inv_freq
    cos, sin = jnp.cos(ang), jnp.sin(ang)
    return (jnp.tile(jnp.concatenate([cos, cos], axis=1), (1, reps)),
            jnp.tile(jnp.concatenate([-sin, sin], axis=1), (1, reps)))


def _layer(x2, c, positions, bsz, seq, w_ada, b_ada, w_in, conv_w, conv_b, dt_bias, a_log, d_skip, ssd_norm_w,
           idx_k_ln_g, idx_k_ln_b, w_out, ln1_g, ln1_b, w_router, router_bias, w_gate_e, w_up_e, w_down_e,
           w_gate_s, w_up_s, w_down_s, ln2_g, ln2_b, tables):
    d = D_MODEL
    mod = _ada(c, w_ada, b_ada).reshape(bsz, 6, 1, d)
    shift1, scale1, gate1, shift2, scale2, gate2 = [mod[:, k] for k in range(6)]

    o_z, o_xbc, o_dt, o_q = 0, D_SSM, 2 * D_SSM + 2 * SSM_GROUPS * SSM_STATE, 2 * D_SSM + 2 * SSM_GROUPS * SSM_STATE + SSM_HEADS
    o_k, o_v, o_qi = o_q + D_ATTN, o_q + 2 * D_ATTN, o_q + 3 * D_ATTN
    o_ki = o_qi + IDX_HEADS * IDX_DIM
    o_wi = o_ki + IDX_DIM
    w_main = jnp.concatenate([w_in[:, o_z:o_z + D_SSM], w_in[:, o_xbc:o_xbc + D_SSM], w_in[:, o_q:o_ki],
                              w_in[:, o_xbc + D_SSM:o_dt]], axis=1).astype(BF16)
    padc = lambda w: jnp.pad(w, ((0, 0), (0, LANES - w.shape[1])))
    w_tail = jnp.concatenate([padc(w_in[:, o_dt:o_q]), padc(w_in[:, o_ki:o_wi]), padc(w_in[:, o_wi:])], axis=1)
    wt_hi, wt_lo = _split_hi_lo(w_tail)

    proj, tail = _inproj(x2, scale1, shift1, w_main, wt_hi, wt_lo, seq)
    y_ssm = _ssm(proj, tail, conv_w, conv_b, dt_bias, a_log, d_skip, ssd_norm_w, bsz, seq)
    q_r, k_r, v_t, qi_r, ki0, ki1, w_s = _prep(proj, tail, *tables, idx_k_ln_g, idx_k_ln_b)
    y_att = _attn(q_r, k_r, v_t, qi_r, ki0, ki1, w_s, bsz, seq)

    wo = w_out.astype(BF16)
    wr_hi, wr_lo = _split_hi_lo(w_router.T)
    rbias = jnp.broadcast_to(router_bias[:, None], (N_EXPERTS, LANES))
    x1, u2, gates_t = _outproj(y_ssm, y_att, wo[:D_SSM], wo[D_SSM:], x2, gate1, scale2, shift2, ln1_g, ln1_b,
                               wr_hi, wr_lo, rbias, seq)
    plan = _moe_plan(gates_t, x2.shape[0] // MOE_TILE)
    xs, lkt = _dispatch(u2, gates_t, plan)
    ys = _experts(xs, plan, w_gate_e, w_up_e, w_down_e)
    return _final(u2, w_gate_s.astype(BF16), w_up_s.astype(BF16), w_down_s.astype(BF16), ys, lkt, plan, x1, gate2,
                  ln2_g, ln2_b, seq)


def kernel(x, c, positions, w_ada, b_ada, w_in, conv_w, conv_b, dt_bias, a_log, d_skip, ssd_norm_w, idx_k_ln_g, idx_k_ln_b, w_out, ln1_g, ln1_b, w_router, router_bias, w_gate_e, w_up_e, w_down_e, w_gate_s, w_up_s, w_down_s, ln2_g, ln2_b):
    bsz, seq, d = x.shape
    tables = _rope_tables(positions, ATTN_HEAD_DIM, 1) + _rope_tables(positions, IDX_DIM, 2)
    x2 = x.reshape(bsz * seq, d)
    for l in range(w_ada.shape[0]):
        x2 = _layer(x2, c, positions, bsz, seq, w_ada[l], b_ada[l], w_in[l], conv_w[l], conv_b[l], dt_bias[l],
                    a_log[l], d_skip[l], ssd_norm_w[l], idx_k_ln_g[l], idx_k_ln_b[l], w_out[l], ln1_g[l], ln1_b[l],
                    w_router[l], router_bias[l], w_gate_e[l], w_up_e[l], w_down_e[l], w_gate_s[l], w_up_s[l],
                    w_down_s[l], ln2_g[l], ln2_b[l], tables)
    return x2.reshape(bsz, seq, d)
```

```python
import functools
import math

import jax
import jax.numpy as jnp
import numpy as np
from jax import lax
from jax.experimental import pallas as pl
from jax.experimental.pallas import tpu as pltpu

F32 = jnp.float32
BF16 = jnp.bfloat16
I32 = jnp.int32
HIGHEST = lax.Precision.HIGHEST

D_MODEL = 2048
D_SSM = 1024
D_ATTN = 1024
SSM_HEAD_DIM = 64
SSM_HEADS = 16
SSM_GROUPS = 2
SSM_STATE = 128
CONV_WIDTH = 4
SSD_CHUNK = 128
ATTN_HEAD_DIM = 128
ATTN_HEADS = 8
IDX_HEADS = 16
IDX_DIM = 64
INDEX_TOPK = 256
ROPE_THETA = 10000.0
N_EXPERTS = 64
N_EXPERT_GROUPS = 8
EXPERTS_PER_GROUP = 8
TOPK_GROUPS = 4
TOP_K = 8
D_EXPERT = 512
D_SHARED = 512
ROUTED_SCALE = 2.5
DEPTH = 1
ALPHA = (2.0 * DEPTH) ** 0.25
LN_EPS = 1e-5

LANES = 128
SUBLANES = 8
VMEM_BYTES_V7X = 64 * 1024 * 1024
INT_MIN = -(2 ** 31)
KEY_LOWEST_FINITE = INT_MIN + 0x00800000

COL_Z, COL_XS, COL_Q, COL_K, COL_V, COL_QI, COL_BC = 0, 1024, 2048, 3072, 4096, 5120, 6144
N_MAIN = 6656
N_TAIL = 3 * LANES


def _cparams(sem, vmem_mb):
    return pltpu.CompilerParams(dimension_semantics=sem, vmem_limit_bytes=vmem_mb * 1024 * 1024)


def _silu(v):
    return v * (1.0 / (1.0 + jnp.exp(-v)))


def _dot(a, b, precision=None):
    return jnp.dot(a, b, preferred_element_type=F32, precision=precision)


def _dot_nt(a, b, precision=None):
    return lax.dot_general(a, b, (((1,), (1,)), ((), ())), preferred_element_type=F32, precision=precision)


def _layer_norm_rows(h, g, b):
    mu = jnp.mean(h, axis=-1, keepdims=True)
    d = h - mu
    var = jnp.mean(d * d, axis=-1, keepdims=True)
    return d * lax.rsqrt(var + LN_EPS) * g + b


def _ada_kernel(cb_ref, w_ref, b_ref, o_ref):
    tn = w_ref.shape[1]
    for bi in range(cb_ref.shape[0]):
        cb = _silu(cb_ref[bi])
        cols = [jnp.sum(w_ref[:, j * LANES:(j + 1) * LANES] * cb, axis=0, keepdims=True)
                for j in range(tn // LANES)]
        o_ref[bi:bi + 1, :] = jnp.concatenate(cols, axis=1) + b_ref[...]


def _ada(c, w_ada, b_ada):
    bsz, d = c.shape
    n = w_ada.shape[1]
    tn = 1024
    cb = jnp.broadcast_to(c[:, :, None], (bsz, d, LANES))
    return pl.pallas_call(
        _ada_kernel,
        grid=(n // tn,),
        in_specs=[pl.BlockSpec((bsz, d, LANES), lambda j: (0, 0, 0)),
                  pl.BlockSpec((d, tn), lambda j: (0, j)),
                  pl.BlockSpec((1, tn), lambda j: (0, j))],
        out_specs=pl.BlockSpec((bsz, tn), lambda j: (0, j)),
        out_shape=jax.ShapeDtypeStruct((bsz, n), F32),
        compiler_params=_cparams(("parallel",), 40),
        name="ada",
    )(cb, w_ada, b_ada.reshape(1, n))


def _inproj_kernel(x_ref, sc_ref, sh_ref, w_ref, wth_ref, wtl_ref, o_ref, t_ref, u_ref):
    @pl.when(pl.program_id(1) == 0)
    def _():
        u = x_ref[...] * (1.0 + sc_ref[0]) + sh_ref[0]
        uh = u.astype(BF16)
        ul = (u - uh.astype(F32)).astype(BF16)
        u_ref[...] = uh
        t_ref[...] = (_dot(uh, wth_ref[...]) + _dot(uh, wtl_ref[...]) + _dot(ul, wth_ref[...]))

    o_ref[...] = _dot(u_ref[...], w_ref[...])


def _inproj(x2, scale1, shift1, w_main, wt_hi, wt_lo, seq):
    t, d = x2.shape
    tm, tn = 1024, 512
    tpb = seq // tm
    return pl.pallas_call(
        _inproj_kernel,
        grid=(t // tm, N_MAIN // tn),
        in_specs=[pl.BlockSpec((tm, d), lambda i, j: (i, 0)),
                  pl.BlockSpec((1, 1, d), lambda i, j: (i // tpb, 0, 0)),
                  pl.BlockSpec((1, 1, d), lambda i, j: (i // tpb, 0, 0)),
                  pl.BlockSpec((d, tn), lambda i, j: (0, j)),
                  pl.BlockSpec((d, N_TAIL), lambda i, j: (0, 0)),
                  pl.BlockSpec((d, N_TAIL), lambda i, j: (0, 0))],
        out_specs=[pl.BlockSpec((tm, tn), lambda i, j: (i, j)),
                   pl.BlockSpec((tm, N_TAIL), lambda i, j: (i, 0))],
        out_shape=[jax.ShapeDtypeStruct((t, N_MAIN), F32),
                   jax.ShapeDtypeStruct((t, N_TAIL), F32)],
        scratch_shapes=[pltpu.VMEM((tm, d), BF16)],
        compiler_params=_cparams(("parallel", "arbitrary"), 48),
        name="inproj",
    )(x2, scale1, shift1, w_main, wt_hi, wt_lo)


def _ssm_kernel(z_ref, xs_ref, bc_ref, dt_ref, cwx_ref, cbx_ref, cwb_ref, cbb_ref, dtb_ref, alog_ref,
                dsk_ref, nw_ref, e_ref, e2_ref, tril_ref, o_ref, px_ref, pb_ref, st_ref):
    q = SSD_CHUNK
    hpg = SSM_HEADS // SSM_GROUPS
    gw = hpg * SSM_HEAD_DIM

    @pl.when(pl.program_id(1) == 0)
    def _():
        px_ref[0:SUBLANES, :] = jnp.zeros((SUBLANES, px_ref.shape[1]), F32)
        pb_ref[0:SUBLANES, :] = jnp.zeros((SUBLANES, pb_ref.shape[1]), F32)
        st_ref[...] = jnp.zeros(st_ref.shape, F32)

    def conv_silu(raw_ref, pad_ref, w_ref, b_ref):
        pad_ref[SUBLANES:SUBLANES + q, :] = raw_ref[...]
        acc = b_ref[...] + w_ref[0:1, :] * pad_ref[SUBLANES - 3:SUBLANES - 3 + q, :]
        for k in range(1, CONV_WIDTH):
            acc = acc + w_ref[k:k + 1, :] * pad_ref[SUBLANES - 3 + k:SUBLANES - 3 + k + q, :]
        pad_ref[0:SUBLANES, :] = raw_ref[q - SUBLANES:q, :]
        return _silu(acc)

    xs = conv_silu(xs_ref, px_ref, cwx_ref, cbx_ref)
    bc = conv_silu(bc_ref, pb_ref, cwb_ref, cbb_ref)

    dtr = dt_ref[...] + dtb_ref[...]
    dt = jnp.maximum(dtr, 0.0) + jnp.log(1.0 + jnp.exp(-jnp.abs(dtr)))
    log_a = dt * (-jnp.exp(alog_ref[...]))
    cs = _dot(tril_ref[...], log_a, HIGHEST)
    cs_e = _dot(cs, e_ref[...], HIGHEST)
    dt_e = _dot(dt, e_ref[...], HIGHEST)
    cs_col = _dot(cs, e2_ref[...], HIGHEST)
    cs_t = cs.T
    cs_last = cs_e[q - 1:q, :]

    xdt = xs * dt_e
    rows = lax.broadcasted_iota(I32, (q, q), 0)
    cols = lax.broadcasted_iota(I32, (q, q), 1)
    causal = rows >= cols
    first_half = lax.broadcasted_iota(I32, (q, LANES), 1) < SSM_HEAD_DIM

    y_pairs = []
    for g in range(SSM_GROUPS):
        b_g = bc[:, g * SSM_STATE:(g + 1) * SSM_STATE]
        c_g = bc[:, (SSM_GROUPS + g) * SSM_STATE:(SSM_GROUPS + g + 1) * SSM_STATE]
        cb = _dot_nt(c_g, b_g, HIGHEST)
        for hp in range(hpg // 2):
            pair = g * (hpg // 2) + hp
            x_pair = xdt[:, pair * LANES:(pair + 1) * LANES]
            ys = []
            for sub in range(2):
                h = 2 * pair + sub
                seg = cs_col[:, h * LANES:(h + 1) * LANES] - cs_t[h:h + 1, :]
                dec = jnp.exp(jnp.where(causal, seg, -jnp.inf))
                ys.append(_dot(cb * dec, x_pair, HIGHEST))
            y_pairs.append(jnp.where(first_half, ys[0], ys[1]))
    y_diag = jnp.concatenate(y_pairs, axis=1)

    xw = xdt * jnp.exp(cs_last - cs_e)
    y_off, new_states = [], []
    for g in range(SSM_GROUPS):
        b_g = bc[:, g * SSM_STATE:(g + 1) * SSM_STATE]
        c_g = bc[:, (SSM_GROUPS + g) * SSM_STATE:(SSM_GROUPS + g + 1) * SSM_STATE]
        h_in = st_ref[:, g * gw:(g + 1) * gw]
        y_off.append(_dot(c_g, h_in, HIGHEST))
        new_states.append(_dot(b_g.T, xw[:, g * gw:(g + 1) * gw], HIGHEST))
    y_off = jnp.concatenate(y_off, axis=1) * jnp.exp(cs_e)
    st_ref[...] = jnp.exp(cs_last) * st_ref[...] + jnp.concatenate(new_states, axis=1)

    y = y_diag + y_off + dsk_ref[...] * xs
    yf = y * _silu(z_ref[...])
    ms = jnp.mean(yf * yf, axis=-1, keepdims=True)
    o_ref[...] = (yf * lax.rsqrt(ms + LN_EPS) * nw_ref[...]).astype(o_ref.dtype)


def _ssm(proj, tail, conv_w, conv_b, dt_bias, a_log, d_skip, ssd_norm_w, bsz, seq):
    t = proj.shape[0]
    q = SSD_CHUNK
    n_c = seq // q
    nbc = 2 * SSM_GROUPS * SSM_STATE
    cw_x, cw_b = conv_w[:, :D_SSM], conv_w[:, D_SSM:]
    cb_x, cb_b = conv_b[None, :D_SSM], conv_b[None, D_SSM:]
    pad16 = lambda v: jnp.pad(v, (0, LANES - SSM_HEADS))[None, :]
    head_of_lane = np.arange(D_SSM) // SSM_HEAD_DIM
    e_mat = jnp.asarray((np.arange(LANES)[:, None] == head_of_lane[None, :]).astype(np.float32))
    e2_mat = jnp.asarray((np.arange(LANES)[:, None] == (np.arange(SSM_HEADS * LANES) // LANES)[None, :])
                         .astype(np.float32))
    tril = jnp.asarray(np.tril(np.ones((q, q), np.float32)))
    row = lambda b, c: b * n_c + c
    full = lambda shape: pl.BlockSpec(shape, lambda b, c: (0,) * len(shape))
    return pl.pallas_call(
        _ssm_kernel,
        grid=(bsz, n_c),
        in_specs=[pl.BlockSpec((q, D_SSM), lambda b, c: (row(b, c), COL_Z // D_SSM)),
                  pl.BlockSpec((q, D_SSM), lambda b, c: (row(b, c), COL_XS // D_SSM)),
                  pl.BlockSpec((q, nbc), lambda b, c: (row(b, c), COL_BC // nbc)),
                  pl.BlockSpec((q, LANES), lambda b, c: (row(b, c), 0)),
                  full((CONV_WIDTH, D_SSM)), full((1, D_SSM)), full((CONV_WIDTH, nbc)), full((1, nbc)),
                  full((1, LANES)), full((1, LANES)), full((1, D_SSM)), full((1, D_SSM)),
                  full((LANES, D_SSM)), full((LANES, SSM_HEADS * LANES)), full((q, q))],
        out_specs=pl.BlockSpec((q, D_SSM), lambda b, c: (row(b, c), 0)),
        out_shape=jax.ShapeDtypeStruct((t, D_SSM), BF16),
        scratch_shapes=[pltpu.VMEM((SUBLANES + q, D_SSM), F32),
                        pltpu.VMEM((SUBLANES + q, nbc), F32),
                        pltpu.VMEM((SSM_STATE, D_SSM), F32)],
        compiler_params=_cparams(("parallel", "arbitrary"), 40),
        name="ssm",
    )(proj, proj, proj, tail, cw_x, cb_x, cw_b, cb_b, pad16(dt_bias), pad16(a_log),
      jnp.repeat(d_skip, SSM_HEAD_DIM)[None, :], ssd_norm_w[None, :], e_mat, e2_mat, tril)


def _prep_kernel(q_ref, k_ref, v_ref, qi_ref, ki_ref, wi_ref, ca_ref, sa_ref, ci_ref, si_ref, lg_ref, lb_ref,
                 qo_ref, ko_ref, vo_ref, qio_ref, ki0_ref, ki1_ref, wo_ref):
    ca, sa, ci, si = ca_ref[...], sa_ref[...], ci_ref[...], si_ref[...]
    lane = lax.broadcasted_iota(I32, ca.shape, 1)
    first32 = (lane % IDX_DIM) < (IDX_DIM // 2)
    q_scale = ATTN_HEAD_DIM ** -0.5

    def rope_attn(v):
        return v * ca + pltpu.roll(v, ATTN_HEAD_DIM // 2, 1) * sa

    def rope_idx(v):
        rot = jnp.where(first32, pltpu.roll(v, LANES - IDX_DIM // 2, 1), pltpu.roll(v, IDX_DIM // 2, 1))
        return v * ci + rot * si

    for h in range(ATTN_HEADS):
        sl = slice(h * LANES, (h + 1) * LANES)
        qo_ref[:, sl] = (rope_attn(q_ref[:, sl]) * q_scale).astype(BF16)
        ko_ref[:, sl] = rope_attn(k_ref[:, sl]).astype(BF16)
    vo_ref[0] = v_ref[...].T.astype(BF16)
    for p in range(IDX_HEADS * IDX_DIM // LANES):
        sl = slice(p * LANES, (p + 1) * LANES)
        qio_ref[:, sl] = rope_idx(qi_ref[:, sl]).astype(BF16)

    kraw = ki_ref[...]
    valid = lane < IDX_DIM
    mu = jnp.sum(kraw, axis=-1, keepdims=True) * (1.0 / IDX_DIM)
    dk = jnp.where(valid, kraw - mu, 0.0)
    var = jnp.sum(dk * dk, axis=-1, keepdims=True) * (1.0 / IDX_DIM)
    kn = jnp.where(valid, dk * lax.rsqrt(var + LN_EPS) * lg_ref[...] + lb_ref[...], 0.0)
    kr = jnp.where(valid, rope_idx(kn), 0.0)
    ki0_ref[...] = kr.astype(BF16)
    ki1_ref[...] = pltpu.roll(kr, IDX_DIM, 1).astype(BF16)
    wo_ref[...] = wi_ref[...] * (IDX_HEADS ** -0.5 * IDX_DIM ** -0.5)


def _prep(proj, tail, cos_a, sin_a, cos_i, sin_i, ln_g, ln_b):
    t = proj.shape[0]
    tm = ATT_KC
    w = D_ATTN
    pad64 = lambda v: jnp.pad(v, (0, LANES - IDX_DIM))[None, :]
    col = lambda c: pl.BlockSpec((tm, w), lambda i: (i, c // w))
    lane_blk = lambda c: pl.BlockSpec((tm, LANES), lambda i: (i, c))
    full = lambda: pl.BlockSpec((1, LANES), lambda i: (0, 0))
    return pl.pallas_call(
        _prep_kernel,
        grid=(t // tm,),
        in_specs=[col(COL_Q), col(COL_K), col(COL_V), col(COL_QI), lane_blk(1), lane_blk(2),
                  lane_blk(0), lane_blk(0), lane_blk(0), lane_blk(0), full(), full()],
        out_specs=[pl.BlockSpec((tm, w), lambda i: (i, 0))] * 2 + [pl.BlockSpec((1, w, tm), lambda i: (i, 0, 0))]
        + [pl.BlockSpec((tm, w), lambda i: (i, 0))] + [pl.BlockSpec((tm, LANES), lambda i: (i, 0))] * 3,
        out_shape=[jax.ShapeDtypeStruct((t, w), BF16)] * 2 + [jax.ShapeDtypeStruct((t // tm, w, tm), BF16)]
        + [jax.ShapeDtypeStruct((t, w), BF16)]
        + [jax.ShapeDtypeStruct((t, LANES), BF16)] * 2 + [jax.ShapeDtypeStruct((t, LANES), F32)],
        compiler_params=_cparams(("parallel",), 48),
        name="prep",
    )(proj, proj, proj, proj, tail, tail, cos_a, sin_a, cos_i, sin_i, pad64(ln_g), pad64(ln_b))


ATT_QB = 128
ATT_KC = 512


def _attn_kernel(q_ref, k_ref, vt_ref, qi_ref, ki0_ref, ki1_ref, w_ref, o_ref, sc_ref, m_ref, l_ref, acc_ref,
                 *, topk):
    i = pl.program_id(1)
    n_ch = (i * ATT_QB + ATT_QB + ATT_KC - 1) // ATT_KC
    key_off = lax.broadcasted_iota(I32, (ATT_KC, ATT_QB), 0)
    q_pos = i * ATT_QB + lax.broadcasted_iota(I32, (ATT_KC, ATT_QB), 1)
    w_t = w_ref[...].T

    def score_chunk(c, carry):
        start = pl.multiple_of(c * ATT_KC, ATT_KC)
        ki0 = ki0_ref[pl.ds(start, ATT_KC), :]
        ki1 = ki1_ref[pl.ds(start, ATT_KC), :]
        acc = jnp.zeros((ATT_KC, ATT_QB), F32)
        for h in range(IDX_HEADS):
            pair = qi_ref[:, (h // 2) * LANES:(h // 2 + 1) * LANES]
            rel = _dot_nt(ki0 if h % 2 == 0 else ki1, pair)
            acc = acc + jnp.maximum(rel, 0.0) * w_t[h:h + 1, :]
        sc_ref[c] = jnp.where(start + key_off <= q_pos, acc, -jnp.inf)
        return carry

    lax.fori_loop(0, n_ch, score_chunk, 0)

    def key_to_float(key):
        return pltpu.bitcast(jnp.where(key < 0, key ^ jnp.int32(0x7FFFFFFF), key), F32)

    def search_bit(it, ans):
        cand = ans ^ lax.shift_left(jnp.int32(1), jnp.int32(31) - it)
        cand_f = key_to_float(cand)

        def count_chunk(c, cnt):
            part = jnp.where(sc_ref[c] >= cand_f, 1, 0).astype(I32)
            return cnt + jnp.sum(part.reshape(ATT_KC // SUBLANES, SUBLANES, ATT_QB), axis=0)

        cnt = lax.fori_loop(0, n_ch, count_chunk, jnp.zeros((SUBLANES, ATT_QB), I32))
        total = jnp.sum(cnt, axis=0, keepdims=True)
        return jnp.where(total >= topk, cand, ans)

    kth = lax.fori_loop(0, 32, search_bit, jnp.full((1, ATT_QB), INT_MIN, I32))
    lowest_finite = jnp.float32(np.finfo(np.float32).min)
    thr = jnp.where(kth < KEY_LOWEST_FINITE, lowest_finite, key_to_float(kth))

    m_ref[...] = jnp.full(m_ref.shape, -1e30, F32)
    l_ref[...] = jnp.zeros(l_ref.shape, F32)
    acc_ref[...] = jnp.zeros(acc_ref.shape, F32)

    def att_chunk(c, carry):
        start = pl.multiple_of(c * ATT_KC, ATT_KC)
        sel = sc_ref[c] >= thr
        for h in range(ATTN_HEADS):
            sl = slice(h * LANES, (h + 1) * LANES)
            s = _dot_nt(k_ref[pl.ds(start, ATT_KC), sl], q_ref[:, sl])
            s = jnp.where(sel, s, -jnp.inf)
            m_old = m_ref[h:h + 1, :]
            m_new = jnp.maximum(m_old, jnp.max(s, axis=0, keepdims=True))
            p = jnp.exp(s - m_new)
            alpha = jnp.exp(m_old - m_new)
            l_ref[h:h + 1, :] = alpha * l_ref[h:h + 1, :] + jnp.sum(p, axis=0, keepdims=True)
            m_ref[h:h + 1, :] = m_new
            acc_ref[h] = alpha * acc_ref[h] + _dot(vt_ref[c, sl, :], p.astype(BF16))
        return carry

    lax.fori_loop(0, n_ch, att_chunk, 0)
    for h in range(ATTN_HEADS):
        o_ref[:, h * LANES:(h + 1) * LANES] = (acc_ref[h] / l_ref[h:h + 1, :]).T.astype(o_ref.dtype)


def _attn(q_r, k_r, v_t, qi_r, ki0, ki1, w_s, bsz, seq):
    t = q_r.shape[0]
    nq = seq // ATT_QB
    n_kc = seq // ATT_KC
    topk = min(INDEX_TOPK, seq // 4)
    once = pl.Buffered(1)
    qblk = lambda width: pl.BlockSpec((ATT_QB, width), lambda b, i: (b * nq + i, 0))
    per_batch = lambda width: pl.BlockSpec((seq, width), lambda b, i: (b, 0), pipeline_mode=once)
    return pl.pallas_call(
        functools.partial(_attn_kernel, topk=topk),
        grid=(bsz, nq),
        in_specs=[qblk(D_ATTN), per_batch(D_ATTN),
                  pl.BlockSpec((n_kc, D_ATTN, ATT_KC), lambda b, i: (b, 0, 0), pipeline_mode=once),
                  qblk(IDX_HEADS * IDX_DIM), per_batch(LANES), per_batch(LANES), qblk(LANES)],
        out_specs=qblk(D_ATTN),
        out_shape=jax.ShapeDtypeStruct((t, D_ATTN), BF16),
        scratch_shapes=[pltpu.VMEM((n_kc, ATT_KC, ATT_QB), F32),
                        pltpu.VMEM((ATTN_HEADS, ATT_QB), F32),
                        pltpu.VMEM((ATTN_HEADS, ATT_QB), F32),
                        pltpu.VMEM((ATTN_HEADS, ATTN_HEAD_DIM, ATT_QB), F32)],
        compiler_params=_cparams(("parallel", "arbitrary"), 48),
        name="attn",
    )(q_r, k_r, v_t, qi_r, ki0, ki1, w_s)


def _first_index_of_max(v, iota, n):
    m = jnp.max(v, axis=0, keepdims=True)
    idx = jnp.min(jnp.where(v == m, iota, n), axis=0, keepdims=True)
    return m, idx


def _route(scores, biased):
    tm = scores.shape[1]
    neg_inf = jnp.float32(-jnp.inf)
    iota8 = lax.broadcasted_iota(I32, (EXPERTS_PER_GROUP, tm), 0)
    group_scores = []
    for g in range(N_EXPERT_GROUPS):
        v = biased[g * EXPERTS_PER_GROUP:(g + 1) * EXPERTS_PER_GROUP, :]
        m1, i1 = _first_index_of_max(v, iota8, EXPERTS_PER_GROUP)
        m2 = jnp.max(jnp.where(iota8 == i1, neg_inf, v), axis=0, keepdims=True)
        group_scores.append(m1 + m2)
    gs = jnp.concatenate(group_scores, axis=0)
    keep = jnp.zeros(gs.shape, jnp.bool_)
    for _ in range(TOPK_GROUPS):
        _, gi = _first_index_of_max(gs, iota8, N_EXPERT_GROUPS)
        hit = iota8 == gi
        keep = jnp.logical_or(keep, hit)
        gs = jnp.where(hit, neg_inf, gs)
    keep_f = jnp.where(keep, 1.0, 0.0)
    masked = jnp.concatenate(
        [jnp.where(keep_f[g:g + 1, :] > 0.0, biased[g * EXPERTS_PER_GROUP:(g + 1) * EXPERTS_PER_GROUP, :], neg_inf)
         for g in range(N_EXPERT_GROUPS)], axis=0)
    iota64 = lax.broadcasted_iota(I32, (N_EXPERTS, tm), 0)
    sel_w = jnp.zeros((N_EXPERTS, tm), F32)
    for _ in range(TOP_K):
        _, ei = _first_index_of_max(masked, iota64, N_EXPERTS)
        hit = iota64 == ei
        sel_w = jnp.where(hit, scores, sel_w)
        masked = jnp.where(hit, neg_inf, masked)
    denom = jnp.sum(sel_w, axis=0, keepdims=True)
    return sel_w / denom * ROUTED_SCALE


def _outproj_kernel(ys_ref, ya_ref, wa_ref, wb_ref, x_ref, g1_ref, sc2_ref, sh2_ref, lg_ref, lb_ref,
                    wrh_ref, wrl_ref, rb_ref, x1_ref, u2_ref, gt_ref):
    mix = _dot(ys_ref[...], wa_ref[...]) + _dot(ya_ref[...], wb_ref[...])
    h = ALPHA * x_ref[...] + g1_ref[0] * mix
    x1 = _layer_norm_rows(h, lg_ref[...], lb_ref[...])
    x1_ref[...] = x1
    u2 = x1 * (1.0 + sc2_ref[0]) + sh2_ref[0]
    uh = u2.astype(BF16)
    ul = (u2 - uh.astype(F32)).astype(BF16)
    u2_ref[...] = uh
    wrh = wrh_ref[...]
    logits = _dot_nt(wrh, uh) + _dot_nt(wrh, ul) + _dot_nt(wrl_ref[...], uh)
    scores = 1.0 / (1.0 + jnp.exp(-logits))
    gt_ref[...] = _route(scores, scores + rb_ref[:, 0:1])


def _outproj(y_ssm, y_att, wo_a, wo_b, x2, gate1, scale2, shift2, ln_g, ln_b, wr_hi, wr_lo, rbias, seq):
    t, d = x2.shape
    tm = MOE_TILE
    tpb = seq // tm
    half = y_ssm.shape[1]
    rows = lambda width: pl.BlockSpec((tm, width), lambda i: (i, 0))
    full = lambda shape: pl.BlockSpec(shape, lambda i: (0,) * len(shape))
    mod = lambda: pl.BlockSpec((1, 1, d), lambda i: (i // tpb, 0, 0))
    return pl.pallas_call(
        _outproj_kernel,
        grid=(t // tm,),
        in_specs=[rows(half), rows(half), full((half, d)), full((half, d)), rows(d), mod(), mod(), mod(),
                  full((1, d)), full((1, d)), full((N_EXPERTS, d)), full((N_EXPERTS, d)), full((N_EXPERTS, LANES))],
        out_specs=[rows(d), rows(d), pl.BlockSpec((N_EXPERTS, tm), lambda i: (0, i))],
        out_shape=[jax.ShapeDtypeStruct((t, d), F32), jax.ShapeDtypeStruct((t, d), BF16),
                   jax.ShapeDtypeStruct((N_EXPERTS, t), F32)],
        compiler_params=_cparams(("parallel",), 48),
        name="outproj",
    )(y_ssm, y_att, wo_a, wo_b, x2, gate1, scale2, shift2, ln_g[None, :], ln_b[None, :], wr_hi, wr_lo, rbias)


MOE_TILE = 512
MOE_UNIT = 16
MOE_BM = 512
MOE_SUB = 128
MOE_RCAP = TOP_K * MOE_TILE + N_EXPERTS * MOE_UNIT
MOE_RCH = 512
MOE_GW = 3 * LANES
MOE_UNSET = 1e9


def _moe_plan(gates_t, n_tiles):
    upb = MOE_BM // MOE_UNIT
    ups = MOE_SUB // MOE_UNIT
    cnt = jnp.sum((gates_t > 0.0).reshape(N_EXPERTS, n_tiles, MOE_TILE), axis=2, dtype=I32).T
    nun = (cnt + MOE_UNIT - 1) // MOE_UNIT
    loff = jnp.cumsum(nun, axis=1) - nun
    tot = jnp.sum(nun, axis=0)
    tot_sub = (tot + ups - 1) // ups * ups
    nb = (tot + upb - 1) // upb
    cum_nb = jnp.cumsum(nb)
    first_blk = cum_nb - nb
    base = first_blk * upb
    goff = base[None, :] + jnp.cumsum(nun, axis=0) - nun
    n_blocks = (TOP_K * n_tiles * MOE_TILE // MOE_UNIT + n_tiles * N_EXPERTS + N_EXPERTS * (upb - 1) + upb - 1) // upb
    blk = jnp.arange(n_blocks, dtype=I32)
    blk_e = jnp.minimum(jnp.sum((cum_nb[None, :] <= blk[:, None]).astype(I32), axis=1), N_EXPERTS - 1)
    blk_rows = jnp.clip(tot_sub[blk_e] - (blk - first_blk[blk_e]) * upb, 0, upb) * MOE_UNIT
    loff_v = jnp.broadcast_to((loff * MOE_UNIT).astype(F32)[:, :, None], (n_tiles, N_EXPERTS, LANES))
    flat = lambda v: v.reshape(-1).astype(I32)
    return dict(nun=flat(nun), loff=flat(loff), goff=flat(goff), padstart=flat(base + tot), padn=flat(tot_sub - tot),
                blk_e=flat(blk_e), blk_rows=flat(blk_rows), nb_used=cum_nb[-1:].astype(I32), loff_v=loff_v,
                n_blocks=n_blocks)


def _unit_rows(unit):
    return pl.ds(pl.multiple_of(unit * MOE_UNIT, MOE_UNIT), MOE_UNIT)


def _one_hot_rows(row_id, targets, axis):
    p = jnp.zeros(row_id.shape, F32)
    for k in range(TOP_K):
        tgt = targets[k:k + 1, :] if axis == 0 else targets[:, k:k + 1]
        p = jnp.where(row_id == tgt, 1.0, p)
    return p.astype(BF16)


def _dispatch_kernel(nun_ref, loff_ref, goff_ref, pst_ref, pnn_ref, u_ref, g_ref, lv_ref, tri_ref,
                     xs_ref, lkt_ref, sorted_ref, xaug_ref, zero_ref, sem):
    j = pl.program_id(0)
    g = g_ref[...]
    sel = g > 0.0
    rank = _dot(jnp.where(sel, 1.0, 0.0).astype(BF16), tri_ref[...])
    loff = jnp.concatenate([lv_ref[0]] * (MOE_TILE // LANES), axis=1)
    rem = jnp.where(sel, loff + rank, MOE_UNSET)
    rows = []
    for _ in range(TOP_K):
        cur = jnp.min(rem, axis=0, keepdims=True)
        rows.append(cur)
        rem = jnp.where(rem == cur, MOE_UNSET, rem)
    lk = jnp.concatenate(rows, axis=0)
    lkt_ref[0] = jnp.concatenate([lk, jnp.full((LANES - TOP_K, MOE_TILE), MOE_UNSET, F32)], axis=0).T

    gt = jnp.concatenate([g, jnp.zeros_like(g)], axis=0).T
    hi = gt.astype(BF16)
    r1 = gt - hi.astype(F32)
    mid = r1.astype(BF16)
    xaug_ref[:, :D_MODEL] = u_ref[...]
    xaug_ref[:, D_MODEL:D_MODEL + LANES] = hi
    xaug_ref[:, D_MODEL + LANES:D_MODEL + 2 * LANES] = mid
    xaug_ref[:, D_MODEL + 2 * LANES:] = (r1 - mid.astype(F32)).astype(BF16)

    def sort_chunk(rc, carry):
        r0 = pl.multiple_of(rc * MOE_RCH, MOE_RCH)
        row_id = (lax.broadcasted_iota(I32, (MOE_RCH, MOE_TILE), 0) + r0).astype(F32)
        sorted_ref[pl.ds(r0, MOE_RCH), :] = _dot(_one_hot_rows(row_id, lk, 0), xaug_ref[...]).astype(BF16)
        return carry

    last = j * N_EXPERTS + N_EXPERTS - 1
    used_units = loff_ref[last] + nun_ref[last]
    lax.fori_loop(0, (used_units * MOE_UNIT + MOE_RCH - 1) // MOE_RCH, sort_chunk, 0)

    def unit_copy(src_unit, dst_unit):
        return pltpu.make_async_copy(sorted_ref.at[_unit_rows(src_unit)], xs_ref.at[_unit_rows(dst_unit)], sem)

    def send_segment(e, carry):
        seg = j * N_EXPERTS + e
        src0, dst0 = loff_ref[seg], goff_ref[seg]

        def send_unit(uu, c):
            unit_copy(src0 + uu, dst0 + uu).start()
            return c

        lax.fori_loop(0, nun_ref[seg], send_unit, 0)
        return carry

    lax.fori_loop(0, N_EXPERTS, send_segment, 0)

    def drain_unit(uu, c):
        unit_copy(0, 0).wait()
        return c

    lax.fori_loop(0, used_units, drain_unit, 0)

    @pl.when(j == pl.num_programs(0) - 1)
    def _():
        zero_ref[...] = jnp.zeros(zero_ref.shape, BF16)

        def pad_expert(e, carry):
            def pad_copy(uu):
                return pltpu.make_async_copy(zero_ref, xs_ref.at[_unit_rows(pst_ref[e] + uu)], sem)

            def start(uu, c):
                pad_copy(uu).start()
                return c

            def wait(uu, c):
                pad_copy(uu).wait()
                return c

            lax.fori_loop(0, pnn_ref[e], start, 0)
            lax.fori_loop(0, pnn_ref[e], wait, 0)
            return carry

        lax.fori_loop(0, N_EXPERTS, pad_expert, 0)


def _dispatch(u2, gates_t, plan):
    t, d = u2.shape
    n_tiles = t // MOE_TILE
    xw = d + MOE_GW
    tri = jnp.asarray(np.triu(np.ones((MOE_TILE, MOE_TILE), np.float32), k=1), BF16)
    grid_spec = pltpu.PrefetchScalarGridSpec(
        num_scalar_prefetch=5,
        grid=(n_tiles,),
        in_specs=[pl.BlockSpec((MOE_TILE, d), lambda j, *_: (j, 0)),
                  pl.BlockSpec((N_EXPERTS, MOE_TILE), lambda j, *_: (0, j)),
                  pl.BlockSpec((1, N_EXPERTS, LANES), lambda j, *_: (j, 0, 0)),
                  pl.BlockSpec((MOE_TILE, MOE_TILE), lambda j, *_: (0, 0))],
        out_specs=[pl.BlockSpec(memory_space=pl.ANY),
                   pl.BlockSpec((1, MOE_TILE, LANES), lambda j, *_: (j, 0, 0))],
        scratch_shapes=[pltpu.VMEM((MOE_RCAP, xw), BF16), pltpu.VMEM((MOE_TILE, xw), BF16),
                        pltpu.VMEM((MOE_UNIT, xw), BF16), pltpu.SemaphoreType.DMA(())])
    return pl.pallas_call(
        _dispatch_kernel,
        grid_spec=grid_spec,
        out_shape=[jax.ShapeDtypeStruct((plan["n_blocks"] * MOE_BM, xw), BF16),
                   jax.ShapeDtypeStruct((n_tiles, MOE_TILE, LANES), F32)],
        compiler_params=_cparams(("arbitrary",), 56),
        name="dispatch",
    )(plan["nun"], plan["loff"], plan["goff"], plan["padstart"], plan["padn"], u2, gates_t, plan["loff_v"], tri)


def _experts_kernel(be_ref, br_ref, nbu_ref, x_ref, wg_ref, wu_ref, wd_ref, y_ref, wgb_ref, wub_ref, wdb_ref):
    b = pl.program_id(0)

    @pl.when(b < nbu_ref[0])
    def _():
        e = be_ref[b]

        @pl.when(jnp.logical_or(b == 0, e != be_ref[jnp.maximum(b - 1, 0)]))
        def _():
            wgb_ref[...] = wg_ref[0].astype(BF16)
            wub_ref[...] = wu_ref[0].astype(BF16)
            wdb_ref[...] = wd_ref[0].astype(BF16)

        for rows in range(MOE_SUB, MOE_BM + 1, MOE_SUB):
            @pl.when(br_ref[b] == rows)
            def _(rows=rows):
                x = x_ref[:rows, :D_MODEL]
                g3 = ((x_ref[:rows, D_MODEL:D_MODEL + LANES].astype(F32)
                       + x_ref[:rows, D_MODEL + LANES:D_MODEL + 2 * LANES].astype(F32))
                      + x_ref[:rows, D_MODEL + 2 * LANES:].astype(F32))
                lane = lax.broadcasted_iota(I32, g3.shape, 1)
                gate = jnp.sum(jnp.where(lane == e, g3, 0.0), axis=1, keepdims=True)
                hid = (_silu(_dot(x, wgb_ref[...])) * _dot(x, wub_ref[...]) * gate).astype(BF16)
                y_ref[:rows, :] = _dot(hid, wdb_ref[...]).astype(y_ref.dtype)


def _experts(xs, plan, wg, wu, wd):
    ns, xw = xs.shape
    _, d, f = wg.shape
    blk = lambda b, nbu: jnp.minimum(b, nbu[0] - 1)
    grid_spec = pltpu.PrefetchScalarGridSpec(
        num_scalar_prefetch=3,
        grid=(ns // MOE_BM,),
        in_specs=[pl.BlockSpec((MOE_BM, xw), lambda b, be, br, nbu: (blk(b, nbu), 0)),
                  pl.BlockSpec((1, d, f), lambda b, be, br, nbu: (be[blk(b, nbu)], 0, 0)),
                  pl.BlockSpec((1, d, f), lambda b, be, br, nbu: (be[blk(b, nbu)], 0, 0)),
                  pl.BlockSpec((1, f, d), lambda b, be, br, nbu: (be[blk(b, nbu)], 0, 0))],
        out_specs=pl.BlockSpec((MOE_BM, d), lambda b, be, br, nbu: (blk(b, nbu), 0)),
        scratch_shapes=[pltpu.VMEM((d, f), BF16), pltpu.VMEM((d, f), BF16), pltpu.VMEM((f, d), BF16)])
    return pl.pallas_call(
        _experts_kernel,
        grid_spec=grid_spec,
        out_shape=jax.ShapeDtypeStruct((ns, d), BF16),
        compiler_params=_cparams(("arbitrary",), 52),
        name="experts",
    )(plan["blk_e"], plan["blk_rows"], plan["nb_used"], xs, wg, wu, wd)


def _final_kernel(nun_ref, loff_ref, goff_ref, lkt_ref, u_ref, wg_ref, wu_ref, wd_ref, x1_ref, g2_ref, lg_ref, lb_ref,
                  ys_ref, o_ref, ybuf_ref, sem):
    j = pl.program_id(0)

    @pl.when(j == 0)
    def _():
        ybuf_ref[...] = jnp.zeros(ybuf_ref.shape, ybuf_ref.dtype)

    def unit_copy(src_unit, dst_unit):
        return pltpu.make_async_copy(ys_ref.at[_unit_rows(src_unit)], ybuf_ref.at[_unit_rows(dst_unit)], sem)

    def fetch_segment(e, carry):
        seg = j * N_EXPERTS + e
        src0, dst0 = goff_ref[seg], loff_ref[seg]

        def fetch_unit(uu, c):
            unit_copy(src0 + uu, dst0 + uu).start()
            return c

        lax.fori_loop(0, nun_ref[seg], fetch_unit, 0)
        return carry

    lax.fori_loop(0, N_EXPERTS, fetch_segment, 0)

    u = u_ref[...]
    hid = (_silu(_dot(u, wg_ref[...])) * _dot(u, wu_ref[...])).astype(BF16)
    o_ref[...] = _dot(hid, wd_ref[...])

    last = j * N_EXPERTS + N_EXPERTS - 1
    used_units = loff_ref[last] + nun_ref[last]

    def drain_unit(uu, c):
        unit_copy(0, 0).wait()
        return c

    lax.fori_loop(0, used_units, drain_unit, 0)

    lkt = lkt_ref[0]

    def combine_chunk(rc, carry):
        r0 = pl.multiple_of(rc * MOE_RCH, MOE_RCH)
        row_id = (lax.broadcasted_iota(I32, (MOE_TILE, MOE_RCH), 1) + r0).astype(F32)
        o_ref[...] += _dot(_one_hot_rows(row_id, lkt, 1), ybuf_ref[pl.ds(r0, MOE_RCH), :])
        return carry

    lax.fori_loop(0, (used_units * MOE_UNIT + MOE_RCH - 1) // MOE_RCH, combine_chunk, 0)
    h = ALPHA * x1_ref[...] + g2_ref[0] * o_ref[...]
    o_ref[...] = _layer_norm_rows(h, lg_ref[...], lb_ref[...])


def _final(u2, wg_s, wu_s, wd_s, ys, lkt, plan, x1, gate2, ln_g, ln_b, seq):
    t, d = x1.shape
    tm = MOE_TILE
    tpb = seq // tm
    f = wg_s.shape[1]
    once = pl.Buffered(1)
    rows = lambda: pl.BlockSpec((tm, d), lambda j, *_: (j, 0))
    full = lambda shape: pl.BlockSpec(shape, lambda j, *_: (0,) * len(shape), pipeline_mode=once)
    grid_spec = pltpu.PrefetchScalarGridSpec(
        num_scalar_prefetch=3,
        grid=(t // tm,),
        in_specs=[pl.BlockSpec((1, tm, LANES), lambda j, *_: (j, 0, 0)), rows(),
                  full((d, f)), full((d, f)), full((f, d)), rows(),
                  pl.BlockSpec((1, 1, d), lambda j, *_: (j // tpb, 0, 0)), full((1, d)), full((1, d)),
                  pl.BlockSpec(memory_space=pl.ANY)],
        out_specs=rows(),
        scratch_shapes=[pltpu.VMEM((MOE_RCAP, d), BF16), pltpu.SemaphoreType.DMA(())])
    return pl.pallas_call(
        _final_kernel,
        grid_spec=grid_spec,
        out_shape=jax.ShapeDtypeStruct((t, d), F32),
        compiler_params=_cparams(("arbitrary",), 58),
        name="final",
    )(plan["nun"], plan["loff"], plan["goff"], lkt, u2, wg_s, wu_s, wd_s, x1, gate2, ln_g[None, :], ln_b[None, :], ys)


def _split_hi_lo(w):
    hi = w.astype(BF16)
    lo = (w - hi.astype(F32)).astype(BF16)
    return hi, lo


def _rope_tables(positions, dim, reps):
    inv_freq = 1.0 / (ROPE_THETA ** (jnp.arange(0, dim, 2, dtype=F32) / dim))
    ang = positions.astype(F32).reshape(-1)[:, None] * inv_freq
    cos, sin = jnp.cos(ang), jnp.sin(ang)
    return (jnp.tile(jnp.concatenate([cos, cos], axis=1), (1, reps)),
            jnp.tile(jnp.concatenate([-sin, sin], axis=1), (1, reps)))


def _layer(x2, c, positions, bsz, seq, w_ada, b_ada, w_in, conv_w, conv_b, dt_bias, a_log, d_skip, ssd_norm_w,
           idx_k_ln_g, idx_k_ln_b, w_out, ln1_g, ln1_b, w_router, router_bias, w_gate_e, w_up_e, w_down_e,
           w_gate_s, w_up_s, w_down_s, ln2_g, ln2_b, tables):
    d = D_MODEL
    mod = _ada(c, w_ada, b_ada).reshape(bsz, 6, 1, d)
    shift1, scale1, gate1, shift2, scale2, gate2 = [mod[:, k] for k in range(6)]

    o_z, o_xbc, o_dt, o_q = 0, D_SSM, 2 * D_SSM + 2 * SSM_GROUPS * SSM_STATE, 2 * D_SSM + 2 * SSM_GROUPS * SSM_STATE + SSM_HEADS
    o_k, o_v, o_qi = o_q + D_ATTN, o_q + 2 * D_ATTN, o_q + 3 * D_ATTN
    o_ki = o_qi + IDX_HEADS * IDX_DIM
    o_wi = o_ki + IDX_DIM
    w_main = jnp.concatenate([w_in[:, o_z:o_z + D_SSM], w_in[:, o_xbc:o_xbc + D_SSM], w_in[:, o_q:o_ki],
                              w_in[:, o_xbc + D_SSM:o_dt]], axis=1).astype(BF16)
    padc = lambda w: jnp.pad(w, ((0, 0), (0, LANES - w.shape[1])))
    w_tail = jnp.concatenate([padc(w_in[:, o_dt:o_q]), padc(w_in[:, o_ki:o_wi]), padc(w_in[:, o_wi:])], axis=1)
    wt_hi, wt_lo = _split_hi_lo(w_tail)

    proj, tail = _inproj(x2, scale1, shift1, w_main, wt_hi, wt_lo, seq)
    y_ssm = _ssm(proj, tail, conv_w, conv_b, dt_bias, a_log, d_skip, ssd_norm_w, bsz, seq)
    q_r, k_r, v_t, qi_r, ki0, ki1, w_s = _prep(proj, tail, *tables, idx_k_ln_g, idx_k_ln_b)
    y_att = _attn(q_r, k_r, v_t, qi_r, ki0, ki1, w_s, bsz, seq)

    wo = w_out.astype(BF16)
    wr_hi, wr_lo = _split_hi_lo(w_router.T)
    rbias = jnp.broadcast_to(router_bias[:, None], (N_EXPERTS, LANES))
    x1, u2, gates_t = _outproj(y_ssm, y_att, wo[:D_SSM], wo[D_SSM:], x2, gate1, scale2, shift2, ln1_g, ln1_b,
                               wr_hi, wr_lo, rbias, seq)
    plan = _moe_plan(gates_t, x2.shape[0] // MOE_TILE)
    xs, lkt = _dispatch(u2, gates_t, plan)
    ys = _experts(xs, plan, w_gate_e, w_up_e, w_down_e)
    return _final(u2, w_gate_s.astype(BF16), w_up_s.astype(BF16), w_down_s.astype(BF16), ys, lkt, plan, x1, gate2,
                  ln2_g, ln2_b, seq)


def kernel(x, c, positions, w_ada, b_ada, w_in, conv_w, conv_b, dt_bias, a_log, d_skip, ssd_norm_w, idx_k_ln_g, idx_k_ln_b, w_out, ln1_g, ln1_b, w_router, router_bias, w_gate_e, w_up_e, w_down_e, w_gate_s, w_up_s, w_down_s, ln2_g, ln2_b):
    bsz, seq, d = x.shape
    tables = _rope_tables(positions, ATTN_HEAD_DIM, 1) + _rope_tables(positions, IDX_DIM, 2)
    x2 = x.reshape(bsz * seq, d)
    for l in range(w_ada.shape[0]):
        x2 = _layer(x2, c, positions, bsz, seq, w_ada[l], b_ada[l], w_in[l], conv_w[l], conv_b[l], dt_bias[l],
                    a_log[l], d_skip[l], ssd_norm_w[l], idx_k_ln_g[l], idx_k_ln_b[l], w_out[l], ln1_g[l], ln1_b[l],
                    w_router[l], router_bias[l], w_gate_e[l], w_up_e[l], w_down_e[l], w_gate_s[l], w_up_s[l],
                    w_down_s[l], ln2_g[l], ln2_b[l], tables)
    return x2.reshape(bsz, seq, d)
```

```python
import functools
import math

import jax
import jax.numpy as jnp
import numpy as np
from jax import lax
from jax.experimental import pallas as pl
from jax.experimental.pallas import tpu as pltpu

F32 = jnp.float32
BF16 = jnp.bfloat16
I32 = jnp.int32
HIGHEST = lax.Precision.HIGHEST

D_MODEL = 2048
D_SSM = 1024
D_ATTN = 1024
SSM_HEAD_DIM = 64
SSM_HEADS = 16
SSM_GROUPS = 2
SSM_STATE = 128
CONV_WIDTH = 4
SSD_CHUNK = 128
ATTN_HEAD_DIM = 128
ATTN_HEADS = 8
IDX_HEADS = 16
IDX_DIM = 64
INDEX_TOPK = 256
ROPE_THETA = 10000.0
N_EXPERTS = 64
N_EXPERT_GROUPS = 8
EXPERTS_PER_GROUP = 8
TOPK_GROUPS = 4
TOP_K = 8
D_EXPERT = 512
D_SHARED = 512
ROUTED_SCALE = 2.5
DEPTH = 1
ALPHA = (2.0 * DEPTH) ** 0.25
LN_EPS = 1e-5

LANES = 128
SUBLANES = 8
VMEM_BYTES_V7X = 64 * 1024 * 1024
INT_MIN = -(2 ** 31)
KEY_LOWEST_FINITE = INT_MIN + 0x00800000

COL_Z, COL_XS, COL_Q, COL_K, COL_V, COL_QI, COL_BC = 0, 1024, 2048, 3072, 4096, 5120, 6144
N_MAIN = 6656
N_TAIL = 3 * LANES


def _cparams(sem, vmem_mb):
    return pltpu.CompilerParams(dimension_semantics=sem, vmem_limit_bytes=vmem_mb * 1024 * 1024)


def _silu(v):
    return v * (1.0 / (1.0 + jnp.exp(-v)))


def _dot(a, b, precision=None):
    return jnp.dot(a, b, preferred_element_type=F32, precision=precision)


def _dot_nt(a, b, precision=None):
    return lax.dot_general(a, b, (((1,), (1,)), ((), ())), preferred_element_type=F32, precision=precision)


def _split3(x):
    hi = x.astype(BF16)
    r1 = x - hi.astype(F32)
    mid = r1.astype(BF16)
    return hi, mid, (r1 - mid.astype(F32)).astype(BF16)


def _layer_norm_rows(h, g, b):
    mu = jnp.mean(h, axis=-1, keepdims=True)
    d = h - mu
    var = jnp.mean(d * d, axis=-1, keepdims=True)
    return d * lax.rsqrt(var + LN_EPS) * g + b


def _ada_kernel(cb_ref, w_ref, b_ref, o_ref):
    tn = w_ref.shape[1]
    for bi in range(cb_ref.shape[0]):
        cb = _silu(cb_ref[bi])
        cols = [jnp.sum(w_ref[:, j * LANES:(j + 1) * LANES] * cb, axis=0, keepdims=True)
                for j in range(tn // LANES)]
        o_ref[bi:bi + 1, :] = jnp.concatenate(cols, axis=1) + b_ref[...]


def _ada(c, w_ada, b_ada):
    bsz, d = c.shape
    n = w_ada.shape[1]
    tn = 1024
    cb = jnp.broadcast_to(c[:, :, None], (bsz, d, LANES))
    return pl.pallas_call(
        _ada_kernel,
        grid=(n // tn,),
        in_specs=[pl.BlockSpec((bsz, d, LANES), lambda j: (0, 0, 0)),
                  pl.BlockSpec((d, tn), lambda j: (0, j)),
                  pl.BlockSpec((1, tn), lambda j: (0, j))],
        out_specs=pl.BlockSpec((bsz, tn), lambda j: (0, j)),
        out_shape=jax.ShapeDtypeStruct((bsz, n), F32),
        compiler_params=_cparams(("parallel",), 40),
        name="ada",
    )(cb, w_ada, b_ada.reshape(1, n))


def _inproj_kernel(x_ref, sc_ref, sh_ref, w_ref, wth_ref, wtl_ref, o_ref, t_ref, u_ref):
    @pl.when(pl.program_id(1) == 0)
    def _():
        u = x_ref[...] * (1.0 + sc_ref[0]) + sh_ref[0]
        uh = u.astype(BF16)
        ul = (u - uh.astype(F32)).astype(BF16)
        u_ref[...] = uh
        t_ref[...] = (_dot(uh, wth_ref[...]) + _dot(uh, wtl_ref[...]) + _dot(ul, wth_ref[...]))

    o_ref[...] = _dot(u_ref[...], w_ref[...])


def _inproj(x2, scale1, shift1, w_main, wt_hi, wt_lo, seq):
    t, d = x2.shape
    tm, tn = 1024, 512
    tpb = seq // tm
    return pl.pallas_call(
        _inproj_kernel,
        grid=(t // tm, N_MAIN // tn),
        in_specs=[pl.BlockSpec((tm, d), lambda i, j: (i, 0)),
                  pl.BlockSpec((1, 1, d), lambda i, j: (i // tpb, 0, 0)),
                  pl.BlockSpec((1, 1, d), lambda i, j: (i // tpb, 0, 0)),
                  pl.BlockSpec((d, tn), lambda i, j: (0, j)),
                  pl.BlockSpec((d, N_TAIL), lambda i, j: (0, 0)),
                  pl.BlockSpec((d, N_TAIL), lambda i, j: (0, 0))],
        out_specs=[pl.BlockSpec((tm, tn), lambda i, j: (i, j)),
                   pl.BlockSpec((tm, N_TAIL), lambda i, j: (i, 0))],
        out_shape=[jax.ShapeDtypeStruct((t, N_MAIN), F32),
                   jax.ShapeDtypeStruct((t, N_TAIL), F32)],
        scratch_shapes=[pltpu.VMEM((tm, d), BF16)],
        compiler_params=_cparams(("parallel", "arbitrary"), 48),
        name="inproj",
    )(x2, scale1, shift1, w_main, wt_hi, wt_lo)


def _ssm_kernel(z_ref, xs_ref, bc_ref, dt_ref, cwx_ref, cbx_ref, cwb_ref, cbb_ref, dtb_ref, alog_ref,
                dsk_ref, nw_ref, e_ref, e2_ref, tril_ref, o_ref, px_ref, pb_ref, st_ref):
    q = SSD_CHUNK
    hpg = SSM_HEADS // SSM_GROUPS
    gw = hpg * SSM_HEAD_DIM

    @pl.when(pl.program_id(1) == 0)
    def _():
        px_ref[0:SUBLANES, :] = jnp.zeros((SUBLANES, px_ref.shape[1]), F32)
        pb_ref[0:SUBLANES, :] = jnp.zeros((SUBLANES, pb_ref.shape[1]), F32)
        st_ref[...] = jnp.zeros(st_ref.shape, F32)

    def conv_silu(raw_ref, pad_ref, w_ref, b_ref):
        pad_ref[SUBLANES:SUBLANES + q, :] = raw_ref[...]
        acc = b_ref[...] + w_ref[0:1, :] * pad_ref[SUBLANES - 3:SUBLANES - 3 + q, :]
        for k in range(1, CONV_WIDTH):
            acc = acc + w_ref[k:k + 1, :] * pad_ref[SUBLANES - 3 + k:SUBLANES - 3 + k + q, :]
        pad_ref[0:SUBLANES, :] = raw_ref[q - SUBLANES:q, :]
        return _silu(acc)

    xs = conv_silu(xs_ref, px_ref, cwx_ref, cbx_ref)
    bc = conv_silu(bc_ref, pb_ref, cwb_ref, cbb_ref)

    dtr = dt_ref[...] + dtb_ref[...]
    dt = jnp.maximum(dtr, 0.0) + jnp.log(1.0 + jnp.exp(-jnp.abs(dtr)))
    log_a = dt * (-jnp.exp(alog_ref[...]))
    cs = sum(_dot(tril_ref[...], part) for part in _split3(log_a))
    cs_parts = _split3(cs)
    cs_e = sum(_dot(part, e_ref[...]) for part in cs_parts)
    dt_e = sum(_dot(part, e_ref[...]) for part in _split3(dt))
    cs_col = sum(_dot(part, e2_ref[...]) for part in cs_parts)
    cs_t = cs.T
    cs_last = cs_e[q - 1:q, :]

    xdt = xs * dt_e
    rows = lax.broadcasted_iota(I32, (q, q), 0)
    cols = lax.broadcasted_iota(I32, (q, q), 1)
    causal = rows >= cols
    first_half = lax.broadcasted_iota(I32, (q, LANES), 1) < SSM_HEAD_DIM

    y_pairs = []
    for g in range(SSM_GROUPS):
        b_g = bc[:, g * SSM_STATE:(g + 1) * SSM_STATE]
        c_g = bc[:, (SSM_GROUPS + g) * SSM_STATE:(SSM_GROUPS + g + 1) * SSM_STATE]
        cb = _dot_nt(c_g.astype(BF16), b_g.astype(BF16))
        for hp in range(hpg // 2):
            pair = g * (hpg // 2) + hp
            x_pair = xdt[:, pair * LANES:(pair + 1) * LANES].astype(BF16)
            ys = []
            for sub in range(2):
                h = 2 * pair + sub
                seg = cs_col[:, h * LANES:(h + 1) * LANES] - cs_t[h:h + 1, :]
                dec = jnp.exp(jnp.where(causal, seg, -jnp.inf))
                ys.append(_dot((cb * dec).astype(BF16), x_pair))
            y_pairs.append(jnp.where(first_half, ys[0], ys[1]))
    y_diag = jnp.concatenate(y_pairs, axis=1)

    xw = xdt * jnp.exp(cs_last - cs_e)
    y_off, new_states = [], []
    for g in range(SSM_GROUPS):
        b_g = bc[:, g * SSM_STATE:(g + 1) * SSM_STATE]
        c_g = bc[:, (SSM_GROUPS + g) * SSM_STATE:(SSM_GROUPS + g + 1) * SSM_STATE]
        h_in = st_ref[:, g * gw:(g + 1) * gw]
        y_off.append(_dot(c_g.astype(BF16), h_in.astype(BF16)))
        new_states.append(_dot(b_g.T.astype(BF16), xw[:, g * gw:(g + 1) * gw].astype(BF16)))
    y_off = jnp.concatenate(y_off, axis=1) * jnp.exp(cs_e)
    st_ref[...] = jnp.exp(cs_last) * st_ref[...] + jnp.concatenate(new_states, axis=1)

    y = y_diag + y_off + dsk_ref[...] * xs
    yf = y * _silu(z_ref[...])
    ms = jnp.mean(yf * yf, axis=-1, keepdims=True)
    o_ref[...] = (yf * lax.rsqrt(ms + LN_EPS) * nw_ref[...]).astype(o_ref.dtype)


def _ssm(proj, tail, conv_w, conv_b, dt_bias, a_log, d_skip, ssd_norm_w, bsz, seq):
    t = proj.shape[0]
    q = SSD_CHUNK
    n_c = seq // q
    nbc = 2 * SSM_GROUPS * SSM_STATE
    cw_x, cw_b = conv_w[:, :D_SSM], conv_w[:, D_SSM:]
    cb_x, cb_b = conv_b[None, :D_SSM], conv_b[None, D_SSM:]
    pad16 = lambda v: jnp.pad(v, (0, LANES - SSM_HEADS))[None, :]
    head_of_lane = np.arange(D_SSM) // SSM_HEAD_DIM
    e_mat = jnp.asarray((np.arange(LANES)[:, None] == head_of_lane[None, :]).astype(np.float32), BF16)
    e2_mat = jnp.asarray((np.arange(LANES)[:, None] == (np.arange(SSM_HEADS * LANES) // LANES)[None, :])
                         .astype(np.float32), BF16)
    tril = jnp.asarray(np.tril(np.ones((q, q), np.float32)), BF16)
    row = lambda b, c: b * n_c + c
    full = lambda shape: pl.BlockSpec(shape, lambda b, c: (0,) * len(shape))
    return pl.pallas_call(
        _ssm_kernel,
        grid=(bsz, n_c),
        in_specs=[pl.BlockSpec((q, D_SSM), lambda b, c: (row(b, c), COL_Z // D_SSM)),
                  pl.BlockSpec((q, D_SSM), lambda b, c: (row(b, c), COL_XS // D_SSM)),
                  pl.BlockSpec((q, nbc), lambda b, c: (row(b, c), COL_BC // nbc)),
                  pl.BlockSpec((q, LANES), lambda b, c: (row(b, c), 0)),
                  full((CONV_WIDTH, D_SSM)), full((1, D_SSM)), full((CONV_WIDTH, nbc)), full((1, nbc)),
                  full((1, LANES)), full((1, LANES)), full((1, D_SSM)), full((1, D_SSM)),
                  full((LANES, D_SSM)), full((LANES, SSM_HEADS * LANES)), full((q, q))],
        out_specs=pl.BlockSpec((q, D_SSM), lambda b, c: (row(b, c), 0)),
        out_shape=jax.ShapeDtypeStruct((t, D_SSM), BF16),
        scratch_shapes=[pltpu.VMEM((SUBLANES + q, D_SSM), F32),
                        pltpu.VMEM((SUBLANES + q, nbc), F32),
                        pltpu.VMEM((SSM_STATE, D_SSM), F32)],
        compiler_params=_cparams(("parallel", "arbitrary"), 40),
        name="ssm",
    )(proj, proj, proj, tail, cw_x, cb_x, cw_b, cb_b, pad16(dt_bias), pad16(a_log),
      jnp.repeat(d_skip, SSM_HEAD_DIM)[None, :], ssd_norm_w[None, :], e_mat, e2_mat, tril)


def _prep_kernel(q_ref, k_ref, v_ref, qi_ref, ki_ref, wi_ref, ca_ref, sa_ref, ci_ref, si_ref, lg_ref, lb_ref,
                 qo_ref, ko_ref, vo_ref, qio_ref, ki0_ref, ki1_ref, wo_ref):
    ca, sa, ci, si = ca_ref[...], sa_ref[...], ci_ref[...], si_ref[...]
    lane = lax.broadcasted_iota(I32, ca.shape, 1)
    first32 = (lane % IDX_DIM) < (IDX_DIM // 2)
    q_scale = ATTN_HEAD_DIM ** -0.5

    def rope_attn(v):
        return v * ca + pltpu.roll(v, ATTN_HEAD_DIM // 2, 1) * sa

    def rope_idx(v):
        rot = jnp.where(first32, pltpu.roll(v, LANES - IDX_DIM // 2, 1), pltpu.roll(v, IDX_DIM // 2, 1))
        return v * ci + rot * si

    for h in range(ATTN_HEADS):
        sl = slice(h * LANES, (h + 1) * LANES)
        qo_ref[:, sl] = (rope_attn(q_ref[:, sl]) * q_scale).astype(BF16)
        ko_ref[:, sl] = rope_attn(k_ref[:, sl]).astype(BF16)
    vo_ref[0] = v_ref[...].T.astype(BF16)
    for p in range(IDX_HEADS * IDX_DIM // LANES):
        sl = slice(p * LANES, (p + 1) * LANES)
        qio_ref[:, sl] = rope_idx(qi_ref[:, sl]).astype(BF16)

    kraw = ki_ref[...]
    valid = lane < IDX_DIM
    mu = jnp.sum(kraw, axis=-1, keepdims=True) * (1.0 / IDX_DIM)
    dk = jnp.where(valid, kraw - mu, 0.0)
    var = jnp.sum(dk * dk, axis=-1, keepdims=True) * (1.0 / IDX_DIM)
    kn = jnp.where(valid, dk * lax.rsqrt(var + LN_EPS) * lg_ref[...] + lb_ref[...], 0.0)
    kr = jnp.where(valid, rope_idx(kn), 0.0)
    ki0_ref[...] = kr.astype(BF16)
    ki1_ref[...] = pltpu.roll(kr, IDX_DIM, 1).astype(BF16)
    wo_ref[...] = wi_ref[...] * (IDX_HEADS ** -0.5 * IDX_DIM ** -0.5)


def _prep(proj, tail, cos_a, sin_a, cos_i, sin_i, ln_g, ln_b):
    t = proj.shape[0]
    tm = ATT_KC
    w = D_ATTN
    pad64 = lambda v: jnp.pad(v, (0, LANES - IDX_DIM))[None, :]
    col = lambda c: pl.BlockSpec((tm, w), lambda i: (i, c // w))
    lane_blk = lambda c: pl.BlockSpec((tm, LANES), lambda i: (i, c))
    full = lambda: pl.BlockSpec((1, LANES), lambda i: (0, 0))
    return pl.pallas_call(
        _prep_kernel,
        grid=(t // tm,),
        in_specs=[col(COL_Q), col(COL_K), col(COL_V), col(COL_QI), lane_blk(1), lane_blk(2),
                  lane_blk(0), lane_blk(0), lane_blk(0), lane_blk(0), full(), full()],
        out_specs=[pl.BlockSpec((tm, w), lambda i: (i, 0))] * 2 + [pl.BlockSpec((1, w, tm), lambda i: (i, 0, 0))]
        + [pl.BlockSpec((tm, w), lambda i: (i, 0))] + [pl.BlockSpec((tm, LANES), lambda i: (i, 0))] * 3,
        out_shape=[jax.ShapeDtypeStruct((t, w), BF16)] * 2 + [jax.ShapeDtypeStruct((t // tm, w, tm), BF16)]
        + [jax.ShapeDtypeStruct((t, w), BF16)]
        + [jax.ShapeDtypeStruct((t, LANES), BF16)] * 2 + [jax.ShapeDtypeStruct((t, LANES), F32)],
        compiler_params=_cparams(("parallel",), 48),
        name="prep",
    )(proj, proj, proj, proj, tail, tail, cos_a, sin_a, cos_i, sin_i, pad64(ln_g), pad64(ln_b))


ATT_QB = 128
ATT_KC = 512


def _attn_kernel(q_ref, k_ref, vt_ref, qi_ref, ki0_ref, ki1_ref, w_ref, o_ref, sc_ref, m_ref, l_ref, acc_ref,
                 *, topk):
    i = pl.program_id(1)
    n_ch = (i * ATT_QB + ATT_QB + ATT_KC - 1) // ATT_KC
    key_off = lax.broadcasted_iota(I32, (ATT_KC, ATT_QB), 0)
    q_pos = i * ATT_QB + lax.broadcasted_iota(I32, (ATT_KC, ATT_QB), 1)
    w_t = w_ref[...].T

    def score_chunk(c, carry):
        start = pl.multiple_of(c * ATT_KC, ATT_KC)
        ki0 = ki0_ref[pl.ds(start, ATT_KC), :]
        ki1 = ki1_ref[pl.ds(start, ATT_KC), :]
        acc = jnp.zeros((ATT_KC, ATT_QB), F32)
        for h in range(IDX_HEADS):
            pair = qi_ref[:, (h // 2) * LANES:(h // 2 + 1) * LANES]
            rel = _dot_nt(ki0 if h % 2 == 0 else ki1, pair)
            acc = acc + jnp.maximum(rel, 0.0) * w_t[h:h + 1, :]
        sc_ref[c] = jnp.where(start + key_off <= q_pos, acc, -jnp.inf)
        return carry

    lax.fori_loop(0, n_ch, score_chunk, 0)

    def key_to_float(key):
        return pltpu.bitcast(jnp.where(key < 0, key ^ jnp.int32(0x7FFFFFFF), key), F32)

    def search_bit(it, ans):
        cand = ans ^ lax.shift_left(jnp.int32(1), jnp.int32(31) - it)
        cand_f = key_to_float(cand)

        def count_chunk(c, cnt):
            part = jnp.where(sc_ref[c] >= cand_f, 1, 0).astype(I32)
            return cnt + jnp.sum(part.reshape(ATT_KC // SUBLANES, SUBLANES, ATT_QB), axis=0)

        cnt = lax.fori_loop(0, n_ch, count_chunk, jnp.zeros((SUBLANES, ATT_QB), I32))
        total = jnp.sum(cnt, axis=0, keepdims=True)
        return jnp.where(total >= topk, cand, ans)

    kth = lax.fori_loop(0, 32, search_bit, jnp.full((1, ATT_QB), INT_MIN, I32))
    lowest_finite = jnp.float32(np.finfo(np.float32).min)
    thr = jnp.where(kth < KEY_LOWEST_FINITE, lowest_finite, key_to_float(kth))

    m_ref[...] = jnp.full(m_ref.shape, -1e30, F32)
    l_ref[...] = jnp.zeros(l_ref.shape, F32)
    acc_ref[...] = jnp.zeros(acc_ref.shape, F32)

    def att_chunk(c, carry):
        start = pl.multiple_of(c * ATT_KC, ATT_KC)
        sel = sc_ref[c] >= thr
        for h in range(ATTN_HEADS):
            sl = slice(h * LANES, (h + 1) * LANES)
            s = _dot_nt(k_ref[pl.ds(start, ATT_KC), sl], q_ref[:, sl])
            s = jnp.where(sel, s, -jnp.inf)
            m_old = m_ref[h:h + 1, :]
            m_new = jnp.maximum(m_old, jnp.max(s, axis=0, keepdims=True))
            p = jnp.exp(s - m_new)
            alpha = jnp.exp(m_old - m_new)
            l_ref[h:h + 1, :] = alpha * l_ref[h:h + 1, :] + jnp.sum(p, axis=0, keepdims=True)
            m_ref[h:h + 1, :] = m_new
            acc_ref[h] = alpha * acc_ref[h] + _dot(vt_ref[c, sl, :], p.astype(BF16))
        return carry

    lax.fori_loop(0, n_ch, att_chunk, 0)
    for h in range(ATTN_HEADS):
        o_ref[:, h * LANES:(h + 1) * LANES] = (acc_ref[h] / l_ref[h:h + 1, :]).T.astype(o_ref.dtype)


def _attn(q_r, k_r, v_t, qi_r, ki0, ki1, w_s, bsz, seq):
    t = q_r.shape[0]
    nq = seq // ATT_QB
    n_kc = seq // ATT_KC
    topk = min(INDEX_TOPK, seq // 4)
    once = pl.Buffered(1)
    qblk = lambda width: pl.BlockSpec((ATT_QB, width), lambda b, i: (b * nq + i, 0))
    per_batch = lambda width: pl.BlockSpec((seq, width), lambda b, i: (b, 0), pipeline_mode=once)
    return pl.pallas_call(
        functools.partial(_attn_kernel, topk=topk),
        grid=(bsz, nq),
        in_specs=[qblk(D_ATTN), per_batch(D_ATTN),
                  pl.BlockSpec((n_kc, D_ATTN, ATT_KC), lambda b, i: (b, 0, 0), pipeline_mode=once),
                  qblk(IDX_HEADS * IDX_DIM), per_batch(LANES), per_batch(LANES), qblk(LANES)],
        out_specs=qblk(D_ATTN),
        out_shape=jax.ShapeDtypeStruct((t, D_ATTN), BF16),
        scratch_shapes=[pltpu.VMEM((n_kc, ATT_KC, ATT_QB), F32),
                        pltpu.VMEM((ATTN_HEADS, ATT_QB), F32),
                        pltpu.VMEM((ATTN_HEADS, ATT_QB), F32),
                        pltpu.VMEM((ATTN_HEADS, ATTN_HEAD_DIM, ATT_QB), F32)],
        compiler_params=_cparams(("parallel", "arbitrary"), 48),
        name="attn",
    )(q_r, k_r, v_t, qi_r, ki0, ki1, w_s)


def _first_index_of_max(v, iota, n):
    m = jnp.max(v, axis=0, keepdims=True)
    idx = jnp.min(jnp.where(v == m, iota, n), axis=0, keepdims=True)
    return m, idx


def _route(scores, biased):
    tm = scores.shape[1]
    neg_inf = jnp.float32(-jnp.inf)
    iota8 = lax.broadcasted_iota(I32, (EXPERTS_PER_GROUP, tm), 0)
    group_scores = []
    for g in range(N_EXPERT_GROUPS):
        v = biased[g * EXPERTS_PER_GROUP:(g + 1) * EXPERTS_PER_GROUP, :]
        m1, i1 = _first_index_of_max(v, iota8, EXPERTS_PER_GROUP)
        m2 = jnp.max(jnp.where(iota8 == i1, neg_inf, v), axis=0, keepdims=True)
        group_scores.append(m1 + m2)
    gs = jnp.concatenate(group_scores, axis=0)
    keep = jnp.zeros(gs.shape, jnp.bool_)
    for _ in range(TOPK_GROUPS):
        _, gi = _first_index_of_max(gs, iota8, N_EXPERT_GROUPS)
        hit = iota8 == gi
        keep = jnp.logical_or(keep, hit)
        gs = jnp.where(hit, neg_inf, gs)
    keep_f = jnp.where(keep, 1.0, 0.0)
    masked = jnp.concatenate(
        [jnp.where(keep_f[g:g + 1, :] > 0.0, biased[g * EXPERTS_PER_GROUP:(g + 1) * EXPERTS_PER_GROUP, :], neg_inf)
         for g in range(N_EXPERT_GROUPS)], axis=0)
    iota64 = lax.broadcasted_iota(I32, (N_EXPERTS, tm), 0)
    sel_w = jnp.zeros((N_EXPERTS, tm), F32)
    for _ in range(TOP_K):
        _, ei = _first_index_of_max(masked, iota64, N_EXPERTS)
        hit = iota64 == ei
        sel_w = jnp.where(hit, scores, sel_w)
        masked = jnp.where(hit, neg_inf, masked)
    denom = jnp.sum(sel_w, axis=0, keepdims=True)
    return sel_w / denom * ROUTED_SCALE


def _outproj_kernel(ys_ref, ya_ref, wa_ref, wb_ref, x_ref, g1_ref, sc2_ref, sh2_ref, lg_ref, lb_ref,
                    wrh_ref, wrl_ref, rb_ref, x1_ref, u2_ref, gt_ref):
    mix = _dot(ys_ref[...], wa_ref[...]) + _dot(ya_ref[...], wb_ref[...])
    h = ALPHA * x_ref[...] + g1_ref[0] * mix
    x1 = _layer_norm_rows(h, lg_ref[...], lb_ref[...])
    x1_ref[...] = x1
    u2 = x1 * (1.0 + sc2_ref[0]) + sh2_ref[0]
    uh = u2.astype(BF16)
    ul = (u2 - uh.astype(F32)).astype(BF16)
    u2_ref[...] = uh
    wrh = wrh_ref[...]
    logits = _dot_nt(wrh, uh) + _dot_nt(wrh, ul) + _dot_nt(wrl_ref[...], uh)
    scores = 1.0 / (1.0 + jnp.exp(-logits))
    gt_ref[...] = _route(scores, scores + rb_ref[:, 0:1])


def _outproj(y_ssm, y_att, wo_a, wo_b, x2, gate1, scale2, shift2, ln_g, ln_b, wr_hi, wr_lo, rbias, seq):
    t, d = x2.shape
    tm = MOE_TILE
    tpb = seq // tm
    half = y_ssm.shape[1]
    rows = lambda width: pl.BlockSpec((tm, width), lambda i: (i, 0))
    full = lambda shape: pl.BlockSpec(shape, lambda i: (0,) * len(shape))
    mod = lambda: pl.BlockSpec((1, 1, d), lambda i: (i // tpb, 0, 0))
    return pl.pallas_call(
        _outproj_kernel,
        grid=(t // tm,),
        in_specs=[rows(half), rows(half), full((half, d)), full((half, d)), rows(d), mod(), mod(), mod(),
                  full((1, d)), full((1, d)), full((N_EXPERTS, d)), full((N_EXPERTS, d)), full((N_EXPERTS, LANES))],
        out_specs=[rows(d), rows(d), pl.BlockSpec((N_EXPERTS, tm), lambda i: (0, i))],
        out_shape=[jax.ShapeDtypeStruct((t, d), F32), jax.ShapeDtypeStruct((t, d), BF16),
                   jax.ShapeDtypeStruct((N_EXPERTS, t), F32)],
        compiler_params=_cparams(("parallel",), 48),
        name="outproj",
    )(y_ssm, y_att, wo_a, wo_b, x2, gate1, scale2, shift2, ln_g[None, :], ln_b[None, :], wr_hi, wr_lo, rbias)


MOE_TILE = 512
MOE_UNIT = 16
MOE_BM = 512
MOE_SUB = 128
MOE_RCAP = TOP_K * MOE_TILE + N_EXPERTS * MOE_UNIT
MOE_RCH = 512
MOE_GW = 3 * LANES
MOE_UNSET = 1e9


def _moe_plan(gates_t, n_tiles):
    upb = MOE_BM // MOE_UNIT
    ups = MOE_SUB // MOE_UNIT
    cnt = jnp.sum((gates_t > 0.0).reshape(N_EXPERTS, n_tiles, MOE_TILE), axis=2, dtype=I32).T
    nun = (cnt + MOE_UNIT - 1) // MOE_UNIT
    lend = jnp.cumsum(nun, axis=1)
    loff = lend - nun
    tot = jnp.sum(nun, axis=0)
    tot_sub = (tot + ups - 1) // ups * ups
    nb = (tot + upb - 1) // upb
    cum_nb = jnp.cumsum(nb)
    first_blk = cum_nb - nb
    base = first_blk * upb
    goff = base[None, :] + jnp.cumsum(nun, axis=0) - nun
    unit = jnp.arange(MOE_RCAP // MOE_UNIT, dtype=I32)
    seg = jnp.minimum(jnp.sum((lend[:, None, :] <= unit[None, :, None]).astype(I32), axis=2), N_EXPERTS - 1)
    unit_dst = (jnp.take_along_axis(goff, seg, axis=1) + unit[None, :] - jnp.take_along_axis(loff, seg, axis=1))
    n_blocks = (TOP_K * n_tiles * MOE_TILE // MOE_UNIT + n_tiles * N_EXPERTS + N_EXPERTS * (upb - 1) + upb - 1) // upb
    step = jnp.arange(n_blocks, dtype=I32)
    step_e = jnp.minimum(jnp.sum((cum_nb[None, :] <= step[:, None]).astype(I32), axis=1), N_EXPERTS - 1)
    nb_e = jnp.maximum(nb[step_e], 1)
    local = (step - first_blk[step_e] + nb_e - 1) % nb_e
    step_rows = jnp.clip(tot_sub[step_e] - local * upb, 0, upb) * MOE_UNIT
    loff_v = jnp.broadcast_to((loff * MOE_UNIT).astype(F32)[:, :, None], (n_tiles, N_EXPERTS, LANES))
    flat = lambda v: v.reshape(-1).astype(I32)
    return dict(unit_dst=flat(unit_dst), used=flat(lend[:, -1]), padstart=flat(base + tot), padn=flat(tot_sub - tot),
                step_e=flat(step_e), step_blk=flat(first_blk[step_e] + local), step_rows=flat(step_rows),
                nb_used=cum_nb[-1:].astype(I32), loff_v=loff_v, n_blocks=n_blocks)


def _unit_rows(unit):
    return pl.ds(pl.multiple_of(unit * MOE_UNIT, MOE_UNIT), MOE_UNIT)


def _one_hot_rows(row_id, targets, axis):
    p = jnp.zeros(row_id.shape, F32)
    for k in range(TOP_K):
        tgt = targets[k:k + 1, :] if axis == 0 else targets[:, k:k + 1]
        p = jnp.where(row_id == tgt, 1.0, p)
    return p.astype(BF16)


def _dispatch_kernel(ud_ref, used_ref, pst_ref, pnn_ref, u_ref, g_ref, lv_ref, tri_ref,
                     xs_ref, lkt_ref, sorted_ref, xaug_ref, zero_ref, sem):
    j = pl.program_id(0)
    g = g_ref[...]
    sel = g > 0.0
    rank = _dot(jnp.where(sel, 1.0, 0.0).astype(BF16), tri_ref[...])
    loff = jnp.concatenate([lv_ref[0]] * (MOE_TILE // LANES), axis=1)
    rem = jnp.where(sel, loff + rank, MOE_UNSET)
    rows = []
    for _ in range(TOP_K):
        cur = jnp.min(rem, axis=0, keepdims=True)
        rows.append(cur)
        rem = jnp.where(rem == cur, MOE_UNSET, rem)
    lk = jnp.concatenate(rows, axis=0)
    lkt_ref[0] = jnp.concatenate([lk, jnp.full((LANES - TOP_K, MOE_TILE), MOE_UNSET, F32)], axis=0).T

    gt = jnp.concatenate([g, jnp.zeros_like(g)], axis=0).T
    hi = gt.astype(BF16)
    r1 = gt - hi.astype(F32)
    mid = r1.astype(BF16)
    xaug_ref[:, :D_MODEL] = u_ref[...]
    xaug_ref[:, D_MODEL:D_MODEL + LANES] = hi
    xaug_ref[:, D_MODEL + LANES:D_MODEL + 2 * LANES] = mid
    xaug_ref[:, D_MODEL + 2 * LANES:] = (r1 - mid.astype(F32)).astype(BF16)

    used_units = used_ref[j]
    upc = MOE_RCH // MOE_UNIT

    def unit_copy(src_unit, dst_unit):
        return pltpu.make_async_copy(sorted_ref.at[_unit_rows(src_unit)], xs_ref.at[_unit_rows(dst_unit)], sem)

    def sort_chunk(rc, carry):
        r0 = pl.multiple_of(rc * MOE_RCH, MOE_RCH)
        row_id = (lax.broadcasted_iota(I32, (MOE_RCH, MOE_TILE), 0) + r0).astype(F32)
        sorted_ref[pl.ds(r0, MOE_RCH), :] = _dot(_one_hot_rows(row_id, lk, 0), xaug_ref[...]).astype(BF16)

        def send_unit(uu, c):
            unit = rc * upc + uu
            unit_copy(unit, ud_ref[j * (MOE_RCAP // MOE_UNIT) + unit]).start()
            return c

        lax.fori_loop(0, jnp.clip(used_units - rc * upc, 0, upc), send_unit, 0)
        return carry

    lax.fori_loop(0, (used_units + upc - 1) // upc, sort_chunk, 0)

    def drain_unit(uu, c):
        unit_copy(0, 0).wait()
        return c

    lax.fori_loop(0, used_units, drain_unit, 0)

    @pl.when(j == pl.num_programs(0) - 1)
    def _():
        zero_ref[...] = jnp.zeros(zero_ref.shape, BF16)

        def pad_expert(e, carry):
            def pad_copy(uu):
                return pltpu.make_async_copy(zero_ref, xs_ref.at[_unit_rows(pst_ref[e] + uu)], sem)

            def start(uu, c):
                pad_copy(uu).start()
                return c

            def wait(uu, c):
                pad_copy(uu).wait()
                return c

            lax.fori_loop(0, pnn_ref[e], start, 0)
            lax.fori_loop(0, pnn_ref[e], wait, 0)
            return carry

        lax.fori_loop(0, N_EXPERTS, pad_expert, 0)


def _dispatch(u2, gates_t, plan):
    t, d = u2.shape
    n_tiles = t // MOE_TILE
    xw = d + MOE_GW
    tri = jnp.asarray(np.triu(np.ones((MOE_TILE, MOE_TILE), np.float32), k=1), BF16)
    grid_spec = pltpu.PrefetchScalarGridSpec(
        num_scalar_prefetch=4,
        grid=(n_tiles,),
        in_specs=[pl.BlockSpec((MOE_TILE, d), lambda j, *_: (j, 0)),
                  pl.BlockSpec((N_EXPERTS, MOE_TILE), lambda j, *_: (0, j)),
                  pl.BlockSpec((1, N_EXPERTS, LANES), lambda j, *_: (j, 0, 0)),
                  pl.BlockSpec((MOE_TILE, MOE_TILE), lambda j, *_: (0, 0))],
        out_specs=[pl.BlockSpec(memory_space=pl.ANY),
                   pl.BlockSpec((1, MOE_TILE, LANES), lambda j, *_: (j, 0, 0))],
        scratch_shapes=[pltpu.VMEM((MOE_RCAP, xw), BF16), pltpu.VMEM((MOE_TILE, xw), BF16),
                        pltpu.VMEM((MOE_UNIT, xw), BF16), pltpu.SemaphoreType.DMA(())])
    return pl.pallas_call(
        _dispatch_kernel,
        grid_spec=grid_spec,
        out_shape=[jax.ShapeDtypeStruct((plan["n_blocks"] * MOE_BM, xw), BF16),
                   jax.ShapeDtypeStruct((n_tiles, MOE_TILE, LANES), F32)],
        compiler_params=_cparams(("arbitrary",), 56),
        name="dispatch",
    )(plan["unit_dst"], plan["used"], plan["padstart"], plan["padn"], u2, gates_t, plan["loff_v"], tri)


def _experts_kernel(be_ref, sb_ref, br_ref, nbu_ref, x_ref, wg_ref, wu_ref, wd_ref, y_ref, wgb_ref, wub_ref, wdb_ref):
    b = pl.program_id(0)

    @pl.when(b < nbu_ref[0])
    def _():
        e = be_ref[b]

        @pl.when(jnp.logical_or(b == 0, e != be_ref[jnp.maximum(b - 1, 0)]))
        def _():
            wgb_ref[...] = wg_ref[0].astype(BF16)
            wub_ref[...] = wu_ref[0].astype(BF16)
            wdb_ref[...] = wd_ref[0].astype(BF16)

        for rows in range(MOE_SUB, MOE_BM + 1, MOE_SUB):
            @pl.when(br_ref[b] == rows)
            def _(rows=rows):
                x = x_ref[:rows, :D_MODEL]
                g3 = ((x_ref[:rows, D_MODEL:D_MODEL + LANES].astype(F32)
                       + x_ref[:rows, D_MODEL + LANES:D_MODEL + 2 * LANES].astype(F32))
                      + x_ref[:rows, D_MODEL + 2 * LANES:].astype(F32))
                lane = lax.broadcasted_iota(I32, g3.shape, 1)
                gate = jnp.sum(jnp.where(lane == e, g3, 0.0), axis=1, keepdims=True)
                hid = (_silu(_dot(x, wgb_ref[...])) * _dot(x, wub_ref[...]) * gate).astype(BF16)
                y_ref[:rows, :] = _dot(hid, wdb_ref[...]).astype(y_ref.dtype)


def _experts(xs, plan, wg, wu, wd):
    ns, xw = xs.shape
    _, d, f = wg.shape
    step = lambda b, nbu: jnp.minimum(b, nbu[0] - 1)
    rows_map = lambda b, be, sb, br, nbu: (sb[step(b, nbu)], 0)
    w_map = lambda b, be, sb, br, nbu: (be[step(b, nbu)], 0, 0)
    grid_spec = pltpu.PrefetchScalarGridSpec(
        num_scalar_prefetch=4,
        grid=(ns // MOE_BM,),
        in_specs=[pl.BlockSpec((MOE_BM, xw), rows_map),
                  pl.BlockSpec((1, d, f), w_map), pl.BlockSpec((1, d, f), w_map), pl.BlockSpec((1, f, d), w_map)],
        out_specs=pl.BlockSpec((MOE_BM, d), rows_map),
        scratch_shapes=[pltpu.VMEM((d, f), BF16), pltpu.VMEM((d, f), BF16), pltpu.VMEM((f, d), BF16)])
    return pl.pallas_call(
        _experts_kernel,
        grid_spec=grid_spec,
        out_shape=jax.ShapeDtypeStruct((ns, d), BF16),
        compiler_params=_cparams(("arbitrary",), 52),
        name="experts",
    )(plan["step_e"], plan["step_blk"], plan["step_rows"], plan["nb_used"], xs, wg, wu, wd)


def _final_kernel(ud_ref, used_ref, lkt_ref, u_ref, wg_ref, wu_ref, wd_ref, x1_ref, g2_ref, lg_ref, lb_ref,
                  ys_ref, o_ref, ybuf_ref, sems):
    j = pl.program_id(0)
    used_units = used_ref[j]
    upc = MOE_RCH // MOE_UNIT

    @pl.when(j == 0)
    def _():
        ybuf_ref[...] = jnp.zeros(ybuf_ref.shape, ybuf_ref.dtype)

    def unit_copy(src_unit, dst_unit, chunk):
        return pltpu.make_async_copy(ys_ref.at[_unit_rows(src_unit)], ybuf_ref.at[_unit_rows(dst_unit)],
                                     sems.at[chunk])

    def fetch_unit(unit, c):
        unit_copy(ud_ref[j * (MOE_RCAP // MOE_UNIT) + unit], unit, unit // upc).start()
        return c

    lax.fori_loop(0, used_units, fetch_unit, 0)

    u = u_ref[...]
    hid = (_silu(_dot(u, wg_ref[...])) * _dot(u, wu_ref[...])).astype(BF16)
    o_ref[...] = _dot(hid, wd_ref[...])

    lkt = lkt_ref[0]

    def combine_chunk(rc, carry):
        def wait_unit(uu, c):
            unit_copy(0, 0, rc).wait()
            return c

        lax.fori_loop(0, jnp.clip(used_units - rc * upc, 0, upc), wait_unit, 0)
        r0 = pl.multiple_of(rc * MOE_RCH, MOE_RCH)
        row_id = (lax.broadcasted_iota(I32, (MOE_TILE, MOE_RCH), 1) + r0).astype(F32)
        o_ref[...] += _dot(_one_hot_rows(row_id, lkt, 1), ybuf_ref[pl.ds(r0, MOE_RCH), :])
        return carry

    lax.fori_loop(0, (used_units + upc - 1) // upc, combine_chunk, 0)
    h = ALPHA * x1_ref[...] + g2_ref[0] * o_ref[...]
    o_ref[...] = _layer_norm_rows(h, lg_ref[...], lb_ref[...])


def _final(u2, wg_s, wu_s, wd_s, ys, lkt, plan, x1, gate2, ln_g, ln_b, seq):
    t, d = x1.shape
    tm = MOE_TILE
    tpb = seq // tm
    f = wg_s.shape[1]
    once = pl.Buffered(1)
    rows = lambda: pl.BlockSpec((tm, d), lambda j, *_: (j, 0))
    full = lambda shape: pl.BlockSpec(shape, lambda j, *_: (0,) * len(shape), pipeline_mode=once)
    grid_spec = pltpu.PrefetchScalarGridSpec(
        num_scalar_prefetch=2,
        grid=(t // tm,),
        in_specs=[pl.BlockSpec((1, tm, LANES), lambda j, *_: (j, 0, 0)), rows(),
                  full((d, f)), full((d, f)), full((f, d)), rows(),
                  pl.BlockSpec((1, 1, d), lambda j, *_: (j // tpb, 0, 0)), full((1, d)), full((1, d)),
                  pl.BlockSpec(memory_space=pl.ANY)],
        out_specs=rows(),
        scratch_shapes=[pltpu.VMEM((MOE_RCAP, d), BF16), pltpu.SemaphoreType.DMA((MOE_RCAP // MOE_RCH,))])
    return pl.pallas_call(
        _final_kernel,
        grid_spec=grid_spec,
        out_shape=jax.ShapeDtypeStruct((t, d), F32),
        compiler_params=_cparams(("arbitrary",), 58),
        name="final",
    )(plan["unit_dst"], plan["used"], lkt, u2, wg_s, wu_s, wd_s, x1, gate2, ln_g[None, :], ln_b[None, :], ys)


def _split_hi_lo(w):
    hi = w.astype(BF16)
    lo = (w - hi.astype(F32)).astype(BF16)
    return hi, lo


def _rope_tables(positions, dim, reps):
    inv_freq = 1.0 / (ROPE_THETA ** (jnp.arange(0, dim, 2, dtype=F32) / dim))
    ang = positions.astype(F32).reshape(-1)[:, None] * inv_freq
    cos, sin = jnp.cos(ang), jnp.sin(ang)
    return (jnp.tile(jnp.concatenate([cos, cos], axis=1), (1, reps)),
            jnp.tile(jnp.concatenate([-sin, sin], axis=1), (1, reps)))


def _layer(x2, c, positions, bsz, seq, w_ada, b_ada, w_in, conv_w, conv_b, dt_bias, a_log, d_skip, ssd_norm_w,
           idx_k_ln_g, idx_k_ln_b, w_out, ln1_g, ln1_b, w_router, router_bias, w_gate_e, w_up_e, w_down_e,
           w_gate_s, w_up_s, w_down_s, ln2_g, ln2_b, tables):
    d = D_MODEL
    mod = _ada(c, w_ada, b_ada).reshape(bsz, 6, 1, d)
    shift1, scale1, gate1, shift2, scale2, gate2 = [mod[:, k] for k in range(6)]

    o_z, o_xbc, o_dt, o_q = 0, D_SSM, 2 * D_SSM + 2 * SSM_GROUPS * SSM_STATE, 2 * D_SSM + 2 * SSM_GROUPS * SSM_STATE + SSM_HEADS
    o_k, o_v, o_qi = o_q + D_ATTN, o_q + 2 * D_ATTN, o_q + 3 * D_ATTN
    o_ki = o_qi + IDX_HEADS * IDX_DIM
    o_wi = o_ki + IDX_DIM
    w_main = jnp.concatenate([w_in[:, o_z:o_z + D_SSM], w_in[:, o_xbc:o_xbc + D_SSM], w_in[:, o_q:o_ki],
                              w_in[:, o_xbc + D_SSM:o_dt]], axis=1).astype(BF16)
    padc = lambda w: jnp.pad(w, ((0, 0), (0, LANES - w.shape[1])))
    w_tail = jnp.concatenate([padc(w_in[:, o_dt:o_q]), padc(w_in[:, o_ki:o_wi]), padc(w_in[:, o_wi:])], axis=1)
    wt_hi, wt_lo = _split_hi_lo(w_tail)

    proj, tail = _inproj(x2, scale1, shift1, w_main, wt_hi, wt_lo, seq)
    y_ssm = _ssm(proj, tail, conv_w, conv_b, dt_bias, a_log, d_skip, ssd_norm_w, bsz, seq)
    q_r, k_r, v_t, qi_r, ki0, ki1, w_s = _prep(proj, tail, *tables, idx_k_ln_g, idx_k_ln_b)
    y_att = _attn(q_r, k_r, v_t, qi_r, ki0, ki1, w_s, bsz, seq)

    wo = w_out.astype(BF16)
    wr_hi, wr_lo = _split_hi_lo(w_router.T)
    rbias = jnp.broadcast_to(router_bias[:, None], (N_EXPERTS, LANES))
    x1, u2, gates_t = _outproj(y_ssm, y_att, wo[:D_SSM], wo[D_SSM:], x2, gate1, scale2, shift2, ln1_g, ln1_b,
                               wr_hi, wr_lo, rbias, seq)
    plan = _moe_plan(gates_t, x2.shape[0] // MOE_TILE)
    xs, lkt = _dispatch(u2, gates_t, plan)
    ys = _experts(xs, plan, w_gate_e, w_up_e, w_down_e)
    return _final(u2, w_gate_s.astype(BF16), w_up_s.astype(BF16), w_down_s.astype(BF16), ys, lkt, plan, x1, gate2,
                  ln2_g, ln2_b, seq)


def kernel(x, c, positions, w_ada, b_ada, w_in, conv_w, conv_b, dt_bias, a_log, d_skip, ssd_norm_w, idx_k_ln_g, idx_k_ln_b, w_out, ln1_g, ln1_b, w_router, router_bias, w_gate_e, w_up_e, w_down_e, w_gate_s, w_up_s, w_down_s, ln2_g, ln2_b):
    bsz, seq, d = x.shape
    tables = _rope_tables(positions, ATTN_HEAD_DIM, 1) + _rope_tables(positions, IDX_DIM, 2)
    x2 = x.reshape(bsz * seq, d)
    for l in range(w_ada.shape[0]):
        x2 = _layer(x2, c, positions, bsz, seq, w_ada[l], b_ada[l], w_in[l], conv_w[l], conv_b[l], dt_bias[l],
                    a_log[l], d_skip[l], ssd_norm_w[l], idx_k_ln_g[l], idx_k_ln_b[l], w_out[l], ln1_g[l], ln1_b[l],
                    w_router[l], router_bias[l], w_gate_e[l], w_up_e[l], w_down_e[l], w_gate_s[l], w_up_s[l],
                    w_down_s[l], ln2_g[l], ln2_b[l], tables)
    return x2.reshape(bsz, seq, d)
```

```python
import functools
import math

import jax
import jax.numpy as jnp
import numpy as np
from jax import lax
from jax.experimental import pallas as pl
from jax.experimental.pallas import tpu as pltpu

F32 = jnp.float32
BF16 = jnp.bfloat16
I32 = jnp.int32
HIGHEST = lax.Precision.HIGHEST

D_MODEL = 2048
D_SSM = 1024
D_ATTN = 1024
SSM_HEAD_DIM = 64
SSM_HEADS = 16
SSM_GROUPS = 2
SSM_STATE = 128
CONV_WIDTH = 4
SSD_CHUNK = 128
ATTN_HEAD_DIM = 128
ATTN_HEADS = 8
IDX_HEADS = 16
IDX_DIM = 64
INDEX_TOPK = 256
ROPE_THETA = 10000.0
N_EXPERTS = 64
N_EXPERT_GROUPS = 8
EXPERTS_PER_GROUP = 8
TOPK_GROUPS = 4
TOP_K = 8
D_EXPERT = 512
D_SHARED = 512
ROUTED_SCALE = 2.5
DEPTH = 1
ALPHA = (2.0 * DEPTH) ** 0.25
LN_EPS = 1e-5

LANES = 128
SUBLANES = 8
VMEM_BYTES_V7X = 64 * 1024 * 1024
INT_MIN = -(2 ** 31)
KEY_LOWEST_FINITE = INT_MIN + 0x00800000

COL_Z, COL_XS, COL_Q, COL_K, COL_V, COL_QI, COL_BC = 0, 1024, 2048, 3072, 4096, 5120, 6144
N_MAIN = 6656
N_TAIL = 3 * LANES


def _cparams(sem, vmem_mb):
    return pltpu.CompilerParams(dimension_semantics=sem, vmem_limit_bytes=vmem_mb * 1024 * 1024)


def _silu(v):
    return v * (1.0 / (1.0 + jnp.exp(-v)))


def _dot(a, b, precision=None):
    return jnp.dot(a, b, preferred_element_type=F32, precision=precision)


def _dot_nt(a, b, precision=None):
    return lax.dot_general(a, b, (((1,), (1,)), ((), ())), preferred_element_type=F32, precision=precision)


def _split3(x):
    hi = x.astype(BF16)
    r1 = x - hi.astype(F32)
    mid = r1.astype(BF16)
    return hi, mid, (r1 - mid.astype(F32)).astype(BF16)


def _layer_norm_rows(h, g, b):
    mu = jnp.mean(h, axis=-1, keepdims=True)
    d = h - mu
    var = jnp.mean(d * d, axis=-1, keepdims=True)
    return d * lax.rsqrt(var + LN_EPS) * g + b


def _ada_kernel(cb_ref, w_ref, b_ref, o_ref):
    tn = w_ref.shape[1]
    for bi in range(cb_ref.shape[0]):
        cb = _silu(cb_ref[bi])
        cols = [jnp.sum(w_ref[:, j * LANES:(j + 1) * LANES] * cb, axis=0, keepdims=True)
                for j in range(tn // LANES)]
        o_ref[bi:bi + 1, :] = jnp.concatenate(cols, axis=1) + b_ref[...]


def _ada(c, w_ada, b_ada):
    bsz, d = c.shape
    n = w_ada.shape[1]
    tn = 1024
    cb = jnp.broadcast_to(c[:, :, None], (bsz, d, LANES))
    return pl.pallas_call(
        _ada_kernel,
        grid=(n // tn,),
        in_specs=[pl.BlockSpec((bsz, d, LANES), lambda j: (0, 0, 0)),
                  pl.BlockSpec((d, tn), lambda j: (0, j)),
                  pl.BlockSpec((1, tn), lambda j: (0, j))],
        out_specs=pl.BlockSpec((bsz, tn), lambda j: (0, j)),
        out_shape=jax.ShapeDtypeStruct((bsz, n), F32),
        compiler_params=_cparams(("parallel",), 40),
        name="ada",
    )(cb, w_ada, b_ada.reshape(1, n))


def _inproj_kernel(x_ref, sc_ref, sh_ref, w_ref, wth_ref, wtl_ref, o_ref, t_ref, u_ref):
    @pl.when(pl.program_id(1) == 0)
    def _():
        u = x_ref[...] * (1.0 + sc_ref[0]) + sh_ref[0]
        uh = u.astype(BF16)
        ul = (u - uh.astype(F32)).astype(BF16)
        u_ref[...] = uh
        t_ref[...] = (_dot_nt(uh, wth_ref[...]) + _dot_nt(uh, wtl_ref[...]) + _dot_nt(ul, wth_ref[...]))

    o_ref[...] = _dot_nt(u_ref[...], w_ref[...])


def _inproj(x2, scale1, shift1, w_main, wt_hi, wt_lo, seq):
    t, d = x2.shape
    tm, tn = 1024, 512
    tpb = seq // tm
    return pl.pallas_call(
        _inproj_kernel,
        grid=(t // tm, N_MAIN // tn),
        in_specs=[pl.BlockSpec((tm, d), lambda i, j: (i, 0)),
                  pl.BlockSpec((1, 1, d), lambda i, j: (i // tpb, 0, 0)),
                  pl.BlockSpec((1, 1, d), lambda i, j: (i // tpb, 0, 0)),
                  pl.BlockSpec((tn, d), lambda i, j: (j, 0)),
                  pl.BlockSpec((N_TAIL, d), lambda i, j: (0, 0)),
                  pl.BlockSpec((N_TAIL, d), lambda i, j: (0, 0))],
        out_specs=[pl.BlockSpec((tm, tn), lambda i, j: (i, j)),
                   pl.BlockSpec((tm, N_TAIL), lambda i, j: (i, 0))],
        out_shape=[jax.ShapeDtypeStruct((t, N_MAIN), F32),
                   jax.ShapeDtypeStruct((t, N_TAIL), F32)],
        scratch_shapes=[pltpu.VMEM((tm, d), BF16)],
        compiler_params=_cparams(("parallel", "arbitrary"), 48),
        name="inproj",
    )(x2, scale1, shift1, w_main, wt_hi, wt_lo)


def _ssm_kernel(z_ref, xs_ref, bc_ref, dt_ref, cwx_ref, cbx_ref, cwb_ref, cbb_ref, dtb_ref, alog_ref,
                dsk_ref, nw_ref, e_ref, e2_ref, tril_ref, o_ref, px_ref, pb_ref, st_ref):
    q = SSD_CHUNK
    hpg = SSM_HEADS // SSM_GROUPS
    gw = hpg * SSM_HEAD_DIM

    @pl.when(pl.program_id(1) == 0)
    def _():
        px_ref[0:SUBLANES, :] = jnp.zeros((SUBLANES, px_ref.shape[1]), F32)
        pb_ref[0:SUBLANES, :] = jnp.zeros((SUBLANES, pb_ref.shape[1]), F32)
        st_ref[...] = jnp.zeros(st_ref.shape, F32)

    def conv_silu(raw_ref, pad_ref, w_ref, b_ref):
        pad_ref[SUBLANES:SUBLANES + q, :] = raw_ref[...]
        acc = b_ref[...] + w_ref[0:1, :] * pad_ref[SUBLANES - 3:SUBLANES - 3 + q, :]
        for k in range(1, CONV_WIDTH):
            acc = acc + w_ref[k:k + 1, :] * pad_ref[SUBLANES - 3 + k:SUBLANES - 3 + k + q, :]
        pad_ref[0:SUBLANES, :] = raw_ref[q - SUBLANES:q, :]
        return _silu(acc)

    xs = conv_silu(xs_ref, px_ref, cwx_ref, cbx_ref)
    bc = conv_silu(bc_ref, pb_ref, cwb_ref, cbb_ref)

    dtr = dt_ref[...] + dtb_ref[...]
    dt = jnp.maximum(dtr, 0.0) + jnp.log(1.0 + jnp.exp(-jnp.abs(dtr)))
    log_a = dt * (-jnp.exp(alog_ref[...]))
    cs = sum(_dot(tril_ref[...], part) for part in _split3(log_a))
    cs_parts = _split3(cs)
    cs_e = sum(_dot(part, e_ref[...]) for part in cs_parts)
    dt_e = sum(_dot(part, e_ref[...]) for part in _split3(dt))
    cs_col = sum(_dot(part, e2_ref[...]) for part in cs_parts)
    cs_t = cs.T
    cs_last = cs_e[q - 1:q, :]

    xdt = xs * dt_e
    rows = lax.broadcasted_iota(I32, (q, q), 0)
    cols = lax.broadcasted_iota(I32, (q, q), 1)
    causal = rows >= cols
    first_half = lax.broadcasted_iota(I32, (q, LANES), 1) < SSM_HEAD_DIM

    y_pairs = []
    for g in range(SSM_GROUPS):
        b_g = bc[:, g * SSM_STATE:(g + 1) * SSM_STATE]
        c_g = bc[:, (SSM_GROUPS + g) * SSM_STATE:(SSM_GROUPS + g + 1) * SSM_STATE]
        cb = _dot_nt(c_g.astype(BF16), b_g.astype(BF16))
        for hp in range(hpg // 2):
            pair = g * (hpg // 2) + hp
            x_pair = xdt[:, pair * LANES:(pair + 1) * LANES].astype(BF16)
            ys = []
            for sub in range(2):
                h = 2 * pair + sub
                seg = cs_col[:, h * LANES:(h + 1) * LANES] - cs_t[h:h + 1, :]
                dec = jnp.exp(jnp.where(causal, seg, -jnp.inf))
                ys.append(_dot((cb * dec).astype(BF16), x_pair))
            y_pairs.append(jnp.where(first_half, ys[0], ys[1]))
    y_diag = jnp.concatenate(y_pairs, axis=1)

    xw = xdt * jnp.exp(cs_last - cs_e)
    y_off, new_states = [], []
    for g in range(SSM_GROUPS):
        b_g = bc[:, g * SSM_STATE:(g + 1) * SSM_STATE]
        c_g = bc[:, (SSM_GROUPS + g) * SSM_STATE:(SSM_GROUPS + g + 1) * SSM_STATE]
        h_in = st_ref[:, g * gw:(g + 1) * gw]
        y_off.append(_dot(c_g.astype(BF16), h_in.astype(BF16)))
        new_states.append(_dot(b_g.T.astype(BF16), xw[:, g * gw:(g + 1) * gw].astype(BF16)))
    y_off = jnp.concatenate(y_off, axis=1) * jnp.exp(cs_e)
    st_ref[...] = jnp.exp(cs_last) * st_ref[...] + jnp.concatenate(new_states, axis=1)

    y = y_diag + y_off + dsk_ref[...] * xs
    yf = y * _silu(z_ref[...])
    ms = jnp.mean(yf * yf, axis=-1, keepdims=True)
    o_ref[...] = (yf * lax.rsqrt(ms + LN_EPS) * nw_ref[...]).astype(o_ref.dtype)


def _ssm(proj, tail, conv_w, conv_b, dt_bias, a_log, d_skip, ssd_norm_w, bsz, seq):
    t = proj.shape[0]
    q = SSD_CHUNK
    n_c = seq // q
    nbc = 2 * SSM_GROUPS * SSM_STATE
    cw_x, cw_b = conv_w[:, :D_SSM], conv_w[:, D_SSM:]
    cb_x, cb_b = conv_b[None, :D_SSM], conv_b[None, D_SSM:]
    pad16 = lambda v: jnp.pad(v, (0, LANES - SSM_HEADS))[None, :]
    head_of_lane = np.arange(D_SSM) // SSM_HEAD_DIM
    e_mat = jnp.asarray((np.arange(LANES)[:, None] == head_of_lane[None, :]).astype(np.float32), BF16)
    e2_mat = jnp.asarray((np.arange(LANES)[:, None] == (np.arange(SSM_HEADS * LANES) // LANES)[None, :])
                         .astype(np.float32), BF16)
    tril = jnp.asarray(np.tril(np.ones((q, q), np.float32)), BF16)
    row = lambda b, c: b * n_c + c
    full = lambda shape: pl.BlockSpec(shape, lambda b, c: (0,) * len(shape))
    return pl.pallas_call(
        _ssm_kernel,
        grid=(bsz, n_c),
        in_specs=[pl.BlockSpec((q, D_SSM), lambda b, c: (row(b, c), COL_Z // D_SSM)),
                  pl.BlockSpec((q, D_SSM), lambda b, c: (row(b, c), COL_XS // D_SSM)),
                  pl.BlockSpec((q, nbc), lambda b, c: (row(b, c), COL_BC // nbc)),
                  pl.BlockSpec((q, LANES), lambda b, c: (row(b, c), 0)),
                  full((CONV_WIDTH, D_SSM)), full((1, D_SSM)), full((CONV_WIDTH, nbc)), full((1, nbc)),
                  full((1, LANES)), full((1, LANES)), full((1, D_SSM)), full((1, D_SSM)),
                  full((LANES, D_SSM)), full((LANES, SSM_HEADS * LANES)), full((q, q))],
        out_specs=pl.BlockSpec((q, D_SSM), lambda b, c: (row(b, c), 0)),
        out_shape=jax.ShapeDtypeStruct((t, D_SSM), BF16),
        scratch_shapes=[pltpu.VMEM((SUBLANES + q, D_SSM), F32),
                        pltpu.VMEM((SUBLANES + q, nbc), F32),
                        pltpu.VMEM((SSM_STATE, D_SSM), F32)],
        compiler_params=_cparams(("parallel", "arbitrary"), 40),
        name="ssm",
    )(proj, proj, proj, tail, cw_x, cb_x, cw_b, cb_b, pad16(dt_bias), pad16(a_log),
      jnp.repeat(d_skip, SSM_HEAD_DIM)[None, :], ssd_norm_w[None, :], e_mat, e2_mat, tril)


def _prep_kernel(q_ref, k_ref, v_ref, qi_ref, ki_ref, wi_ref, ca_ref, sa_ref, ci_ref, si_ref, lg_ref, lb_ref,
                 qo_ref, ko_ref, vo_ref, qio_ref, ki0_ref, ki1_ref, wo_ref):
    ca, sa, ci, si = ca_ref[...], sa_ref[...], ci_ref[...], si_ref[...]
    lane = lax.broadcasted_iota(I32, ca.shape, 1)
    first32 = (lane % IDX_DIM) < (IDX_DIM // 2)
    q_scale = ATTN_HEAD_DIM ** -0.5 * math.log2(math.e)

    def rope_attn(v):
        return v * ca + pltpu.roll(v, ATTN_HEAD_DIM // 2, 1) * sa

    def rope_idx(v):
        rot = jnp.where(first32, pltpu.roll(v, LANES - IDX_DIM // 2, 1), pltpu.roll(v, IDX_DIM // 2, 1))
        return v * ci + rot * si

    for h in range(ATTN_HEADS):
        sl = slice(h * LANES, (h + 1) * LANES)
        qo_ref[:, sl] = (rope_attn(q_ref[:, sl]) * q_scale).astype(BF16)
        ko_ref[:, sl] = rope_attn(k_ref[:, sl]).astype(BF16)
    vo_ref[0] = v_ref[...].T.astype(BF16)
    for p in range(IDX_HEADS * IDX_DIM // LANES):
        sl = slice(p * LANES, (p + 1) * LANES)
        qio_ref[:, sl] = rope_idx(qi_ref[:, sl]).astype(BF16)

    kraw = ki_ref[...]
    valid = lane < IDX_DIM
    mu = jnp.sum(kraw, axis=-1, keepdims=True) * (1.0 / IDX_DIM)
    dk = jnp.where(valid, kraw - mu, 0.0)
    var = jnp.sum(dk * dk, axis=-1, keepdims=True) * (1.0 / IDX_DIM)
    kn = jnp.where(valid, dk * lax.rsqrt(var + LN_EPS) * lg_ref[...] + lb_ref[...], 0.0)
    kr = jnp.where(valid, rope_idx(kn), 0.0)
    ki0_ref[...] = kr.astype(BF16)
    ki1_ref[...] = pltpu.roll(kr, IDX_DIM, 1).astype(BF16)
    wo_ref[...] = wi_ref[...] * (IDX_HEADS ** -0.5 * IDX_DIM ** -0.5)


def _prep(proj, tail, cos_a, sin_a, cos_i, sin_i, ln_g, ln_b):
    t = proj.shape[0]
    tm = ATT_KC
    w = D_ATTN
    pad64 = lambda v: jnp.pad(v, (0, LANES - IDX_DIM))[None, :]
    col = lambda c: pl.BlockSpec((tm, w), lambda i: (i, c // w))
    lane_blk = lambda c: pl.BlockSpec((tm, LANES), lambda i: (i, c))
    full = lambda: pl.BlockSpec((1, LANES), lambda i: (0, 0))
    return pl.pallas_call(
        _prep_kernel,
        grid=(t // tm,),
        in_specs=[col(COL_Q), col(COL_K), col(COL_V), col(COL_QI), lane_blk(1), lane_blk(2),
                  lane_blk(0), lane_blk(0), lane_blk(0), lane_blk(0), full(), full()],
        out_specs=[pl.BlockSpec((tm, w), lambda i: (i, 0))] * 2 + [pl.BlockSpec((1, w, tm), lambda i: (i, 0, 0))]
        + [pl.BlockSpec((tm, w), lambda i: (i, 0))] + [pl.BlockSpec((tm, LANES), lambda i: (i, 0))] * 3,
        out_shape=[jax.ShapeDtypeStruct((t, w), BF16)] * 2 + [jax.ShapeDtypeStruct((t // tm, w, tm), BF16)]
        + [jax.ShapeDtypeStruct((t, w), BF16)]
        + [jax.ShapeDtypeStruct((t, LANES), BF16)] * 2 + [jax.ShapeDtypeStruct((t, LANES), F32)],
        compiler_params=_cparams(("parallel",), 48),
        name="prep",
    )(proj, proj, proj, proj, tail, tail, cos_a, sin_a, cos_i, sin_i, pad64(ln_g), pad64(ln_b))


ATT_QB = 128
ATT_KC = 512
ATT_KS = 256


def _attn_kernel(q_ref, k_ref, vt_ref, qi_ref, ki0_ref, ki1_ref, w_ref, o_ref, sc_ref, acc_ref, *, topk):
    i = pl.program_id(1)
    n_ch = (i * ATT_QB + ATT_QB + ATT_KC - 1) // ATT_KC
    key_off = lax.broadcasted_iota(I32, (ATT_KC, ATT_QB), 0)
    q_pos = i * ATT_QB + lax.broadcasted_iota(I32, (ATT_KC, ATT_QB), 1)
    w_t = w_ref[...].T

    def score_chunk(c, carry):
        start = pl.multiple_of(c * ATT_KC, ATT_KC)
        ki0 = ki0_ref[pl.ds(start, ATT_KC), :]
        ki1 = ki1_ref[pl.ds(start, ATT_KC), :]
        acc = jnp.zeros((ATT_KC, ATT_QB), F32)
        for h in range(IDX_HEADS):
            pair = qi_ref[:, (h // 2) * LANES:(h // 2 + 1) * LANES]
            rel = _dot_nt(ki0 if h % 2 == 0 else ki1, pair)
            acc = acc + jnp.maximum(rel, 0.0) * w_t[h:h + 1, :]
        sc_ref[c] = jnp.where(start + key_off <= q_pos, acc, -jnp.inf)
        return carry

    lax.fori_loop(0, n_ch, score_chunk, 0)

    def key_to_float(key):
        return pltpu.bitcast(jnp.where(key < 0, key ^ jnp.int32(0x7FFFFFFF), key), F32)

    def search_bit(it, ans):
        cand = ans ^ lax.shift_left(jnp.int32(1), jnp.int32(31) - it)
        cand_f = key_to_float(cand)

        def count_chunk(c, cnt):
            part = jnp.where(sc_ref[c] >= cand_f, 1, 0).astype(I32)
            return cnt + jnp.sum(part.reshape(ATT_KC // SUBLANES, SUBLANES, ATT_QB), axis=0)

        cnt = lax.fori_loop(0, n_ch, count_chunk, jnp.zeros((SUBLANES, ATT_QB), I32))
        total = jnp.sum(cnt, axis=0, keepdims=True)
        return jnp.where(total >= topk, cand, ans)

    kth = lax.fori_loop(0, 32, search_bit, jnp.full((1, ATT_QB), INT_MIN, I32))
    lowest_finite = jnp.float32(np.finfo(np.float32).min)
    thr = jnp.where(kth < KEY_LOWEST_FINITE, lowest_finite, key_to_float(kth))

    acc_ref[...] = jnp.zeros(acc_ref.shape, F32)

    def att_chunk(c, carry):
        m_all, l_all = carry
        start = pl.multiple_of(c * ATT_KC, ATT_KC)
        for part in range(ATT_KC // ATT_KS):
            ks = slice(part * ATT_KS, (part + 1) * ATT_KS)
            sel = sc_ref[c, ks, :] >= thr
            m_rows, l_rows = [], []
            for h in range(ATTN_HEADS):
                sl = slice(h * LANES, (h + 1) * LANES)
                s = _dot_nt(k_ref[pl.ds(start + part * ATT_KS, ATT_KS), sl], q_ref[:, sl])
                s = jnp.where(sel, s, -jnp.inf)
                m_old = m_all[h:h + 1, :]
                m_new = jnp.maximum(m_old, jnp.max(s, axis=0, keepdims=True))
                p = jnp.exp2(s - m_new)
                alpha = jnp.exp2(m_old - m_new)
                l_rows.append(alpha * l_all[h:h + 1, :] + jnp.sum(p, axis=0, keepdims=True))
                m_rows.append(m_new)
                acc_ref[h] = alpha * acc_ref[h] + _dot(vt_ref[c, sl, ks], p.astype(BF16))
            m_all = jnp.concatenate(m_rows, axis=0)
            l_all = jnp.concatenate(l_rows, axis=0)
        return m_all, l_all

    init = (jnp.full((ATTN_HEADS, ATT_QB), -1e30, F32), jnp.zeros((ATTN_HEADS, ATT_QB), F32))
    _, l_all = lax.fori_loop(0, n_ch, att_chunk, init)
    for h in range(ATTN_HEADS):
        o_ref[:, h * LANES:(h + 1) * LANES] = (acc_ref[h] / l_all[h:h + 1, :]).T.astype(o_ref.dtype)


def _attn(q_r, k_r, v_t, qi_r, ki0, ki1, w_s, bsz, seq):
    t = q_r.shape[0]
    nq = seq // ATT_QB
    n_kc = seq // ATT_KC
    topk = min(INDEX_TOPK, seq // 4)
    once = pl.Buffered(1)
    qblk = lambda width: pl.BlockSpec((ATT_QB, width), lambda b, i: (b * nq + i, 0))
    per_batch = lambda width: pl.BlockSpec((seq, width), lambda b, i: (b, 0), pipeline_mode=once)
    return pl.pallas_call(
        functools.partial(_attn_kernel, topk=topk),
        grid=(bsz, nq),
        in_specs=[qblk(D_ATTN), per_batch(D_ATTN),
                  pl.BlockSpec((n_kc, D_ATTN, ATT_KC), lambda b, i: (b, 0, 0), pipeline_mode=once),
                  qblk(IDX_HEADS * IDX_DIM), per_batch(LANES), per_batch(LANES), qblk(LANES)],
        out_specs=qblk(D_ATTN),
        out_shape=jax.ShapeDtypeStruct((t, D_ATTN), BF16),
        scratch_shapes=[pltpu.VMEM((n_kc, ATT_KC, ATT_QB), F32),
                        pltpu.VMEM((ATTN_HEADS, ATTN_HEAD_DIM, ATT_QB), F32)],
        compiler_params=_cparams(("parallel", "arbitrary"), 48),
        name="attn",
    )(q_r, k_r, v_t, qi_r, ki0, ki1, w_s)


def _first_index_of_max(v, iota, n):
    m = jnp.max(v, axis=0, keepdims=True)
    idx = jnp.min(jnp.where(v == m, iota, n), axis=0, keepdims=True)
    return m, idx


def _route(scores, biased):
    tm = scores.shape[1]
    neg_inf = jnp.float32(-jnp.inf)
    iota8 = lax.broadcasted_iota(I32, (EXPERTS_PER_GROUP, tm), 0)
    group_scores = []
    for g in range(N_EXPERT_GROUPS):
        v = biased[g * EXPERTS_PER_GROUP:(g + 1) * EXPERTS_PER_GROUP, :]
        m1, i1 = _first_index_of_max(v, iota8, EXPERTS_PER_GROUP)
        m2 = jnp.max(jnp.where(iota8 == i1, neg_inf, v), axis=0, keepdims=True)
        group_scores.append(m1 + m2)
    gs = jnp.concatenate(group_scores, axis=0)
    keep = jnp.zeros(gs.shape, jnp.bool_)
    for _ in range(TOPK_GROUPS):
        _, gi = _first_index_of_max(gs, iota8, N_EXPERT_GROUPS)
        hit = iota8 == gi
        keep = jnp.logical_or(keep, hit)
        gs = jnp.where(hit, neg_inf, gs)
    keep_f = jnp.where(keep, 1.0, 0.0)
    masked = jnp.concatenate(
        [jnp.where(keep_f[g:g + 1, :] > 0.0, biased[g * EXPERTS_PER_GROUP:(g + 1) * EXPERTS_PER_GROUP, :], neg_inf)
         for g in range(N_EXPERT_GROUPS)], axis=0)
    iota64 = lax.broadcasted_iota(I32, (N_EXPERTS, tm), 0)
    sel_w = jnp.zeros((N_EXPERTS, tm), F32)
    for _ in range(TOP_K):
        _, ei = _first_index_of_max(masked, iota64, N_EXPERTS)
        hit = iota64 == ei
        sel_w = jnp.where(hit, scores, sel_w)
        masked = jnp.where(hit, neg_inf, masked)
    denom = jnp.sum(sel_w, axis=0, keepdims=True)
    return sel_w / denom * ROUTED_SCALE


def _outproj_kernel(ys_ref, ya_ref, wa_ref, wb_ref, x_ref, g1_ref, sc2_ref, sh2_ref, lg_ref, lb_ref,
                    wrh_ref, wrl_ref, rb_ref, x1_ref, u2_ref, gt_ref):
    mix = _dot(ys_ref[...], wa_ref[...]) + _dot(ya_ref[...], wb_ref[...])
    h = ALPHA * x_ref[...] + g1_ref[0] * mix
    x1 = _layer_norm_rows(h, lg_ref[...], lb_ref[...])
    x1_ref[...] = x1
    u2 = x1 * (1.0 + sc2_ref[0]) + sh2_ref[0]
    uh = u2.astype(BF16)
    ul = (u2 - uh.astype(F32)).astype(BF16)
    u2_ref[...] = uh
    wrh = wrh_ref[...]
    logits = _dot_nt(wrh, uh) + _dot_nt(wrh, ul) + _dot_nt(wrl_ref[...], uh)
    scores = 1.0 / (1.0 + jnp.exp(-logits))
    gt_ref[...] = _route(scores, scores + rb_ref[:, 0:1])


def _outproj(y_ssm, y_att, wo_a, wo_b, x2, gate1, scale2, shift2, ln_g, ln_b, wr_hi, wr_lo, rbias, seq):
    t, d = x2.shape
    tm = 512
    tpb = seq // tm
    half = y_ssm.shape[1]
    rows = lambda width: pl.BlockSpec((tm, width), lambda i: (i, 0))
    full = lambda shape: pl.BlockSpec(shape, lambda i: (0,) * len(shape))
    mod = lambda: pl.BlockSpec((1, 1, d), lambda i: (i // tpb, 0, 0))
    return pl.pallas_call(
        _outproj_kernel,
        grid=(t // tm,),
        in_specs=[rows(half), rows(half), full((half, d)), full((half, d)), rows(d), mod(), mod(), mod(),
                  full((1, d)), full((1, d)), full((N_EXPERTS, d)), full((N_EXPERTS, d)), full((N_EXPERTS, LANES))],
        out_specs=[rows(d), rows(d), pl.BlockSpec((N_EXPERTS, tm), lambda i: (0, i))],
        out_shape=[jax.ShapeDtypeStruct((t, d), F32), jax.ShapeDtypeStruct((t, d), BF16),
                   jax.ShapeDtypeStruct((N_EXPERTS, t), F32)],
        compiler_params=_cparams(("parallel",), 48),
        name="outproj",
    )(y_ssm, y_att, wo_a, wo_b, x2, gate1, scale2, shift2, ln_g[None, :], ln_b[None, :], wr_hi, wr_lo, rbias)


MOE_TILE = 256
MOE_UNIT = 16
MOE_BM = 512
MOE_SUB = 128
MOE_RCAP = TOP_K * MOE_TILE + N_EXPERTS * MOE_UNIT
MOE_RCH = 512
MOE_GW = 3 * LANES
MOE_UNSET = 1e9


def _moe_plan(gates_t, n_tiles):
    upb = MOE_BM // MOE_UNIT
    ups = MOE_SUB // MOE_UNIT
    cnt = jnp.sum((gates_t > 0.0).reshape(N_EXPERTS, n_tiles, MOE_TILE), axis=2, dtype=I32).T
    nun = (cnt + MOE_UNIT - 1) // MOE_UNIT
    lend = jnp.cumsum(nun, axis=1)
    loff = lend - nun
    tot = jnp.sum(nun, axis=0)
    tot_sub = (tot + ups - 1) // ups * ups
    nb = (tot + upb - 1) // upb
    cum_nb = jnp.cumsum(nb)
    first_blk = cum_nb - nb
    base = first_blk * upb
    goff = base[None, :] + jnp.cumsum(nun, axis=0) - nun
    unit = jnp.arange(MOE_RCAP // MOE_UNIT, dtype=I32)
    shift = goff - loff
    step_up = jnp.concatenate([shift[:, :1], shift[:, 1:] - shift[:, :-1]], axis=1)
    started = jnp.concatenate([jnp.ones((n_tiles, unit.shape[0], 1), jnp.bool_),
                               lend[:, None, :-1] <= unit[None, :, None]], axis=2)
    unit_dst = unit[None, :] + jnp.sum(jnp.where(started, step_up[:, None, :], 0), axis=2)
    n_blocks = (TOP_K * n_tiles * MOE_TILE // MOE_UNIT + n_tiles * N_EXPERTS + N_EXPERTS * (upb - 1) + upb - 1) // upb
    step = jnp.arange(n_blocks, dtype=I32)
    step_e = jnp.minimum(jnp.sum((cum_nb[None, :] <= step[:, None]).astype(I32), axis=1), N_EXPERTS - 1)
    of_step = step_e[:, None] == jnp.arange(N_EXPERTS, dtype=I32)[None, :]
    pick = lambda per_expert: jnp.sum(jnp.where(of_step, per_expert[None, :], 0), axis=1)
    nb_e = jnp.maximum(pick(nb), 1)
    first_e = pick(first_blk)
    local = (step - first_e + nb_e - 1) % nb_e
    step_rows = jnp.clip(pick(tot_sub) - local * upb, 0, upb) * MOE_UNIT
    loff_v = jnp.broadcast_to((loff * MOE_UNIT).astype(F32)[:, :, None], (n_tiles, N_EXPERTS, LANES))
    flat = lambda v: v.reshape(-1).astype(I32)
    return dict(unit_dst=flat(unit_dst), used=flat(lend[:, -1]), padstart=flat(base + tot), padn=flat(tot_sub - tot),
                step_e=flat(step_e), step_blk=flat(first_e + local), step_rows=flat(step_rows),
                nb_used=cum_nb[-1:].astype(I32), loff_v=loff_v, n_blocks=n_blocks)


def _unit_rows(unit):
    return pl.ds(pl.multiple_of(unit * MOE_UNIT, MOE_UNIT), MOE_UNIT)


def _one_hot_rows(row_id, targets, axis):
    p = jnp.zeros(row_id.shape, F32)
    for k in range(TOP_K):
        tgt = targets[k:k + 1, :] if axis == 0 else targets[:, k:k + 1]
        p = jnp.where(row_id == tgt, 1.0, p)
    return p.astype(BF16)


def _dispatch_kernel(ud_ref, used_ref, pst_ref, pnn_ref, u_ref, g_ref, lv_ref, tri_ref,
                     xs_ref, lkt_ref, sorted_ref, xaug_ref, zero_ref, sem):
    j = pl.program_id(0)
    g = g_ref[...]
    sel = g > 0.0
    rank = _dot(jnp.where(sel, 1.0, 0.0).astype(BF16), tri_ref[...])
    loff = jnp.concatenate([lv_ref[0]] * (MOE_TILE // LANES), axis=1)
    rem = jnp.where(sel, loff + rank, MOE_UNSET)
    rows = []
    for _ in range(TOP_K):
        cur = jnp.min(rem, axis=0, keepdims=True)
        rows.append(cur)
        rem = jnp.where(rem == cur, MOE_UNSET, rem)
    lk = jnp.concatenate(rows, axis=0)
    lkt_ref[0] = jnp.concatenate([lk, jnp.full((LANES - TOP_K, MOE_TILE), MOE_UNSET, F32)], axis=0).T

    gt = jnp.concatenate([g, jnp.zeros_like(g)], axis=0).T
    hi = gt.astype(BF16)
    r1 = gt - hi.astype(F32)
    mid = r1.astype(BF16)
    xaug_ref[:, :D_MODEL] = u_ref[...]
    xaug_ref[:, D_MODEL:D_MODEL + LANES] = hi
    xaug_ref[:, D_MODEL + LANES:D_MODEL + 2 * LANES] = mid
    xaug_ref[:, D_MODEL + 2 * LANES:] = (r1 - mid.astype(F32)).astype(BF16)

    used_units = used_ref[j]
    upc = MOE_RCH // MOE_UNIT

    def unit_copy(src_unit, dst_unit):
        return pltpu.make_async_copy(sorted_ref.at[_unit_rows(src_unit)], xs_ref.at[_unit_rows(dst_unit)], sem)

    def sort_chunk(rc, carry):
        r0 = pl.multiple_of(rc * MOE_RCH, MOE_RCH)
        row_id = (lax.broadcasted_iota(I32, (MOE_RCH, MOE_TILE), 0) + r0).astype(F32)
        sorted_ref[pl.ds(r0, MOE_RCH), :] = _dot(_one_hot_rows(row_id, lk, 0), xaug_ref[...]).astype(BF16)

        def send_unit(uu, c):
            unit = rc * upc + uu
            unit_copy(unit, ud_ref[j * (MOE_RCAP // MOE_UNIT) + unit]).start()
            return c

        lax.fori_loop(0, jnp.clip(used_units - rc * upc, 0, upc), send_unit, 0)
        return carry

    lax.fori_loop(0, (used_units + upc - 1) // upc, sort_chunk, 0)

    def drain_unit(uu, c):
        unit_copy(0, 0).wait()
        return c

    lax.fori_loop(0, used_units, drain_unit, 0)

    @pl.when(j == pl.num_programs(0) - 1)
    def _():
        zero_ref[...] = jnp.zeros(zero_ref.shape, BF16)

        def pad_expert(e, carry):
            def pad_copy(uu):
                return pltpu.make_async_copy(zero_ref, xs_ref.at[_unit_rows(pst_ref[e] + uu)], sem)

            def start(uu, c):
                pad_copy(uu).start()
                return c

            def wait(uu, c):
                pad_copy(uu).wait()
                return c

            lax.fori_loop(0, pnn_ref[e], start, 0)
            lax.fori_loop(0, pnn_ref[e], wait, 0)
            return carry

        lax.fori_loop(0, N_EXPERTS, pad_expert, 0)


def _dispatch(u2, gates_t, plan):
    t, d = u2.shape
    n_tiles = t // MOE_TILE
    xw = d + MOE_GW
    tri = jnp.asarray(np.triu(np.ones((MOE_TILE, MOE_TILE), np.float32), k=1), BF16)
    grid_spec = pltpu.PrefetchScalarGridSpec(
        num_scalar_prefetch=4,
        grid=(n_tiles,),
        in_specs=[pl.BlockSpec((MOE_TILE, d), lambda j, *_: (j, 0)),
                  pl.BlockSpec((N_EXPERTS, MOE_TILE), lambda j, *_: (0, j)),
                  pl.BlockSpec((1, N_EXPERTS, LANES), lambda j, *_: (j, 0, 0)),
                  pl.BlockSpec((MOE_TILE, MOE_TILE), lambda j, *_: (0, 0))],
        out_specs=[pl.BlockSpec(memory_space=pl.ANY),
                   pl.BlockSpec((1, MOE_TILE, LANES), lambda j, *_: (j, 0, 0))],
        scratch_shapes=[pltpu.VMEM((MOE_RCAP, xw), BF16), pltpu.VMEM((MOE_TILE, xw), BF16),
                        pltpu.VMEM((MOE_UNIT, xw), BF16), pltpu.SemaphoreType.DMA(())])
    return pl.pallas_call(
        _dispatch_kernel,
        grid_spec=grid_spec,
        out_shape=[jax.ShapeDtypeStruct((plan["n_blocks"] * MOE_BM, xw), BF16),
                   jax.ShapeDtypeStruct((n_tiles, MOE_TILE, LANES), F32)],
        compiler_params=_cparams(("arbitrary",), 56),
        name="dispatch",
    )(plan["unit_dst"], plan["used"], plan["padstart"], plan["padn"], u2, gates_t, plan["loff_v"], tri)


def _experts_kernel(be_ref, sb_ref, br_ref, nbu_ref, x_ref, wg_ref, wu_ref, wd_ref, y_ref, wgb_ref, wub_ref, wdb_ref):
    b = pl.program_id(0)

    @pl.when(b < nbu_ref[0])
    def _():
        e = be_ref[b]

        @pl.when(jnp.logical_or(b == 0, e != be_ref[jnp.maximum(b - 1, 0)]))
        def _():
            wgb_ref[...] = wg_ref[0].astype(BF16)
            wub_ref[...] = wu_ref[0].astype(BF16)
            wdb_ref[...] = wd_ref[0].astype(BF16)

        for rows in range(MOE_SUB, MOE_BM + 1, MOE_SUB):
            @pl.when(br_ref[b] == rows)
            def _(rows=rows):
                x = x_ref[:rows, :D_MODEL]
                g3 = ((x_ref[:rows, D_MODEL:D_MODEL + LANES].astype(F32)
                       + x_ref[:rows, D_MODEL + LANES:D_MODEL + 2 * LANES].astype(F32))
                      + x_ref[:rows, D_MODEL + 2 * LANES:].astype(F32))
                lane = lax.broadcasted_iota(I32, g3.shape, 1)
                gate = jnp.sum(jnp.where(lane == e, g3, 0.0), axis=1, keepdims=True)
                hid = (_silu(_dot(x, wgb_ref[...])) * _dot(x, wub_ref[...]) * gate).astype(BF16)
                y_ref[:rows, :] = _dot(hid, wdb_ref[...]).astype(y_ref.dtype)


def _experts(xs, plan, wg, wu, wd):
    ns, xw = xs.shape
    _, d, f = wg.shape
    step = lambda b, nbu: jnp.minimum(b, nbu[0] - 1)
    rows_map = lambda b, be, sb, br, nbu: (sb[step(b, nbu)], 0)
    w_map = lambda b, be, sb, br, nbu: (be[step(b, nbu)], 0, 0)
    grid_spec = pltpu.PrefetchScalarGridSpec(
        num_scalar_prefetch=4,
        grid=(ns // MOE_BM,),
        in_specs=[pl.BlockSpec((MOE_BM, xw), rows_map),
                  pl.BlockSpec((1, d, f), w_map), pl.BlockSpec((1, d, f), w_map), pl.BlockSpec((1, f, d), w_map)],
        out_specs=pl.BlockSpec((MOE_BM, d), rows_map),
        scratch_shapes=[pltpu.VMEM((d, f), BF16), pltpu.VMEM((d, f), BF16), pltpu.VMEM((f, d), BF16)])
    return pl.pallas_call(
        _experts_kernel,
        grid_spec=grid_spec,
        out_shape=jax.ShapeDtypeStruct((ns, d), BF16),
        compiler_params=_cparams(("arbitrary",), 52),
        name="experts",
    )(plan["step_e"], plan["step_blk"], plan["step_rows"], plan["nb_used"], xs, wg, wu, wd)


def _final_kernel(ud_ref, used_ref, lkt_ref, u_ref, wg_ref, wu_ref, wd_ref, x1_ref, g2_ref, lg_ref, lb_ref,
                  ys_ref, o_ref, ybuf_ref, sems):
    j = pl.program_id(0)
    used_units = used_ref[j]
    upc = MOE_RCH // MOE_UNIT

    @pl.when(j == 0)
    def _():
        ybuf_ref[...] = jnp.zeros(ybuf_ref.shape, ybuf_ref.dtype)

    def unit_copy(src_unit, dst_unit, chunk):
        return pltpu.make_async_copy(ys_ref.at[_unit_rows(src_unit)], ybuf_ref.at[_unit_rows(dst_unit)],
                                     sems.at[chunk])

    def fetch_unit(unit, c):
        unit_copy(ud_ref[j * (MOE_RCAP // MOE_UNIT) + unit], unit, unit // upc).start()
        return c

    lax.fori_loop(0, used_units, fetch_unit, 0)

    u = u_ref[...]
    hid = (_silu(_dot(u, wg_ref[...])) * _dot(u, wu_ref[...])).astype(BF16)
    o_ref[...] = _dot(hid, wd_ref[...])

    lkt = lkt_ref[0]

    def combine_chunk(rc, carry):
        def wait_unit(uu, c):
            unit_copy(0, 0, rc).wait()
            return c

        lax.fori_loop(0, jnp.clip(used_units - rc * upc, 0, upc), wait_unit, 0)
        r0 = pl.multiple_of(rc * MOE_RCH, MOE_RCH)
        row_id = (lax.broadcasted_iota(I32, (MOE_TILE, MOE_RCH), 1) + r0).astype(F32)
        o_ref[...] += _dot(_one_hot_rows(row_id, lkt, 1), ybuf_ref[pl.ds(r0, MOE_RCH), :])
        return carry

    lax.fori_loop(0, (used_units + upc - 1) // upc, combine_chunk, 0)
    h = ALPHA * x1_ref[...] + g2_ref[0] * o_ref[...]
    o_ref[...] = _layer_norm_rows(h, lg_ref[...], lb_ref[...])


def _final(u2, wg_s, wu_s, wd_s, ys, lkt, plan, x1, gate2, ln_g, ln_b, seq):
    t, d = x1.shape
    tm = MOE_TILE
    tpb = seq // tm
    f = wg_s.shape[1]
    once = pl.Buffered(1)
    rows = lambda: pl.BlockSpec((tm, d), lambda j, *_: (j, 0))
    full = lambda shape: pl.BlockSpec(shape, lambda j, *_: (0,) * len(shape), pipeline_mode=once)
    grid_spec = pltpu.PrefetchScalarGridSpec(
        num_scalar_prefetch=2,
        grid=(t // tm,),
        in_specs=[pl.BlockSpec((1, tm, LANES), lambda j, *_: (j, 0, 0)), rows(),
                  full((d, f)), full((d, f)), full((f, d)), rows(),
                  pl.BlockSpec((1, 1, d), lambda j, *_: (j // tpb, 0, 0)), full((1, d)), full((1, d)),
                  pl.BlockSpec(memory_space=pl.ANY)],
        out_specs=rows(),
        scratch_shapes=[pltpu.VMEM((MOE_RCAP, d), BF16), pltpu.SemaphoreType.DMA((MOE_RCAP // MOE_RCH,))])
    return pl.pallas_call(
        _final_kernel,
        grid_spec=grid_spec,
        out_shape=jax.ShapeDtypeStruct((t, d), F32),
        compiler_params=_cparams(("arbitrary",), 58),
        name="final",
    )(plan["unit_dst"], plan["used"], lkt, u2, wg_s, wu_s, wd_s, x1, gate2, ln_g[None, :], ln_b[None, :], ys)


def _split_hi_lo(w):
    hi = w.astype(BF16)
    lo = (w - hi.astype(F32)).astype(BF16)
    return hi, lo


def _rope_tables(positions, dim, reps):
    inv_freq = 1.0 / (ROPE_THETA ** (jnp.arange(0, dim, 2, dtype=F32) / dim))
    ang = positions.astype(F32).reshape(-1)[:, None] * inv_freq
    cos, sin = jnp.cos(ang), jnp.sin(ang)
    return (jnp.tile(jnp.concatenate([cos, cos], axis=1), (1, reps)),
            jnp.tile(jnp.concatenate([-sin, sin], axis=1), (1, reps)))


def _layer(x2, c, positions, bsz, seq, w_ada, b_ada, w_in, conv_w, conv_b, dt_bias, a_log, d_skip, ssd_norm_w,
           idx_k_ln_g, idx_k_ln_b, w_out, ln1_g, ln1_b, w_router, router_bias, w_gate_e, w_up_e, w_down_e,
           w_gate_s, w_up_s, w_down_s, ln2_g, ln2_b, tables):
    d = D_MODEL
    mod = _ada(c, w_ada, b_ada).reshape(bsz, 6, 1, d)
    shift1, scale1, gate1, shift2, scale2, gate2 = [mod[:, k] for k in range(6)]

    o_z, o_xbc, o_dt, o_q = 0, D_SSM, 2 * D_SSM + 2 * SSM_GROUPS * SSM_STATE, 2 * D_SSM + 2 * SSM_GROUPS * SSM_STATE + SSM_HEADS
    o_k, o_v, o_qi = o_q + D_ATTN, o_q + 2 * D_ATTN, o_q + 3 * D_ATTN
    o_ki = o_qi + IDX_HEADS * IDX_DIM
    o_wi = o_ki + IDX_DIM
    w_t = w_in.T
    w_main = jnp.concatenate([w_t[o_z:o_z + D_SSM], w_t[o_xbc:o_xbc + D_SSM], w_t[o_q:o_ki],
                              w_t[o_xbc + D_SSM:o_dt]], axis=0).astype(BF16)
    padr = lambda w: jnp.pad(w, ((0, LANES - w.shape[0]), (0, 0)))
    w_tail = jnp.concatenate([padr(w_t[o_dt:o_q]), padr(w_t[o_ki:o_wi]), padr(w_t[o_wi:])], axis=0)
    wt_hi, wt_lo = _split_hi_lo(w_tail)

    proj, tail = _inproj(x2, scale1, shift1, w_main, wt_hi, wt_lo, seq)
    y_ssm = _ssm(proj, tail, conv_w, conv_b, dt_bias, a_log, d_skip, ssd_norm_w, bsz, seq)
    q_r, k_r, v_t, qi_r, ki0, ki1, w_s = _prep(proj, tail, *tables, idx_k_ln_g, idx_k_ln_b)
    y_att = _attn(q_r, k_r, v_t, qi_r, ki0, ki1, w_s, bsz, seq)

    wo = w_out.astype(BF16)
    wr_hi, wr_lo = _split_hi_lo(w_router.T)
    rbias = jnp.broadcast_to(router_bias[:, None], (N_EXPERTS, LANES))
    x1, u2, gates_t = _outproj(y_ssm, y_att, wo[:D_SSM], wo[D_SSM:], x2, gate1, scale2, shift2, ln1_g, ln1_b,
                               wr_hi, wr_lo, rbias, seq)
    plan = _moe_plan(gates_t, x2.shape[0] // MOE_TILE)
    xs, lkt = _dispatch(u2, gates_t, plan)
    ys = _experts(xs, plan, w_gate_e, w_up_e, w_down_e)
    return _final(u2, w_gate_s.astype(BF16), w_up_s.astype(BF16), w_down_s.astype(BF16), ys, lkt, plan, x1, gate2,
                  ln2_g, ln2_b, seq)


def kernel(x, c, positions, w_ada, b_ada, w_in, conv_w, conv_b, dt_bias, a_log, d_skip, ssd_norm_w, idx_k_ln_g, idx_k_ln_b, w_out, ln1_g, ln1_b, w_router, router_bias, w_gate_e, w_up_e, w_down_e, w_gate_s, w_up_s, w_down_s, ln2_g, ln2_b):
    bsz, seq, d = x.shape
    tables = _rope_tables(positions, ATTN_HEAD_DIM, 1) + _rope_tables(positions, IDX_DIM, 2)
    x2 = x.reshape(bsz * seq, d)
    for l in range(w_ada.shape[0]):
        x2 = _layer(x2, c, positions, bsz, seq, w_ada[l], b_ada[l], w_in[l], conv_w[l], conv_b[l], dt_bias[l],
                    a_log[l], d_skip[l], ssd_norm_w[l], idx_k_ln_g[l], idx_k_ln_b[l], w_out[l], ln1_g[l], ln1_b[l],
                    w_router[l], router_bias[l], w_gate_e[l], w_up_e[l], w_down_e[l], w_gate_s[l], w_up_s[l],
                    w_down_s[l], ln2_g[l], ln2_b[l], tables)
    return x2.reshape(bsz, seq, d)
```

```python
import functools
import math

import jax
import jax.numpy as jnp
import numpy as np
from jax import lax
from jax.experimental import pallas as pl
from jax.experimental.pallas import tpu as pltpu

F32 = jnp.float32
BF16 = jnp.bfloat16
I32 = jnp.int32
HIGHEST = lax.Precision.HIGHEST

D_MODEL = 2048
D_SSM = 1024
D_ATTN = 1024
SSM_HEAD_DIM = 64
SSM_HEADS = 16
SSM_GROUPS = 2
SSM_STATE = 128
CONV_WIDTH = 4
SSD_CHUNK = 128
ATTN_HEAD_DIM = 128
ATTN_HEADS = 8
IDX_HEADS = 16
IDX_DIM = 64
INDEX_TOPK = 256
ROPE_THETA = 10000.0
N_EXPERTS = 64
N_EXPERT_GROUPS = 8
EXPERTS_PER_GROUP = 8
TOPK_GROUPS = 4
TOP_K = 8
D_EXPERT = 512
D_SHARED = 512
ROUTED_SCALE = 2.5
DEPTH = 1
ALPHA = (2.0 * DEPTH) ** 0.25
LN_EPS = 1e-5

LANES = 128
SUBLANES = 8
VMEM_BYTES_V7X = 64 * 1024 * 1024
INT_MIN = -(2 ** 31)
KEY_LOWEST_FINITE = INT_MIN + 0x00800000

COL_Z, COL_XS, COL_Q, COL_K, COL_V, COL_QI, COL_BC = 0, 1024, 2048, 3072, 4096, 5120, 6144
N_MAIN = 6656
N_TAIL = 3 * LANES


def _cparams(sem, vmem_mb):
    return pltpu.CompilerParams(dimension_semantics=sem, vmem_limit_bytes=vmem_mb * 1024 * 1024)


def _silu(v):
    return v * (1.0 / (1.0 + jnp.exp(-v)))


def _dot(a, b, precision=None):
    return jnp.dot(a, b, preferred_element_type=F32, precision=precision)


def _dot_nt(a, b, precision=None):
    return lax.dot_general(a, b, (((1,), (1,)), ((), ())), preferred_element_type=F32, precision=precision)


def _split3(x):
    hi = x.astype(BF16)
    r1 = x - hi.astype(F32)
    mid = r1.astype(BF16)
    return hi, mid, (r1 - mid.astype(F32)).astype(BF16)


def _layer_norm_rows(h, g, b):
    mu = jnp.mean(h, axis=-1, keepdims=True)
    d = h - mu
    var = jnp.mean(d * d, axis=-1, keepdims=True)
    return d * lax.rsqrt(var + LN_EPS) * g + b


def _ada_kernel(cb_ref, w_ref, b_ref, o_ref):
    tn = w_ref.shape[1]
    for bi in range(cb_ref.shape[0]):
        cb = _silu(cb_ref[bi])
        cols = [jnp.sum(w_ref[:, j * LANES:(j + 1) * LANES] * cb, axis=0, keepdims=True)
                for j in range(tn // LANES)]
        o_ref[bi:bi + 1, :] = jnp.concatenate(cols, axis=1) + b_ref[...]


def _ada(c, w_ada, b_ada):
    bsz, d = c.shape
    n = w_ada.shape[1]
    tn = 1024
    cb = jnp.broadcast_to(c[:, :, None], (bsz, d, LANES))
    return pl.pallas_call(
        _ada_kernel,
        grid=(n // tn,),
        in_specs=[pl.BlockSpec((bsz, d, LANES), lambda j: (0, 0, 0)),
                  pl.BlockSpec((d, tn), lambda j: (0, j)),
                  pl.BlockSpec((1, tn), lambda j: (0, j))],
        out_specs=pl.BlockSpec((bsz, tn), lambda j: (0, j)),
        out_shape=jax.ShapeDtypeStruct((bsz, n), F32),
        compiler_params=_cparams(("parallel",), 40),
        name="ada",
    )(cb, w_ada, b_ada.reshape(1, n))


def _inproj_kernel(x_ref, sc_ref, sh_ref, w_ref, wth_ref, wtl_ref, o_ref, t_ref, u_ref):
    @pl.when(pl.program_id(1) == 0)
    def _():
        u = x_ref[...] * (1.0 + sc_ref[0]) + sh_ref[0]
        uh = u.astype(BF16)
        ul = (u - uh.astype(F32)).astype(BF16)
        u_ref[...] = uh
        t_ref[...] = (_dot_nt(uh, wth_ref[...]) + _dot_nt(uh, wtl_ref[...]) + _dot_nt(ul, wth_ref[...]))

    o_ref[...] = _dot_nt(u_ref[...], w_ref[...])


def _inproj(x2, scale1, shift1, w_main, wt_hi, wt_lo, seq):
    t, d = x2.shape
    tm, tn = 1024, 512
    tpb = seq // tm
    return pl.pallas_call(
        _inproj_kernel,
        grid=(t // tm, N_MAIN // tn),
        in_specs=[pl.BlockSpec((tm, d), lambda i, j: (i, 0)),
                  pl.BlockSpec((1, 1, d), lambda i, j: (i // tpb, 0, 0)),
                  pl.BlockSpec((1, 1, d), lambda i, j: (i // tpb, 0, 0)),
                  pl.BlockSpec((tn, d), lambda i, j: (j, 0)),
                  pl.BlockSpec((N_TAIL, d), lambda i, j: (0, 0)),
                  pl.BlockSpec((N_TAIL, d), lambda i, j: (0, 0))],
        out_specs=[pl.BlockSpec((tm, tn), lambda i, j: (i, j)),
                   pl.BlockSpec((tm, N_TAIL), lambda i, j: (i, 0))],
        out_shape=[jax.ShapeDtypeStruct((t, N_MAIN), F32),
                   jax.ShapeDtypeStruct((t, N_TAIL), F32)],
        scratch_shapes=[pltpu.VMEM((tm, d), BF16)],
        compiler_params=_cparams(("parallel", "arbitrary"), 48),
        name="inproj",
    )(x2, scale1, shift1, w_main, wt_hi, wt_lo)


def _ssm_kernel(z_ref, xs_ref, bc_ref, dt_ref, cwx_ref, cbx_ref, cwb_ref, cbb_ref, dtb_ref, alog_ref,
                dsk_ref, nw_ref, e_ref, e2_ref, tril_ref, o_ref, px_ref, pb_ref, st_ref):
    q = SSD_CHUNK
    hpg = SSM_HEADS // SSM_GROUPS
    gw = hpg * SSM_HEAD_DIM

    @pl.when(pl.program_id(1) == 0)
    def _():
        px_ref[0:SUBLANES, :] = jnp.zeros((SUBLANES, px_ref.shape[1]), F32)
        pb_ref[0:SUBLANES, :] = jnp.zeros((SUBLANES, pb_ref.shape[1]), F32)
        st_ref[...] = jnp.zeros(st_ref.shape, F32)

    def conv_silu(raw_ref, pad_ref, w_ref, b_ref):
        pad_ref[SUBLANES:SUBLANES + q, :] = raw_ref[...]
        acc = b_ref[...] + w_ref[0:1, :] * pad_ref[SUBLANES - 3:SUBLANES - 3 + q, :]
        for k in range(1, CONV_WIDTH):
            acc = acc + w_ref[k:k + 1, :] * pad_ref[SUBLANES - 3 + k:SUBLANES - 3 + k + q, :]
        pad_ref[0:SUBLANES, :] = raw_ref[q - SUBLANES:q, :]
        return _silu(acc)

    xs = conv_silu(xs_ref, px_ref, cwx_ref, cbx_ref)
    bc = conv_silu(bc_ref, pb_ref, cwb_ref, cbb_ref)

    dtr = dt_ref[...] + dtb_ref[...]
    dt = jnp.maximum(dtr, 0.0) + jnp.log(1.0 + jnp.exp(-jnp.abs(dtr)))
    log_a = dt * (-jnp.exp(alog_ref[...]))
    cs = sum(_dot(tril_ref[...], part) for part in _split3(log_a))
    cs_parts = _split3(cs)
    cs_e = sum(_dot(part, e_ref[...]) for part in cs_parts)
    dt_e = sum(_dot(part, e_ref[...]) for part in _split3(dt))
    cs_col = sum(_dot(part, e2_ref[...]) for part in cs_parts)
    cs_t = cs.T
    cs_last = cs_e[q - 1:q, :]

    xdt = xs * dt_e
    rows = lax.broadcasted_iota(I32, (q, q), 0)
    cols = lax.broadcasted_iota(I32, (q, q), 1)
    causal = rows >= cols
    first_half = lax.broadcasted_iota(I32, (q, LANES), 1) < SSM_HEAD_DIM

    y_pairs = []
    for g in range(SSM_GROUPS):
        b_g = bc[:, g * SSM_STATE:(g + 1) * SSM_STATE]
        c_g = bc[:, (SSM_GROUPS + g) * SSM_STATE:(SSM_GROUPS + g + 1) * SSM_STATE]
        cb = _dot_nt(c_g.astype(BF16), b_g.astype(BF16))
        for hp in range(hpg // 2):
            pair = g * (hpg // 2) + hp
            x_pair = xdt[:, pair * LANES:(pair + 1) * LANES].astype(BF16)
            ys = []
            for sub in range(2):
                h = 2 * pair + sub
                seg = cs_col[:, h * LANES:(h + 1) * LANES] - cs_t[h:h + 1, :]
                dec = jnp.exp(jnp.where(causal, seg, -jnp.inf))
                ys.append(_dot((cb * dec).astype(BF16), x_pair))
            y_pairs.append(jnp.where(first_half, ys[0], ys[1]))
    y_diag = jnp.concatenate(y_pairs, axis=1)

    xw = xdt * jnp.exp(cs_last - cs_e)
    y_off, new_states = [], []
    for g in range(SSM_GROUPS):
        b_g = bc[:, g * SSM_STATE:(g + 1) * SSM_STATE]
        c_g = bc[:, (SSM_GROUPS + g) * SSM_STATE:(SSM_GROUPS + g + 1) * SSM_STATE]
        h_in = st_ref[:, g * gw:(g + 1) * gw]
        y_off.append(_dot(c_g.astype(BF16), h_in.astype(BF16)))
        new_states.append(_dot(b_g.T.astype(BF16), xw[:, g * gw:(g + 1) * gw].astype(BF16)))
    y_off = jnp.concatenate(y_off, axis=1) * jnp.exp(cs_e)
    st_ref[...] = jnp.exp(cs_last) * st_ref[...] + jnp.concatenate(new_states, axis=1)

    y = y_diag + y_off + dsk_ref[...] * xs
    yf = y * _silu(z_ref[...])
    ms = jnp.mean(yf * yf, axis=-1, keepdims=True)
    o_ref[...] = (yf * lax.rsqrt(ms + LN_EPS) * nw_ref[...]).astype(o_ref.dtype)


def _ssm(proj, tail, conv_w, conv_b, dt_bias, a_log, d_skip, ssd_norm_w, bsz, seq):
    t = proj.shape[0]
    q = SSD_CHUNK
    n_c = seq // q
    nbc = 2 * SSM_GROUPS * SSM_STATE
    cw_x, cw_b = conv_w[:, :D_SSM], conv_w[:, D_SSM:]
    cb_x, cb_b = conv_b[None, :D_SSM], conv_b[None, D_SSM:]
    pad16 = lambda v: jnp.pad(v, (0, LANES - SSM_HEADS))[None, :]
    head_of_lane = np.arange(D_SSM) // SSM_HEAD_DIM
    e_mat = jnp.asarray((np.arange(LANES)[:, None] == head_of_lane[None, :]).astype(np.float32), BF16)
    e2_mat = jnp.asarray((np.arange(LANES)[:, None] == (np.arange(SSM_HEADS * LANES) // LANES)[None, :])
                         .astype(np.float32), BF16)
    tril = jnp.asarray(np.tril(np.ones((q, q), np.float32)), BF16)
    row = lambda b, c: b * n_c + c
    full = lambda shape: pl.BlockSpec(shape, lambda b, c: (0,) * len(shape))
    return pl.pallas_call(
        _ssm_kernel,
        grid=(bsz, n_c),
        in_specs=[pl.BlockSpec((q, D_SSM), lambda b, c: (row(b, c), COL_Z // D_SSM)),
                  pl.BlockSpec((q, D_SSM), lambda b, c: (row(b, c), COL_XS // D_SSM)),
                  pl.BlockSpec((q, nbc), lambda b, c: (row(b, c), COL_BC // nbc)),
                  pl.BlockSpec((q, LANES), lambda b, c: (row(b, c), 0)),
                  full((CONV_WIDTH, D_SSM)), full((1, D_SSM)), full((CONV_WIDTH, nbc)), full((1, nbc)),
                  full((1, LANES)), full((1, LANES)), full((1, D_SSM)), full((1, D_SSM)),
                  full((LANES, D_SSM)), full((LANES, SSM_HEADS * LANES)), full((q, q))],
        out_specs=pl.BlockSpec((q, D_SSM), lambda b, c: (row(b, c), 0)),
        out_shape=jax.ShapeDtypeStruct((t, D_SSM), BF16),
        scratch_shapes=[pltpu.VMEM((SUBLANES + q, D_SSM), F32),
                        pltpu.VMEM((SUBLANES + q, nbc), F32),
                        pltpu.VMEM((SSM_STATE, D_SSM), F32)],
        compiler_params=_cparams(("parallel", "arbitrary"), 40),
        name="ssm",
    )(proj, proj, proj, tail, cw_x, cb_x, cw_b, cb_b, pad16(dt_bias), pad16(a_log),
      jnp.repeat(d_skip, SSM_HEAD_DIM)[None, :], ssd_norm_w[None, :], e_mat, e2_mat, tril)


def _prep_kernel(q_ref, k_ref, v_ref, qi_ref, ki_ref, wi_ref, ca_ref, sa_ref, ci_ref, si_ref, lg_ref, lb_ref,
                 qo_ref, ko_ref, vo_ref, qio_ref, ki0_ref, ki1_ref, wo_ref):
    ca, sa, ci, si = ca_ref[...], sa_ref[...], ci_ref[...], si_ref[...]
    lane = lax.broadcasted_iota(I32, ca.shape, 1)
    first32 = (lane % IDX_DIM) < (IDX_DIM // 2)
    q_scale = ATTN_HEAD_DIM ** -0.5 * math.log2(math.e)

    def rope_attn(v):
        return v * ca + pltpu.roll(v, ATTN_HEAD_DIM // 2, 1) * sa

    def rope_idx(v):
        rot = jnp.where(first32, pltpu.roll(v, LANES - IDX_DIM // 2, 1), pltpu.roll(v, IDX_DIM // 2, 1))
        return v * ci + rot * si

    for h in range(ATTN_HEADS):
        sl = slice(h * LANES, (h + 1) * LANES)
        qo_ref[:, sl] = (rope_attn(q_ref[:, sl]) * q_scale).astype(BF16)
        ko_ref[:, sl] = rope_attn(k_ref[:, sl]).astype(BF16)
    vo_ref[0] = v_ref[...].T.astype(BF16)
    for p in range(IDX_HEADS * IDX_DIM // LANES):
        sl = slice(p * LANES, (p + 1) * LANES)
        qio_ref[:, sl] = rope_idx(qi_ref[:, sl]).astype(BF16)

    kraw = ki_ref[...]
    valid = lane < IDX_DIM
    mu = jnp.sum(kraw, axis=-1, keepdims=True) * (1.0 / IDX_DIM)
    dk = jnp.where(valid, kraw - mu, 0.0)
    var = jnp.sum(dk * dk, axis=-1, keepdims=True) * (1.0 / IDX_DIM)
    kn = jnp.where(valid, dk * lax.rsqrt(var + LN_EPS) * lg_ref[...] + lb_ref[...], 0.0)
    kr = jnp.where(valid, rope_idx(kn), 0.0)
    ki0_ref[...] = kr.astype(BF16)
    ki1_ref[...] = pltpu.roll(kr, IDX_DIM, 1).astype(BF16)
    wo_ref[...] = wi_ref[...] * (IDX_HEADS ** -0.5 * IDX_DIM ** -0.5)


def _prep(proj, tail, cos_a, sin_a, cos_i, sin_i, ln_g, ln_b):
    t = proj.shape[0]
    tm = ATT_KC
    w = D_ATTN
    pad64 = lambda v: jnp.pad(v, (0, LANES - IDX_DIM))[None, :]
    col = lambda c: pl.BlockSpec((tm, w), lambda i: (i, c // w))
    lane_blk = lambda c: pl.BlockSpec((tm, LANES), lambda i: (i, c))
    full = lambda: pl.BlockSpec((1, LANES), lambda i: (0, 0))
    return pl.pallas_call(
        _prep_kernel,
        grid=(t // tm,),
        in_specs=[col(COL_Q), col(COL_K), col(COL_V), col(COL_QI), lane_blk(1), lane_blk(2),
                  lane_blk(0), lane_blk(0), lane_blk(0), lane_blk(0), full(), full()],
        out_specs=[pl.BlockSpec((tm, w), lambda i: (i, 0))] * 2 + [pl.BlockSpec((1, w, tm), lambda i: (i, 0, 0))]
        + [pl.BlockSpec((tm, w), lambda i: (i, 0))] + [pl.BlockSpec((tm, LANES), lambda i: (i, 0))] * 3,
        out_shape=[jax.ShapeDtypeStruct((t, w), BF16)] * 2 + [jax.ShapeDtypeStruct((t // tm, w, tm), BF16)]
        + [jax.ShapeDtypeStruct((t, w), BF16)]
        + [jax.ShapeDtypeStruct((t, LANES), BF16)] * 2 + [jax.ShapeDtypeStruct((t, LANES), F32)],
        compiler_params=_cparams(("parallel",), 48),
        name="prep",
    )(proj, proj, proj, proj, tail, tail, cos_a, sin_a, cos_i, sin_i, pad64(ln_g), pad64(ln_b))


ATT_QB = 128
ATT_KC = 512
ATT_KS = 256


def _attn_kernel(q_ref, k_ref, vt_ref, qi_ref, ki0_ref, ki1_ref, w_ref, o_ref, sc_ref, acc_ref, *, topk):
    i = pl.program_id(1)
    n_ch = (i * ATT_QB + ATT_QB + ATT_KC - 1) // ATT_KC
    key_off = lax.broadcasted_iota(I32, (ATT_KC, ATT_QB), 0)
    q_pos = i * ATT_QB + lax.broadcasted_iota(I32, (ATT_KC, ATT_QB), 1)
    w_t = w_ref[...].T

    def score_chunk(c, carry):
        start = pl.multiple_of(c * ATT_KC, ATT_KC)
        ki0 = ki0_ref[pl.ds(start, ATT_KC), :]
        ki1 = ki1_ref[pl.ds(start, ATT_KC), :]
        acc = jnp.zeros((ATT_KC, ATT_QB), F32)
        for h in range(IDX_HEADS):
            pair = qi_ref[:, (h // 2) * LANES:(h // 2 + 1) * LANES]
            rel = _dot_nt(ki0 if h % 2 == 0 else ki1, pair)
            acc = acc + jnp.maximum(rel, 0.0) * w_t[h:h + 1, :]
        sc_ref[c] = jnp.where(start + key_off <= q_pos, acc, -jnp.inf)
        return carry

    lax.fori_loop(0, n_ch, score_chunk, 0)

    def key_to_float(key):
        return pltpu.bitcast(jnp.where(key < 0, key ^ jnp.int32(0x7FFFFFFF), key), F32)

    def search_bit(it, ans):
        cand = ans ^ lax.shift_left(jnp.int32(1), jnp.int32(31) - it)
        cand_f = key_to_float(cand)

        def count_chunk(c, cnt):
            part = jnp.where(sc_ref[c] >= cand_f, 1, 0).astype(I32)
            return cnt + jnp.sum(part.reshape(ATT_KC // SUBLANES, SUBLANES, ATT_QB), axis=0)

        cnt = lax.fori_loop(0, n_ch, count_chunk, jnp.zeros((SUBLANES, ATT_QB), I32))
        total = jnp.sum(cnt, axis=0, keepdims=True)
        return jnp.where(total >= topk, cand, ans)

    kth = lax.fori_loop(0, 32, search_bit, jnp.full((1, ATT_QB), INT_MIN, I32))
    lowest_finite = jnp.float32(np.finfo(np.float32).min)
    thr = jnp.where(kth < KEY_LOWEST_FINITE, lowest_finite, key_to_float(kth))

    acc_ref[...] = jnp.zeros(acc_ref.shape, F32)

    def att_chunk(c, carry):
        m_all, l_all = carry
        start = pl.multiple_of(c * ATT_KC, ATT_KC)
        for part in range(ATT_KC // ATT_KS):
            ks = slice(part * ATT_KS, (part + 1) * ATT_KS)
            sel = sc_ref[c, ks, :] >= thr
            m_rows, l_rows = [], []
            for h in range(ATTN_HEADS):
                sl = slice(h * LANES, (h + 1) * LANES)
                s = _dot_nt(k_ref[pl.ds(start + part * ATT_KS, ATT_KS), sl], q_ref[:, sl])
                s = jnp.where(sel, s, -jnp.inf)
                m_old = m_all[h:h + 1, :]
                m_new = jnp.maximum(m_old, jnp.max(s, axis=0, keepdims=True))
                p = jnp.exp2(s - m_new)
                alpha = jnp.exp2(m_old - m_new)
                l_rows.append(alpha * l_all[h:h + 1, :] + jnp.sum(p, axis=0, keepdims=True))
                m_rows.append(m_new)
                acc_ref[h] = alpha * acc_ref[h] + _dot(vt_ref[c, sl, ks], p.astype(BF16))
            m_all = jnp.concatenate(m_rows, axis=0)
            l_all = jnp.concatenate(l_rows, axis=0)
        return m_all, l_all

    init = (jnp.full((ATTN_HEADS, ATT_QB), -1e30, F32), jnp.zeros((ATTN_HEADS, ATT_QB), F32))
    _, l_all = lax.fori_loop(0, n_ch, att_chunk, init)
    for h in range(ATTN_HEADS):
        o_ref[:, h * LANES:(h + 1) * LANES] = (acc_ref[h] / l_all[h:h + 1, :]).T.astype(o_ref.dtype)


def _attn(q_r, k_r, v_t, qi_r, ki0, ki1, w_s, bsz, seq):
    t = q_r.shape[0]
    nq = seq // ATT_QB
    n_kc = seq // ATT_KC
    topk = min(INDEX_TOPK, seq // 4)
    once = pl.Buffered(1)
    qblk = lambda width: pl.BlockSpec((ATT_QB, width), lambda b, i: (b * nq + i, 0))
    per_batch = lambda width: pl.BlockSpec((seq, width), lambda b, i: (b, 0), pipeline_mode=once)
    return pl.pallas_call(
        functools.partial(_attn_kernel, topk=topk),
        grid=(bsz, nq),
        in_specs=[qblk(D_ATTN), per_batch(D_ATTN),
                  pl.BlockSpec((n_kc, D_ATTN, ATT_KC), lambda b, i: (b, 0, 0), pipeline_mode=once),
                  qblk(IDX_HEADS * IDX_DIM), per_batch(LANES), per_batch(LANES), qblk(LANES)],
        out_specs=qblk(D_ATTN),
        out_shape=jax.ShapeDtypeStruct((t, D_ATTN), BF16),
        scratch_shapes=[pltpu.VMEM((n_kc, ATT_KC, ATT_QB), F32),
                        pltpu.VMEM((ATTN_HEADS, ATTN_HEAD_DIM, ATT_QB), F32)],
        compiler_params=_cparams(("parallel", "arbitrary"), 48),
        name="attn",
    )(q_r, k_r, v_t, qi_r, ki0, ki1, w_s)


def _first_index_of_max(v, iota, n):
    m = jnp.max(v, axis=0, keepdims=True)
    idx = jnp.min(jnp.where(v == m, iota, n), axis=0, keepdims=True)
    return m, idx


def _route(scores, biased):
    tm = scores.shape[1]
    neg_inf = jnp.float32(-jnp.inf)
    iota8 = lax.broadcasted_iota(I32, (EXPERTS_PER_GROUP, tm), 0)
    group_scores = []
    for g in range(N_EXPERT_GROUPS):
        v = biased[g * EXPERTS_PER_GROUP:(g + 1) * EXPERTS_PER_GROUP, :]
        m1, i1 = _first_index_of_max(v, iota8, EXPERTS_PER_GROUP)
        m2 = jnp.max(jnp.where(iota8 == i1, neg_inf, v), axis=0, keepdims=True)
        group_scores.append(m1 + m2)
    gs = jnp.concatenate(group_scores, axis=0)
    keep = jnp.zeros(gs.shape, jnp.bool_)
    for _ in range(TOPK_GROUPS):
        _, gi = _first_index_of_max(gs, iota8, N_EXPERT_GROUPS)
        hit = iota8 == gi
        keep = jnp.logical_or(keep, hit)
        gs = jnp.where(hit, neg_inf, gs)
    keep_f = jnp.where(keep, 1.0, 0.0)
    masked = jnp.concatenate(
        [jnp.where(keep_f[g:g + 1, :] > 0.0, biased[g * EXPERTS_PER_GROUP:(g + 1) * EXPERTS_PER_GROUP, :], neg_inf)
         for g in range(N_EXPERT_GROUPS)], axis=0)
    iota64 = lax.broadcasted_iota(I32, (N_EXPERTS, tm), 0)
    sel_w = jnp.zeros((N_EXPERTS, tm), F32)
    for _ in range(TOP_K):
        _, ei = _first_index_of_max(masked, iota64, N_EXPERTS)
        hit = iota64 == ei
        sel_w = jnp.where(hit, scores, sel_w)
        masked = jnp.where(hit, neg_inf, masked)
    denom = jnp.sum(sel_w, axis=0, keepdims=True)
    return sel_w / denom * ROUTED_SCALE


def _outproj_kernel(ys_ref, ya_ref, wa_ref, wb_ref, x_ref, g1_ref, sc2_ref, sh2_ref, lg_ref, lb_ref,
                    wrh_ref, wrl_ref, rb_ref, x1_ref, u2_ref, gt_ref):
    mix = _dot(ys_ref[...], wa_ref[...]) + _dot(ya_ref[...], wb_ref[...])
    h = ALPHA * x_ref[...] + g1_ref[0] * mix
    x1 = _layer_norm_rows(h, lg_ref[...], lb_ref[...])
    x1_ref[...] = x1
    u2 = x1 * (1.0 + sc2_ref[0]) + sh2_ref[0]
    uh = u2.astype(BF16)
    ul = (u2 - uh.astype(F32)).astype(BF16)
    u2_ref[...] = uh
    wrh = wrh_ref[...]
    logits = _dot_nt(wrh, uh) + _dot_nt(wrh, ul) + _dot_nt(wrl_ref[...], uh)
    scores = 1.0 / (1.0 + jnp.exp(-logits))
    gt_ref[...] = _route(scores, scores + rb_ref[:, 0:1])


def _outproj(y_ssm, y_att, wo_a, wo_b, x2, gate1, scale2, shift2, ln_g, ln_b, wr_hi, wr_lo, rbias, seq):
    t, d = x2.shape
    tm = 512
    tpb = seq // tm
    half = y_ssm.shape[1]
    rows = lambda width: pl.BlockSpec((tm, width), lambda i: (i, 0))
    full = lambda shape: pl.BlockSpec(shape, lambda i: (0,) * len(shape))
    mod = lambda: pl.BlockSpec((1, 1, d), lambda i: (i // tpb, 0, 0))
    return pl.pallas_call(
        _outproj_kernel,
        grid=(t // tm,),
        in_specs=[rows(half), rows(half), full((half, d)), full((half, d)), rows(d), mod(), mod(), mod(),
                  full((1, d)), full((1, d)), full((N_EXPERTS, d)), full((N_EXPERTS, d)), full((N_EXPERTS, LANES))],
        out_specs=[rows(d), rows(d), pl.BlockSpec((N_EXPERTS, tm), lambda i: (0, i))],
        out_shape=[jax.ShapeDtypeStruct((t, d), F32), jax.ShapeDtypeStruct((t, d), BF16),
                   jax.ShapeDtypeStruct((N_EXPERTS, t), F32)],
        compiler_params=_cparams(("parallel",), 48),
        name="outproj",
    )(y_ssm, y_att, wo_a, wo_b, x2, gate1, scale2, shift2, ln_g[None, :], ln_b[None, :], wr_hi, wr_lo, rbias)


MOE_TILE = 256
MOE_UNIT = 16
MOE_BM = 512
MOE_SUB = 128
MOE_RCAP = TOP_K * MOE_TILE + N_EXPERTS * MOE_UNIT
MOE_RCH = 512
MOE_GW = 3 * LANES
MOE_UNSET = 1e9


def _moe_plan(gates_t, n_tiles):
    upb = MOE_BM // MOE_UNIT
    ups = MOE_SUB // MOE_UNIT
    cnt = jnp.sum((gates_t > 0.0).reshape(N_EXPERTS, n_tiles, MOE_TILE), axis=2, dtype=I32).T
    nun = (cnt + MOE_UNIT - 1) // MOE_UNIT
    lend = jnp.cumsum(nun, axis=1)
    loff = lend - nun
    tot = jnp.sum(nun, axis=0)
    tot_sub = (tot + ups - 1) // ups * ups
    nb = (tot + upb - 1) // upb
    cum_nb = jnp.cumsum(nb)
    first_blk = cum_nb - nb
    base = first_blk * upb
    goff = base[None, :] + jnp.cumsum(nun, axis=0) - nun
    unit = jnp.arange(MOE_RCAP // MOE_UNIT, dtype=I32)
    shift = goff - loff
    step_up = jnp.concatenate([shift[:, :1], shift[:, 1:] - shift[:, :-1]], axis=1)
    started = jnp.concatenate([jnp.ones((n_tiles, unit.shape[0], 1), jnp.bool_),
                               lend[:, None, :-1] <= unit[None, :, None]], axis=2)
    unit_dst = unit[None, :] + jnp.sum(jnp.where(started, step_up[:, None, :], 0), axis=2)
    n_blocks = (TOP_K * n_tiles * MOE_TILE // MOE_UNIT + n_tiles * N_EXPERTS + N_EXPERTS * (upb - 1) + upb - 1) // upb
    step = jnp.arange(n_blocks, dtype=I32)
    step_e = jnp.minimum(jnp.sum((cum_nb[None, :] <= step[:, None]).astype(I32), axis=1), N_EXPERTS - 1)
    of_step = step_e[:, None] == jnp.arange(N_EXPERTS, dtype=I32)[None, :]
    pick = lambda per_expert: jnp.sum(jnp.where(of_step, per_expert[None, :], 0), axis=1)
    nb_e = jnp.maximum(pick(nb), 1)
    first_e = pick(first_blk)
    local = (step - first_e + nb_e - 1) % nb_e
    step_rows = jnp.clip(pick(tot_sub) - local * upb, 0, upb) * MOE_UNIT
    end = pick(cum_nb)
    next_e = jnp.sum((cum_nb[None, :] <= end[:, None]).astype(I32), axis=1)
    step_next_e = jnp.where(end < cum_nb[-1], jnp.minimum(next_e, N_EXPERTS - 1), -1)
    loff_v = jnp.broadcast_to((loff * MOE_UNIT).astype(F32)[:, :, None], (n_tiles, N_EXPERTS, LANES))
    flat = lambda v: v.reshape(-1).astype(I32)
    return dict(unit_dst=flat(unit_dst), used=flat(lend[:, -1]), padstart=flat(base + tot), padn=flat(tot_sub - tot),
                step_e=flat(step_e), step_blk=flat(first_e + local), step_rows=flat(step_rows),
                step_next_e=flat(step_next_e),
                nb_used=cum_nb[-1:].astype(I32), loff_v=loff_v, n_blocks=n_blocks)


def _unit_rows(unit):
    return pl.ds(pl.multiple_of(unit * MOE_UNIT, MOE_UNIT), MOE_UNIT)


def _one_hot_rows(row_id, targets, axis):
    p = jnp.zeros(row_id.shape, F32)
    for k in range(TOP_K):
        tgt = targets[k:k + 1, :] if axis == 0 else targets[:, k:k + 1]
        p = jnp.where(row_id == tgt, 1.0, p)
    return p.astype(BF16)


def _dispatch_kernel(ud_ref, used_ref, pst_ref, pnn_ref, u_ref, g_ref, lv_ref, tri_ref,
                     xs_ref, lkt_ref, sorted_ref, xaug_ref, zero_ref, sem):
    j = pl.program_id(0)
    g = g_ref[...]
    sel = g > 0.0
    rank = _dot(jnp.where(sel, 1.0, 0.0).astype(BF16), tri_ref[...])
    loff = jnp.concatenate([lv_ref[0]] * (MOE_TILE // LANES), axis=1)
    rem = jnp.where(sel, loff + rank, MOE_UNSET)
    rows = []
    for _ in range(TOP_K):
        cur = jnp.min(rem, axis=0, keepdims=True)
        rows.append(cur)
        rem = jnp.where(rem == cur, MOE_UNSET, rem)
    lk = jnp.concatenate(rows, axis=0)
    lkt_ref[0] = jnp.concatenate([lk, jnp.full((LANES - TOP_K, MOE_TILE), MOE_UNSET, F32)], axis=0).T

    gt = jnp.concatenate([g, jnp.zeros_like(g)], axis=0).T
    hi = gt.astype(BF16)
    r1 = gt - hi.astype(F32)
    mid = r1.astype(BF16)
    xaug_ref[:, :D_MODEL] = u_ref[...]
    xaug_ref[:, D_MODEL:D_MODEL + LANES] = hi
    xaug_ref[:, D_MODEL + LANES:D_MODEL + 2 * LANES] = mid
    xaug_ref[:, D_MODEL + 2 * LANES:] = (r1 - mid.astype(F32)).astype(BF16)

    used_units = used_ref[j]
    upc = MOE_RCH // MOE_UNIT

    def unit_copy(src_unit, dst_unit):
        return pltpu.make_async_copy(sorted_ref.at[_unit_rows(src_unit)], xs_ref.at[_unit_rows(dst_unit)], sem)

    def sort_chunk(rc, carry):
        r0 = pl.multiple_of(rc * MOE_RCH, MOE_RCH)
        row_id = (lax.broadcasted_iota(I32, (MOE_RCH, MOE_TILE), 0) + r0).astype(F32)
        sorted_ref[pl.ds(r0, MOE_RCH), :] = _dot(_one_hot_rows(row_id, lk, 0), xaug_ref[...]).astype(BF16)

        def send_unit(uu, c):
            unit = rc * upc + uu
            unit_copy(unit, ud_ref[j * (MOE_RCAP // MOE_UNIT) + unit]).start()
            return c

        lax.fori_loop(0, jnp.clip(used_units - rc * upc, 0, upc), send_unit, 0)
        return carry

    lax.fori_loop(0, (used_units + upc - 1) // upc, sort_chunk, 0)

    def drain_unit(uu, c):
        unit_copy(0, 0).wait()
        return c

    lax.fori_loop(0, used_units, drain_unit, 0)

    @pl.when(j == pl.num_programs(0) - 1)
    def _():
        zero_ref[...] = jnp.zeros(zero_ref.shape, BF16)

        def pad_expert(e, carry):
            def pad_copy(uu):
                return pltpu.make_async_copy(zero_ref, xs_ref.at[_unit_rows(pst_ref[e] + uu)], sem)

            def start(uu, c):
                pad_copy(uu).start()
                return c

            def wait(uu, c):
                pad_copy(uu).wait()
                return c

            lax.fori_loop(0, pnn_ref[e], start, 0)
            lax.fori_loop(0, pnn_ref[e], wait, 0)
            return carry

        lax.fori_loop(0, N_EXPERTS, pad_expert, 0)


def _dispatch(u2, gates_t, plan):
    t, d = u2.shape
    n_tiles = t // MOE_TILE
    xw = d + MOE_GW
    tri = jnp.asarray(np.triu(np.ones((MOE_TILE, MOE_TILE), np.float32), k=1), BF16)
    grid_spec = pltpu.PrefetchScalarGridSpec(
        num_scalar_prefetch=4,
        grid=(n_tiles,),
        in_specs=[pl.BlockSpec((MOE_TILE, d), lambda j, *_: (j, 0)),
                  pl.BlockSpec((N_EXPERTS, MOE_TILE), lambda j, *_: (0, j)),
                  pl.BlockSpec((1, N_EXPERTS, LANES), lambda j, *_: (j, 0, 0)),
                  pl.BlockSpec((MOE_TILE, MOE_TILE), lambda j, *_: (0, 0))],
        out_specs=[pl.BlockSpec(memory_space=pl.ANY),
                   pl.BlockSpec((1, MOE_TILE, LANES), lambda j, *_: (j, 0, 0))],
        scratch_shapes=[pltpu.VMEM((MOE_RCAP, xw), BF16), pltpu.VMEM((MOE_TILE, xw), BF16),
                        pltpu.VMEM((MOE_UNIT, xw), BF16), pltpu.SemaphoreType.DMA(())])
    return pl.pallas_call(
        _dispatch_kernel,
        grid_spec=grid_spec,
        out_shape=[jax.ShapeDtypeStruct((plan["n_blocks"] * MOE_BM, xw), BF16),
                   jax.ShapeDtypeStruct((n_tiles, MOE_TILE, LANES), F32)],
        compiler_params=_cparams(("arbitrary",), 56),
        name="dispatch",
    )(plan["unit_dst"], plan["used"], plan["padstart"], plan["padn"], u2, gates_t, plan["loff_v"], tri)


def _experts_kernel(be_ref, sb_ref, br_ref, nx_ref, nbu_ref, x_ref, wg_hbm, wu_hbm, wd_hbm, y_ref,
                    wg_st, wu_st, wd_st, wgb_ref, wub_ref, wdb_ref, sems):
    b = pl.program_id(0)

    def weight_copies(e):
        return (pltpu.make_async_copy(wg_hbm.at[e], wg_st, sems.at[0]),
                pltpu.make_async_copy(wu_hbm.at[e], wu_st, sems.at[1]),
                pltpu.make_async_copy(wd_hbm.at[e], wd_st, sems.at[2]))

    @pl.when(b < nbu_ref[0])
    def _():
        e = be_ref[b]

        @pl.when(b == 0)
        def _():
            for cp in weight_copies(e):
                cp.start()

        @pl.when(jnp.logical_or(b == 0, e != be_ref[jnp.maximum(b - 1, 0)]))
        def _():
            for cp in weight_copies(e):
                cp.wait()
            wgb_ref[...] = wg_st[...].astype(BF16)
            wub_ref[...] = wu_st[...].astype(BF16)
            wdb_ref[...] = wd_st[...].astype(BF16)

            @pl.when(nx_ref[b] >= 0)
            def _():
                for cp in weight_copies(nx_ref[b]):
                    cp.start()

        for rows in range(MOE_SUB, MOE_BM + 1, MOE_SUB):
            @pl.when(br_ref[b] == rows)
            def _(rows=rows):
                x = x_ref[:rows, :D_MODEL]
                g3 = ((x_ref[:rows, D_MODEL:D_MODEL + LANES].astype(F32)
                       + x_ref[:rows, D_MODEL + LANES:D_MODEL + 2 * LANES].astype(F32))
                      + x_ref[:rows, D_MODEL + 2 * LANES:].astype(F32))
                lane = lax.broadcasted_iota(I32, g3.shape, 1)
                gate = jnp.sum(jnp.where(lane == e, g3, 0.0), axis=1, keepdims=True)
                hid = (_silu(_dot(x, wgb_ref[...])) * _dot(x, wub_ref[...]) * gate).astype(BF16)
                y_ref[:rows, :] = _dot(hid, wdb_ref[...]).astype(y_ref.dtype)


def _experts(xs, plan, wg, wu, wd):
    ns, xw = xs.shape
    _, d, f = wg.shape
    step = lambda b, nbu: jnp.minimum(b, nbu[0] - 1)
    rows_map = lambda b, be, sb, br, nx, nbu: (sb[step(b, nbu)], 0)
    hbm = pl.BlockSpec(memory_space=pl.ANY)
    grid_spec = pltpu.PrefetchScalarGridSpec(
        num_scalar_prefetch=5,
        grid=(ns // MOE_BM,),
        in_specs=[pl.BlockSpec((MOE_BM, xw), rows_map), hbm, hbm, hbm],
        out_specs=pl.BlockSpec((MOE_BM, d), rows_map),
        scratch_shapes=[pltpu.VMEM((d, f), F32), pltpu.VMEM((d, f), F32), pltpu.VMEM((f, d), F32),
                        pltpu.VMEM((d, f), BF16), pltpu.VMEM((d, f), BF16), pltpu.VMEM((f, d), BF16),
                        pltpu.SemaphoreType.DMA((3,))])
    return pl.pallas_call(
        _experts_kernel,
        grid_spec=grid_spec,
        out_shape=jax.ShapeDtypeStruct((ns, d), BF16),
        compiler_params=_cparams(("arbitrary",), 48),
        name="experts",
    )(plan["step_e"], plan["step_blk"], plan["step_rows"], plan["step_next_e"], plan["nb_used"], xs, wg, wu, wd)


def _final_kernel(ud_ref, used_ref, lkt_ref, u_ref, wg_ref, wu_ref, wd_ref, x1_ref, g2_ref, lg_ref, lb_ref,
                  ys_ref, o_ref, ybuf_ref, sems):
    j = pl.program_id(0)
    used_units = used_ref[j]
    upc = MOE_RCH // MOE_UNIT

    @pl.when(j == 0)
    def _():
        ybuf_ref[...] = jnp.zeros(ybuf_ref.shape, ybuf_ref.dtype)

    def unit_copy(src_unit, dst_unit, chunk):
        return pltpu.make_async_copy(ys_ref.at[_unit_rows(src_unit)], ybuf_ref.at[_unit_rows(dst_unit)],
                                     sems.at[chunk])

    def fetch_unit(unit, c):
        unit_copy(ud_ref[j * (MOE_RCAP // MOE_UNIT) + unit], unit, unit // upc).start()
        return c

    lax.fori_loop(0, used_units, fetch_unit, 0)

    u = u_ref[...]
    hid = (_silu(_dot(u, wg_ref[...])) * _dot(u, wu_ref[...])).astype(BF16)
    o_ref[...] = _dot(hid, wd_ref[...])

    lkt = lkt_ref[0]

    def combine_chunk(rc, carry):
        def wait_unit(uu, c):
            unit_copy(0, 0, rc).wait()
            return c

        lax.fori_loop(0, jnp.clip(used_units - rc * upc, 0, upc), wait_unit, 0)
        r0 = pl.multiple_of(rc * MOE_RCH, MOE_RCH)
        row_id = (lax.broadcasted_iota(I32, (MOE_TILE, MOE_RCH), 1) + r0).astype(F32)
        o_ref[...] += _dot(_one_hot_rows(row_id, lkt, 1), ybuf_ref[pl.ds(r0, MOE_RCH), :])
        return carry

    lax.fori_loop(0, (used_units + upc - 1) // upc, combine_chunk, 0)
    h = ALPHA * x1_ref[...] + g2_ref[0] * o_ref[...]
    o_ref[...] = _layer_norm_rows(h, lg_ref[...], lb_ref[...])


def _final(u2, wg_s, wu_s, wd_s, ys, lkt, plan, x1, gate2, ln_g, ln_b, seq):
    t, d = x1.shape
    tm = MOE_TILE
    tpb = seq // tm
    f = wg_s.shape[1]
    once = pl.Buffered(1)
    rows = lambda: pl.BlockSpec((tm, d), lambda j, *_: (j, 0))
    full = lambda shape: pl.BlockSpec(shape, lambda j, *_: (0,) * len(shape), pipeline_mode=once)
    grid_spec = pltpu.PrefetchScalarGridSpec(
        num_scalar_prefetch=2,
        grid=(t // tm,),
        in_specs=[pl.BlockSpec((1, tm, LANES), lambda j, *_: (j, 0, 0)), rows(),
                  full((d, f)), full((d, f)), full((f, d)), rows(),
                  pl.BlockSpec((1, 1, d), lambda j, *_: (j // tpb, 0, 0)), full((1, d)), full((1, d)),
                  pl.BlockSpec(memory_space=pl.ANY)],
        out_specs=rows(),
        scratch_shapes=[pltpu.VMEM((MOE_RCAP, d), BF16), pltpu.SemaphoreType.DMA((MOE_RCAP // MOE_RCH,))])
    return pl.pallas_call(
        _final_kernel,
        grid_spec=grid_spec,
        out_shape=jax.ShapeDtypeStruct((t, d), F32),
        compiler_params=_cparams(("arbitrary",), 58),
        name="final",
    )(plan["unit_dst"], plan["used"], lkt, u2, wg_s, wu_s, wd_s, x1, gate2, ln_g[None, :], ln_b[None, :], ys)


def _split_hi_lo(w):
    hi = w.astype(BF16)
    lo = (w - hi.astype(F32)).astype(BF16)
    return hi, lo


def _rope_tables(positions, dim, reps):
    inv_freq = 1.0 / (ROPE_THETA ** (jnp.arange(0, dim, 2, dtype=F32) / dim))
    ang = positions.astype(F32).reshape(-1)[:, None] * inv_freq
    cos, sin = jnp.cos(ang), jnp.sin(ang)
    return (jnp.tile(jnp.concatenate([cos, cos], axis=1), (1, reps)),
            jnp.tile(jnp.concatenate([-sin, sin], axis=1), (1, reps)))


def _layer(x2, c, positions, bsz, seq, w_ada, b_ada, w_in, conv_w, conv_b, dt_bias, a_log, d_skip, ssd_norm_w,
           idx_k_ln_g, idx_k_ln_b, w_out, ln1_g, ln1_b, w_router, router_bias, w_gate_e, w_up_e, w_down_e,
           w_gate_s, w_up_s, w_down_s, ln2_g, ln2_b, tables):
    d = D_MODEL
    mod = _ada(c, w_ada, b_ada).reshape(bsz, 6, 1, d)
    shift1, scale1, gate1, shift2, scale2, gate2 = [mod[:, k] for k in range(6)]

    o_z, o_xbc, o_dt, o_q = 0, D_SSM, 2 * D_SSM + 2 * SSM_GROUPS * SSM_STATE, 2 * D_SSM + 2 * SSM_GROUPS * SSM_STATE + SSM_HEADS
    o_k, o_v, o_qi = o_q + D_ATTN, o_q + 2 * D_ATTN, o_q + 3 * D_ATTN
    o_ki = o_qi + IDX_HEADS * IDX_DIM
    o_wi = o_ki + IDX_DIM
    w_t = w_in.T
    w_main = jnp.concatenate([w_t[o_z:o_z + D_SSM], w_t[o_xbc:o_xbc + D_SSM], w_t[o_q:o_ki],
                              w_t[o_xbc + D_SSM:o_dt]], axis=0).astype(BF16)
    padr = lambda w: jnp.pad(w, ((0, LANES - w.shape[0]), (0, 0)))
    w_tail = jnp.concatenate([padr(w_t[o_dt:o_q]), padr(w_t[o_ki:o_wi]), padr(w_t[o_wi:])], axis=0)
    wt_hi, wt_lo = _split_hi_lo(w_tail)

    proj, tail = _inproj(x2, scale1, shift1, w_main, wt_hi, wt_lo, seq)
    y_ssm = _ssm(proj, tail, conv_w, conv_b, dt_bias, a_log, d_skip, ssd_norm_w, bsz, seq)
    q_r, k_r, v_t, qi_r, ki0, ki1, w_s = _prep(proj, tail, *tables, idx_k_ln_g, idx_k_ln_b)
    y_att = _attn(q_r, k_r, v_t, qi_r, ki0, ki1, w_s, bsz, seq)

    wo = w_out.astype(BF16)
    wr_hi, wr_lo = _split_hi_lo(w_router.T)
    rbias = jnp.broadcast_to(router_bias[:, None], (N_EXPERTS, LANES))
    x1, u2, gates_t = _outproj(y_ssm, y_att, wo[:D_SSM], wo[D_SSM:], x2, gate1, scale2, shift2, ln1_g, ln1_b,
                               wr_hi, wr_lo, rbias, seq)
    plan = _moe_plan(gates_t, x2.shape[0] // MOE_TILE)
    xs, lkt = _dispatch(u2, gates_t, plan)
    ys = _experts(xs, plan, w_gate_e, w_up_e, w_down_e)
    return _final(u2, w_gate_s.astype(BF16), w_up_s.astype(BF16), w_down_s.astype(BF16), ys, lkt, plan, x1, gate2,
                  ln2_g, ln2_b, seq)


def kernel(x, c, positions, w_ada, b_ada, w_in, conv_w, conv_b, dt_bias, a_log, d_skip, ssd_norm_w, idx_k_ln_g, idx_k_ln_b, w_out, ln1_g, ln1_b, w_router, router_bias, w_gate_e, w_up_e, w_down_e, w_gate_s, w_up_s, w_down_s, ln2_g, ln2_b):
    bsz, seq, d = x.shape
    tables = _rope_tables(positions, ATTN_HEAD_DIM, 1) + _rope_tables(positions, IDX_DIM, 2)
    x2 = x.reshape(bsz * seq, d)
    for l in range(w_ada.shape[0]):
        x2 = _layer(x2, c, positions, bsz, seq, w_ada[l], b_ada[l], w_in[l], conv_w[l], conv_b[l], dt_bias[l],
                    a_log[l], d_skip[l], ssd_norm_w[l], idx_k_ln_g[l], idx_k_ln_b[l], w_out[l], ln1_g[l], ln1_b[l],
                    w_router[l], router_bias[l], w_gate_e[l], w_up_e[l], w_down_e[l], w_gate_s[l], w_up_s[l],
                    w_down_s[l], ln2_g[l], ln2_b[l], tables)
    return x2.reshape(bsz, seq, d)
```

```python
import functools
import math

import jax
import jax.numpy as jnp
import numpy as np
from jax import lax
from jax.experimental import pallas as pl
from jax.experimental.pallas import tpu as pltpu

F32 = jnp.float32
BF16 = jnp.bfloat16
I32 = jnp.int32
HIGHEST = lax.Precision.HIGHEST

D_MODEL = 2048
D_SSM = 1024
D_ATTN = 1024
SSM_HEAD_DIM = 64
SSM_HEADS = 16
SSM_GROUPS = 2
SSM_STATE = 128
CONV_WIDTH = 4
SSD_CHUNK = 128
ATTN_HEAD_DIM = 128
ATTN_HEADS = 8
IDX_HEADS = 16
IDX_DIM = 64
INDEX_TOPK = 256
ROPE_THETA = 10000.0
N_EXPERTS = 64
N_EXPERT_GROUPS = 8
EXPERTS_PER_GROUP = 8
TOPK_GROUPS = 4
TOP_K = 8
D_EXPERT = 512
D_SHARED = 512
ROUTED_SCALE = 2.5
DEPTH = 1
ALPHA = (2.0 * DEPTH) ** 0.25
LN_EPS = 1e-5

LANES = 128
SUBLANES = 8
VMEM_BYTES_V7X = 64 * 1024 * 1024
INT_MIN = -(2 ** 31)
KEY_LOWEST_FINITE = INT_MIN + 0x00800000

COL_Z, COL_XS, COL_Q, COL_K, COL_V, COL_QI, COL_BC = 0, 1024, 2048, 3072, 4096, 5120, 6144
N_MAIN = 6656
N_TAIL = 3 * LANES


def _cparams(sem, vmem_mb):
    return pltpu.CompilerParams(dimension_semantics=sem, vmem_limit_bytes=vmem_mb * 1024 * 1024)


def _silu(v):
    return v * (1.0 / (1.0 + jnp.exp(-v)))


def _dot(a, b, precision=None):
    return jnp.dot(a, b, preferred_element_type=F32, precision=precision)


def _dot_nt(a, b, precision=None):
    return lax.dot_general(a, b, (((1,), (1,)), ((), ())), preferred_element_type=F32, precision=precision)


def _split3(x):
    hi = x.astype(BF16)
    r1 = x - hi.astype(F32)
    mid = r1.astype(BF16)
    return hi, mid, (r1 - mid.astype(F32)).astype(BF16)


def _layer_norm_rows(h, g, b):
    mu = jnp.mean(h, axis=-1, keepdims=True)
    d = h - mu
    var = jnp.mean(d * d, axis=-1, keepdims=True)
    return d * lax.rsqrt(var + LN_EPS) * g + b


def _ada_kernel(cb_ref, w_ref, b_ref, o_ref):
    tn = w_ref.shape[1]
    for bi in range(cb_ref.shape[0]):
        cb = _silu(cb_ref[bi])
        cols = [jnp.sum(w_ref[:, j * LANES:(j + 1) * LANES] * cb, axis=0, keepdims=True)
                for j in range(tn // LANES)]
        o_ref[bi:bi + 1, :] = jnp.concatenate(cols, axis=1) + b_ref[...]


def _ada(c, w_ada, b_ada):
    bsz, d = c.shape
    n = w_ada.shape[1]
    tn = 1024
    cb = jnp.broadcast_to(c[:, :, None], (bsz, d, LANES))
    return pl.pallas_call(
        _ada_kernel,
        grid=(n // tn,),
        in_specs=[pl.BlockSpec((bsz, d, LANES), lambda j: (0, 0, 0)),
                  pl.BlockSpec((d, tn), lambda j: (0, j)),
                  pl.BlockSpec((1, tn), lambda j: (0, j))],
        out_specs=pl.BlockSpec((bsz, tn), lambda j: (0, j)),
        out_shape=jax.ShapeDtypeStruct((bsz, n), F32),
        compiler_params=_cparams(("parallel",), 40),
        name="ada",
    )(cb, w_ada, b_ada.reshape(1, n))


def _inproj_kernel(x_ref, sc_ref, sh_ref, w_ref, wth_ref, wtl_ref, o_ref, t_ref, u_ref):
    @pl.when(pl.program_id(1) == 0)
    def _():
        u = x_ref[...] * (1.0 + sc_ref[0]) + sh_ref[0]
        uh = u.astype(BF16)
        ul = (u - uh.astype(F32)).astype(BF16)
        u_ref[...] = uh
        t_ref[...] = (_dot_nt(uh, wth_ref[...]) + _dot_nt(uh, wtl_ref[...]) + _dot_nt(ul, wth_ref[...]))

    o_ref[...] = _dot_nt(u_ref[...], w_ref[...])


def _inproj(x2, scale1, shift1, w_main, wt_hi, wt_lo, seq):
    t, d = x2.shape
    tm, tn = 1024, 512
    tpb = seq // tm
    return pl.pallas_call(
        _inproj_kernel,
        grid=(t // tm, N_MAIN // tn),
        in_specs=[pl.BlockSpec((tm, d), lambda i, j: (i, 0)),
                  pl.BlockSpec((1, 1, d), lambda i, j: (i // tpb, 0, 0)),
                  pl.BlockSpec((1, 1, d), lambda i, j: (i // tpb, 0, 0)),
                  pl.BlockSpec((tn, d), lambda i, j: (j, 0)),
                  pl.BlockSpec((N_TAIL, d), lambda i, j: (0, 0)),
                  pl.BlockSpec((N_TAIL, d), lambda i, j: (0, 0))],
        out_specs=[pl.BlockSpec((tm, tn), lambda i, j: (i, j)),
                   pl.BlockSpec((tm, N_TAIL), lambda i, j: (i, 0))],
        out_shape=[jax.ShapeDtypeStruct((t, N_MAIN), F32),
                   jax.ShapeDtypeStruct((t, N_TAIL), F32)],
        scratch_shapes=[pltpu.VMEM((tm, d), BF16)],
        compiler_params=_cparams(("parallel", "arbitrary"), 48),
        name="inproj",
    )(x2, scale1, shift1, w_main, wt_hi, wt_lo)


def _ssm_kernel(z_ref, xs_ref, bc_ref, dt_ref, cwx_ref, cbx_ref, cwb_ref, cbb_ref, dtb_ref, alog_ref,
                dsk_ref, nw_ref, e_ref, e2_ref, tril_ref, o_ref, px_ref, pb_ref, st_ref):
    q = SSD_CHUNK
    hpg = SSM_HEADS // SSM_GROUPS
    gw = hpg * SSM_HEAD_DIM

    @pl.when(pl.program_id(1) == 0)
    def _():
        px_ref[0:SUBLANES, :] = jnp.zeros((SUBLANES, px_ref.shape[1]), F32)
        pb_ref[0:SUBLANES, :] = jnp.zeros((SUBLANES, pb_ref.shape[1]), F32)
        st_ref[...] = jnp.zeros(st_ref.shape, F32)

    def conv_silu(raw_ref, pad_ref, w_ref, b_ref):
        pad_ref[SUBLANES:SUBLANES + q, :] = raw_ref[...]
        acc = b_ref[...] + w_ref[0:1, :] * pad_ref[SUBLANES - 3:SUBLANES - 3 + q, :]
        for k in range(1, CONV_WIDTH):
            acc = acc + w_ref[k:k + 1, :] * pad_ref[SUBLANES - 3 + k:SUBLANES - 3 + k + q, :]
        pad_ref[0:SUBLANES, :] = raw_ref[q - SUBLANES:q, :]
        return _silu(acc)

    xs = conv_silu(xs_ref, px_ref, cwx_ref, cbx_ref)
    bc = conv_silu(bc_ref, pb_ref, cwb_ref, cbb_ref)

    dtr = dt_ref[...] + dtb_ref[...]
    dt = jnp.maximum(dtr, 0.0) + jnp.log(1.0 + jnp.exp(-jnp.abs(dtr)))
    log_a = dt * (-jnp.exp(alog_ref[...]))
    cs = sum(_dot(tril_ref[...], part) for part in _split3(log_a))
    cs_parts = _split3(cs)
    cs_e = sum(_dot(part, e_ref[...]) for part in cs_parts)
    dt_e = sum(_dot(part, e_ref[...]) for part in _split3(dt))
    cs_col = sum(_dot(part, e2_ref[...]) for part in cs_parts)
    cs_t = cs.T
    cs_last = cs_e[q - 1:q, :]

    xdt = xs * dt_e
    rows = lax.broadcasted_iota(I32, (q, q), 0)
    cols = lax.broadcasted_iota(I32, (q, q), 1)
    causal = rows >= cols
    first_half = lax.broadcasted_iota(I32, (q, LANES), 1) < SSM_HEAD_DIM

    y_pairs = []
    for g in range(SSM_GROUPS):
        b_g = bc[:, g * SSM_STATE:(g + 1) * SSM_STATE]
        c_g = bc[:, (SSM_GROUPS + g) * SSM_STATE:(SSM_GROUPS + g + 1) * SSM_STATE]
        cb = _dot_nt(c_g.astype(BF16), b_g.astype(BF16))
        for hp in range(hpg // 2):
            pair = g * (hpg // 2) + hp
            x_pair = xdt[:, pair * LANES:(pair + 1) * LANES].astype(BF16)
            ys = []
            for sub in range(2):
                h = 2 * pair + sub
                seg = cs_col[:, h * LANES:(h + 1) * LANES] - cs_t[h:h + 1, :]
                dec = jnp.exp(jnp.where(causal, seg, -jnp.inf))
                ys.append(_dot((cb * dec).astype(BF16), x_pair))
            y_pairs.append(jnp.where(first_half, ys[0], ys[1]))
    y_diag = jnp.concatenate(y_pairs, axis=1)

    xw = xdt * jnp.exp(cs_last - cs_e)
    y_off, new_states = [], []
    for g in range(SSM_GROUPS):
        b_g = bc[:, g * SSM_STATE:(g + 1) * SSM_STATE]
        c_g = bc[:, (SSM_GROUPS + g) * SSM_STATE:(SSM_GROUPS + g + 1) * SSM_STATE]
        h_in = st_ref[:, g * gw:(g + 1) * gw]
        y_off.append(_dot(c_g.astype(BF16), h_in.astype(BF16)))
        new_states.append(_dot(b_g.T.astype(BF16), xw[:, g * gw:(g + 1) * gw].astype(BF16)))
    y_off = jnp.concatenate(y_off, axis=1) * jnp.exp(cs_e)
    st_ref[...] = jnp.exp(cs_last) * st_ref[...] + jnp.concatenate(new_states, axis=1)

    y = y_diag + y_off + dsk_ref[...] * xs
    yf = y * _silu(z_ref[...])
    ms = jnp.mean(yf * yf, axis=-1, keepdims=True)
    o_ref[...] = (yf * lax.rsqrt(ms + LN_EPS) * nw_ref[...]).astype(o_ref.dtype)


def _ssm(proj, tail, conv_w, conv_b, dt_bias, a_log, d_skip, ssd_norm_w, bsz, seq):
    t = proj.shape[0]
    q = SSD_CHUNK
    n_c = seq // q
    nbc = 2 * SSM_GROUPS * SSM_STATE
    cw_x, cw_b = conv_w[:, :D_SSM], conv_w[:, D_SSM:]
    cb_x, cb_b = conv_b[None, :D_SSM], conv_b[None, D_SSM:]
    pad16 = lambda v: jnp.pad(v, (0, LANES - SSM_HEADS))[None, :]
    head_of_lane = np.arange(D_SSM) // SSM_HEAD_DIM
    e_mat = jnp.asarray((np.arange(LANES)[:, None] == head_of_lane[None, :]).astype(np.float32), BF16)
    e2_mat = jnp.asarray((np.arange(LANES)[:, None] == (np.arange(SSM_HEADS * LANES) // LANES)[None, :])
                         .astype(np.float32), BF16)
    tril = jnp.asarray(np.tril(np.ones((q, q), np.float32)), BF16)
    row = lambda b, c: b * n_c + c
    full = lambda shape: pl.BlockSpec(shape, lambda b, c: (0,) * len(shape))
    return pl.pallas_call(
        _ssm_kernel,
        grid=(bsz, n_c),
        in_specs=[pl.BlockSpec((q, D_SSM), lambda b, c: (row(b, c), COL_Z // D_SSM)),
                  pl.BlockSpec((q, D_SSM), lambda b, c: (row(b, c), COL_XS // D_SSM)),
                  pl.BlockSpec((q, nbc), lambda b, c: (row(b, c), COL_BC // nbc)),
                  pl.BlockSpec((q, LANES), lambda b, c: (row(b, c), 0)),
                  full((CONV_WIDTH, D_SSM)), full((1, D_SSM)), full((CONV_WIDTH, nbc)), full((1, nbc)),
                  full((1, LANES)), full((1, LANES)), full((1, D_SSM)), full((1, D_SSM)),
                  full((LANES, D_SSM)), full((LANES, SSM_HEADS * LANES)), full((q, q))],
        out_specs=pl.BlockSpec((q, D_SSM), lambda b, c: (row(b, c), 0)),
        out_shape=jax.ShapeDtypeStruct((t, D_SSM), BF16),
        scratch_shapes=[pltpu.VMEM((SUBLANES + q, D_SSM), F32),
                        pltpu.VMEM((SUBLANES + q, nbc), F32),
                        pltpu.VMEM((SSM_STATE, D_SSM), F32)],
        compiler_params=_cparams(("parallel", "arbitrary"), 40),
        name="ssm",
    )(proj, proj, proj, tail, cw_x, cb_x, cw_b, cb_b, pad16(dt_bias), pad16(a_log),
      jnp.repeat(d_skip, SSM_HEAD_DIM)[None, :], ssd_norm_w[None, :], e_mat, e2_mat, tril)


def _prep_kernel(q_ref, k_ref, v_ref, qi_ref, ki_ref, wi_ref, ca_ref, sa_ref, ci_ref, si_ref, lg_ref, lb_ref,
                 qo_ref, ko_ref, vo_ref, qio_ref, ki0_ref, ki1_ref, wo_ref):
    ca, sa, ci, si = ca_ref[...], sa_ref[...], ci_ref[...], si_ref[...]
    lane = lax.broadcasted_iota(I32, ca.shape, 1)
    first32 = (lane % IDX_DIM) < (IDX_DIM // 2)
    q_scale = ATTN_HEAD_DIM ** -0.5 * math.log2(math.e)

    def rope_attn(v):
        return v * ca + pltpu.roll(v, ATTN_HEAD_DIM // 2, 1) * sa

    def rope_idx(v):
        rot = jnp.where(first32, pltpu.roll(v, LANES - IDX_DIM // 2, 1), pltpu.roll(v, IDX_DIM // 2, 1))
        return v * ci + rot * si

    for h in range(ATTN_HEADS):
        sl = slice(h * LANES, (h + 1) * LANES)
        qo_ref[:, sl] = (rope_attn(q_ref[:, sl]) * q_scale).astype(BF16)
        ko_ref[:, sl] = rope_attn(k_ref[:, sl]).astype(BF16)
    vo_ref[0] = v_ref[...].T.astype(BF16)
    for p in range(IDX_HEADS * IDX_DIM // LANES):
        sl = slice(p * LANES, (p + 1) * LANES)
        qio_ref[:, sl] = rope_idx(qi_ref[:, sl]).astype(BF16)

    kraw = ki_ref[...]
    valid = lane < IDX_DIM
    mu = jnp.sum(kraw, axis=-1, keepdims=True) * (1.0 / IDX_DIM)
    dk = jnp.where(valid, kraw - mu, 0.0)
    var = jnp.sum(dk * dk, axis=-1, keepdims=True) * (1.0 / IDX_DIM)
    kn = jnp.where(valid, dk * lax.rsqrt(var + LN_EPS) * lg_ref[...] + lb_ref[...], 0.0)
    kr = jnp.where(valid, rope_idx(kn), 0.0)
    ki0_ref[...] = kr.astype(BF16)
    ki1_ref[...] = pltpu.roll(kr, IDX_DIM, 1).astype(BF16)
    wo_ref[...] = wi_ref[...] * (IDX_HEADS ** -0.5 * IDX_DIM ** -0.5)


def _prep(proj, tail, cos_a, sin_a, cos_i, sin_i, ln_g, ln_b):
    t = proj.shape[0]
    tm = ATT_KC
    w = D_ATTN
    pad64 = lambda v: jnp.pad(v, (0, LANES - IDX_DIM))[None, :]
    col = lambda c: pl.BlockSpec((tm, w), lambda i: (i, c // w))
    lane_blk = lambda c: pl.BlockSpec((tm, LANES), lambda i: (i, c))
    full = lambda: pl.BlockSpec((1, LANES), lambda i: (0, 0))
    return pl.pallas_call(
        _prep_kernel,
        grid=(t // tm,),
        in_specs=[col(COL_Q), col(COL_K), col(COL_V), col(COL_QI), lane_blk(1), lane_blk(2),
                  lane_blk(0), lane_blk(0), lane_blk(0), lane_blk(0), full(), full()],
        out_specs=[pl.BlockSpec((tm, w), lambda i: (i, 0))] * 2 + [pl.BlockSpec((1, w, tm), lambda i: (i, 0, 0))]
        + [pl.BlockSpec((tm, w), lambda i: (i, 0))] + [pl.BlockSpec((tm, LANES), lambda i: (i, 0))] * 3,
        out_shape=[jax.ShapeDtypeStruct((t, w), BF16)] * 2 + [jax.ShapeDtypeStruct((t // tm, w, tm), BF16)]
        + [jax.ShapeDtypeStruct((t, w), BF16)]
        + [jax.ShapeDtypeStruct((t, LANES), BF16)] * 2 + [jax.ShapeDtypeStruct((t, LANES), F32)],
        compiler_params=_cparams(("parallel",), 48),
        name="prep",
    )(proj, proj, proj, proj, tail, tail, cos_a, sin_a, cos_i, sin_i, pad64(ln_g), pad64(ln_b))


ATT_QB = 128
ATT_KC = 512
ATT_KS = 256
SEARCH_CHECK = 4


def _attn_kernel(q_ref, k_ref, vt_ref, qi_ref, ki0_ref, ki1_ref, w_ref, o_ref, sc_ref, acc_ref, *, topk):
    i = pl.program_id(1)
    n_ch = (i * ATT_QB + ATT_QB + ATT_KC - 1) // ATT_KC
    key_off = lax.broadcasted_iota(I32, (ATT_KC, ATT_QB), 0)
    q_pos = i * ATT_QB + lax.broadcasted_iota(I32, (ATT_KC, ATT_QB), 1)
    w_t = w_ref[...].T

    def score_chunk(c, carry):
        start = pl.multiple_of(c * ATT_KC, ATT_KC)
        ki0 = ki0_ref[pl.ds(start, ATT_KC), :]
        ki1 = ki1_ref[pl.ds(start, ATT_KC), :]
        acc = jnp.zeros((ATT_KC, ATT_QB), F32)
        for h in range(IDX_HEADS):
            pair = qi_ref[:, (h // 2) * LANES:(h // 2 + 1) * LANES]
            rel = _dot_nt(ki0 if h % 2 == 0 else ki1, pair)
            acc = acc + jnp.maximum(rel, 0.0) * w_t[h:h + 1, :]
        sc_ref[c] = jnp.where(start + key_off <= q_pos, acc, -jnp.inf)
        return carry

    lax.fori_loop(0, n_ch, score_chunk, 0)

    def key_to_float(key):
        return pltpu.bitcast(jnp.where(key < 0, key ^ jnp.int32(0x7FFFFFFF), key), F32)

    def search_bit(state):
        it, ans, cnt_ans, _ = state
        cand = ans ^ lax.shift_left(jnp.int32(1), jnp.int32(31) - it)
        cand_f = key_to_float(cand)

        def count_chunk(c, cnt):
            part = jnp.where(sc_ref[c] >= cand_f, 1, 0).astype(I32)
            return cnt + jnp.sum(part.reshape(ATT_KC // SUBLANES, SUBLANES, ATT_QB), axis=0)

        cnt = lax.fori_loop(0, n_ch, count_chunk, jnp.zeros((SUBLANES, ATT_QB), I32))
        total = jnp.sum(cnt, axis=0, keepdims=True)
        accept = total >= topk
        ans = jnp.where(accept, cand, ans)
        cnt_ans = jnp.where(accept, total, cnt_ans)
        settled = lax.cond(it % SEARCH_CHECK == SEARCH_CHECK - 1,
                           lambda: jnp.min(jnp.where(cnt_ans == topk, 1, 0)).astype(I32), lambda: jnp.int32(0))
        return it + 1, ans, cnt_ans, settled

    first_bit = jnp.where((i + 1) * ATT_QB <= topk, 32, 0).astype(I32)
    init = (first_bit, jnp.full((1, ATT_QB), INT_MIN, I32), jnp.zeros((1, ATT_QB), I32), jnp.int32(0))
    _, kth, _, _ = lax.while_loop(lambda st: jnp.logical_and(st[0] < 32, st[3] == 0), search_bit, init)
    lowest_finite = jnp.float32(np.finfo(np.float32).min)
    thr = jnp.where(kth < KEY_LOWEST_FINITE, lowest_finite, key_to_float(kth))

    acc_ref[...] = jnp.zeros(acc_ref.shape, F32)

    def att_chunk(c, carry):
        m_all, l_all = carry
        start = pl.multiple_of(c * ATT_KC, ATT_KC)
        for part in range(ATT_KC // ATT_KS):
            ks = slice(part * ATT_KS, (part + 1) * ATT_KS)
            sel = sc_ref[c, ks, :] >= thr
            m_rows, l_rows = [], []
            for h in range(ATTN_HEADS):
                sl = slice(h * LANES, (h + 1) * LANES)
                s = _dot_nt(k_ref[pl.ds(start + part * ATT_KS, ATT_KS), sl], q_ref[:, sl])
                s = jnp.where(sel, s, -jnp.inf)
                m_old = m_all[h:h + 1, :]
                m_new = jnp.maximum(m_old, jnp.max(s, axis=0, keepdims=True))
                p = jnp.exp2(s - m_new)
                alpha = jnp.exp2(m_old - m_new)
                l_rows.append(alpha * l_all[h:h + 1, :] + jnp.sum(p, axis=0, keepdims=True))
                m_rows.append(m_new)
                acc_ref[h] = alpha * acc_ref[h] + _dot(vt_ref[c, sl, ks], p.astype(BF16))
            m_all = jnp.concatenate(m_rows, axis=0)
            l_all = jnp.concatenate(l_rows, axis=0)
        return m_all, l_all

    init = (jnp.full((ATTN_HEADS, ATT_QB), -1e30, F32), jnp.zeros((ATTN_HEADS, ATT_QB), F32))
    _, l_all = lax.fori_loop(0, n_ch, att_chunk, init)
    for h in range(ATTN_HEADS):
        o_ref[:, h * LANES:(h + 1) * LANES] = (acc_ref[h] / l_all[h:h + 1, :]).T.astype(o_ref.dtype)


def _attn(q_r, k_r, v_t, qi_r, ki0, ki1, w_s, bsz, seq):
    t = q_r.shape[0]
    nq = seq // ATT_QB
    n_kc = seq // ATT_KC
    topk = min(INDEX_TOPK, seq // 4)
    once = pl.Buffered(1)
    qblk = lambda width: pl.BlockSpec((ATT_QB, width), lambda b, i: (b * nq + i, 0))
    per_batch = lambda width: pl.BlockSpec((seq, width), lambda b, i: (b, 0), pipeline_mode=once)
    return pl.pallas_call(
        functools.partial(_attn_kernel, topk=topk),
        grid=(bsz, nq),
        in_specs=[qblk(D_ATTN), per_batch(D_ATTN),
                  pl.BlockSpec((n_kc, D_ATTN, ATT_KC), lambda b, i: (b, 0, 0), pipeline_mode=once),
                  qblk(IDX_HEADS * IDX_DIM), per_batch(LANES), per_batch(LANES), qblk(LANES)],
        out_specs=qblk(D_ATTN),
        out_shape=jax.ShapeDtypeStruct((t, D_ATTN), BF16),
        scratch_shapes=[pltpu.VMEM((n_kc, ATT_KC, ATT_QB), F32),
                        pltpu.VMEM((ATTN_HEADS, ATTN_HEAD_DIM, ATT_QB), F32)],
        compiler_params=_cparams(("parallel", "arbitrary"), 48),
        name="attn",
    )(q_r, k_r, v_t, qi_r, ki0, ki1, w_s)


def _first_index_of_max(v, iota, n):
    m = jnp.max(v, axis=0, keepdims=True)
    idx = jnp.min(jnp.where(v == m, iota, n), axis=0, keepdims=True)
    return m, idx


def _route(scores, biased):
    tm = scores.shape[1]
    neg_inf = jnp.float32(-jnp.inf)
    iota8 = lax.broadcasted_iota(I32, (EXPERTS_PER_GROUP, tm), 0)
    group_scores = []
    for g in range(N_EXPERT_GROUPS):
        v = biased[g * EXPERTS_PER_GROUP:(g + 1) * EXPERTS_PER_GROUP, :]
        m1, i1 = _first_index_of_max(v, iota8, EXPERTS_PER_GROUP)
        m2 = jnp.max(jnp.where(iota8 == i1, neg_inf, v), axis=0, keepdims=True)
        group_scores.append(m1 + m2)
    gs = jnp.concatenate(group_scores, axis=0)
    keep = jnp.zeros(gs.shape, jnp.bool_)
    for _ in range(TOPK_GROUPS):
        _, gi = _first_index_of_max(gs, iota8, N_EXPERT_GROUPS)
        hit = iota8 == gi
        keep = jnp.logical_or(keep, hit)
        gs = jnp.where(hit, neg_inf, gs)
    keep_f = jnp.where(keep, 1.0, 0.0)
    masked = jnp.concatenate(
        [jnp.where(keep_f[g:g + 1, :] > 0.0, biased[g * EXPERTS_PER_GROUP:(g + 1) * EXPERTS_PER_GROUP, :], neg_inf)
         for g in range(N_EXPERT_GROUPS)], axis=0)
    iota64 = lax.broadcasted_iota(I32, (N_EXPERTS, tm), 0)
    sel_w = jnp.zeros((N_EXPERTS, tm), F32)
    for _ in range(TOP_K):
        _, ei = _first_index_of_max(masked, iota64, N_EXPERTS)
        hit = iota64 == ei
        sel_w = jnp.where(hit, scores, sel_w)
        masked = jnp.where(hit, neg_inf, masked)
    denom = jnp.sum(sel_w, axis=0, keepdims=True)
    return sel_w / denom * ROUTED_SCALE


def _outproj_kernel(ys_ref, ya_ref, wa_ref, wb_ref, x_ref, g1_ref, sc2_ref, sh2_ref, lg_ref, lb_ref,
                    wrh_ref, wrl_ref, rb_ref, x1_ref, u2_ref, gt_ref):
    mix = _dot(ys_ref[...], wa_ref[...]) + _dot(ya_ref[...], wb_ref[...])
    h = ALPHA * x_ref[...] + g1_ref[0] * mix
    x1 = _layer_norm_rows(h, lg_ref[...], lb_ref[...])
    x1_ref[...] = x1
    u2 = x1 * (1.0 + sc2_ref[0]) + sh2_ref[0]
    uh = u2.astype(BF16)
    ul = (u2 - uh.astype(F32)).astype(BF16)
    u2_ref[...] = uh
    wrh = wrh_ref[...]
    logits = _dot_nt(wrh, uh) + _dot_nt(wrh, ul) + _dot_nt(wrl_ref[...], uh)
    scores = 1.0 / (1.0 + jnp.exp(-logits))
    gt_ref[...] = _route(scores, scores + rb_ref[:, 0:1])


def _outproj(y_ssm, y_att, wo_a, wo_b, x2, gate1, scale2, shift2, ln_g, ln_b, wr_hi, wr_lo, rbias, seq):
    t, d = x2.shape
    tm = 512
    tpb = seq // tm
    half = y_ssm.shape[1]
    rows = lambda width: pl.BlockSpec((tm, width), lambda i: (i, 0))
    full = lambda shape: pl.BlockSpec(shape, lambda i: (0,) * len(shape))
    mod = lambda: pl.BlockSpec((1, 1, d), lambda i: (i // tpb, 0, 0))
    return pl.pallas_call(
        _outproj_kernel,
        grid=(t // tm,),
        in_specs=[rows(half), rows(half), full((half, d)), full((half, d)), rows(d), mod(), mod(), mod(),
                  full((1, d)), full((1, d)), full((N_EXPERTS, d)), full((N_EXPERTS, d)), full((N_EXPERTS, LANES))],
        out_specs=[rows(d), rows(d), pl.BlockSpec((N_EXPERTS, tm), lambda i: (0, i))],
        out_shape=[jax.ShapeDtypeStruct((t, d), F32), jax.ShapeDtypeStruct((t, d), BF16),
                   jax.ShapeDtypeStruct((N_EXPERTS, t), F32)],
        compiler_params=_cparams(("parallel",), 48),
        name="outproj",
    )(y_ssm, y_att, wo_a, wo_b, x2, gate1, scale2, shift2, ln_g[None, :], ln_b[None, :], wr_hi, wr_lo, rbias)


MOE_TILE = 256
MOE_UNIT = 16
MOE_BM = 512
MOE_SUB = 128
MOE_RCAP = TOP_K * MOE_TILE + N_EXPERTS * MOE_UNIT
MOE_RCH = 512
MOE_GW = 3 * LANES
MOE_UNSET = 1e9


def _moe_plan(gates_t, n_tiles):
    upb = MOE_BM // MOE_UNIT
    ups = MOE_SUB // MOE_UNIT
    cnt = jnp.sum((gates_t > 0.0).reshape(N_EXPERTS, n_tiles, MOE_TILE), axis=2, dtype=I32).T
    nun = (cnt + MOE_UNIT - 1) // MOE_UNIT
    lend = jnp.cumsum(nun, axis=1)
    loff = lend - nun
    tot = jnp.sum(nun, axis=0)
    tot_sub = (tot + ups - 1) // ups * ups
    nb = (tot + upb - 1) // upb
    cum_nb = jnp.cumsum(nb)
    first_blk = cum_nb - nb
    base = first_blk * upb
    goff = base[None, :] + jnp.cumsum(nun, axis=0) - nun
    unit = jnp.arange(MOE_RCAP // MOE_UNIT, dtype=I32)
    shift = goff - loff
    step_up = jnp.concatenate([shift[:, :1], shift[:, 1:] - shift[:, :-1]], axis=1)
    started = jnp.concatenate([jnp.ones((n_tiles, unit.shape[0], 1), jnp.bool_),
                               lend[:, None, :-1] <= unit[None, :, None]], axis=2)
    unit_dst = unit[None, :] + jnp.sum(jnp.where(started, step_up[:, None, :], 0), axis=2)
    n_blocks = (TOP_K * n_tiles * MOE_TILE // MOE_UNIT + n_tiles * N_EXPERTS + N_EXPERTS * (upb - 1) + upb - 1) // upb
    step = jnp.arange(n_blocks, dtype=I32)
    step_e = jnp.minimum(jnp.sum((cum_nb[None, :] <= step[:, None]).astype(I32), axis=1), N_EXPERTS - 1)
    of_step = step_e[:, None] == jnp.arange(N_EXPERTS, dtype=I32)[None, :]
    pick = lambda per_expert: jnp.sum(jnp.where(of_step, per_expert[None, :], 0), axis=1)
    nb_e = jnp.maximum(pick(nb), 1)
    first_e = pick(first_blk)
    local = (step - first_e + nb_e - 1) % nb_e
    step_rows = jnp.clip(pick(tot_sub) - local * upb, 0, upb) * MOE_UNIT
    end = pick(cum_nb)
    next_e = jnp.sum((cum_nb[None, :] <= end[:, None]).astype(I32), axis=1)
    step_next_e = jnp.where(end < cum_nb[-1], jnp.minimum(next_e, N_EXPERTS - 1), -1)
    loff_v = jnp.broadcast_to((loff * MOE_UNIT).astype(F32)[:, :, None], (n_tiles, N_EXPERTS, LANES))
    flat = lambda v: v.reshape(-1).astype(I32)
    return dict(unit_dst=flat(unit_dst), used=flat(lend[:, -1]), padstart=flat(base + tot), padn=flat(tot_sub - tot),
                step_e=flat(step_e), step_blk=flat(first_e + local), step_rows=flat(step_rows),
                step_next_e=flat(step_next_e),
                nb_used=cum_nb[-1:].astype(I32), loff_v=loff_v, n_blocks=n_blocks)


def _unit_rows(unit):
    return pl.ds(pl.multiple_of(unit * MOE_UNIT, MOE_UNIT), MOE_UNIT)


def _one_hot_rows(row_id, targets, axis):
    p = jnp.zeros(row_id.shape, F32)
    for k in range(TOP_K):
        tgt = targets[k:k + 1, :] if axis == 0 else targets[:, k:k + 1]
        p = jnp.where(row_id == tgt, 1.0, p)
    return p.astype(BF16)


def _dispatch_kernel(ud_ref, used_ref, pst_ref, pnn_ref, u_ref, g_ref, lv_ref, tri_ref,
                     xs_ref, lkt_ref, sorted_ref, xaug_ref, zero_ref, sem):
    j = pl.program_id(0)
    g = g_ref[...]
    sel = g > 0.0
    rank = _dot(jnp.where(sel, 1.0, 0.0).astype(BF16), tri_ref[...])
    loff = jnp.concatenate([lv_ref[0]] * (MOE_TILE // LANES), axis=1)
    rem = jnp.where(sel, loff + rank, MOE_UNSET)
    rows = []
    for _ in range(TOP_K):
        cur = jnp.min(rem, axis=0, keepdims=True)
        rows.append(cur)
        rem = jnp.where(rem == cur, MOE_UNSET, rem)
    lk = jnp.concatenate(rows, axis=0)
    lkt_ref[0] = jnp.concatenate([lk, jnp.full((LANES - TOP_K, MOE_TILE), MOE_UNSET, F32)], axis=0).T

    gt = jnp.concatenate([g, jnp.zeros_like(g)], axis=0).T
    hi = gt.astype(BF16)
    r1 = gt - hi.astype(F32)
    mid = r1.astype(BF16)
    xaug_ref[:, :D_MODEL] = u_ref[...]
    xaug_ref[:, D_MODEL:D_MODEL + LANES] = hi
    xaug_ref[:, D_MODEL + LANES:D_MODEL + 2 * LANES] = mid
    xaug_ref[:, D_MODEL + 2 * LANES:] = (r1 - mid.astype(F32)).astype(BF16)

    used_units = used_ref[j]
    upc = MOE_RCH // MOE_UNIT

    def unit_copy(src_unit, dst_unit):
        return pltpu.make_async_copy(sorted_ref.at[_unit_rows(src_unit)], xs_ref.at[_unit_rows(dst_unit)], sem)

    def sort_chunk(rc, carry):
        r0 = pl.multiple_of(rc * MOE_RCH, MOE_RCH)
        row_id = (lax.broadcasted_iota(I32, (MOE_RCH, MOE_TILE), 0) + r0).astype(F32)
        sorted_ref[pl.ds(r0, MOE_RCH), :] = _dot(_one_hot_rows(row_id, lk, 0), xaug_ref[...]).astype(BF16)

        def send_unit(uu, c):
            unit = rc * upc + uu
            unit_copy(unit, ud_ref[j * (MOE_RCAP // MOE_UNIT) + unit]).start()
            return c

        lax.fori_loop(0, jnp.clip(used_units - rc * upc, 0, upc), send_unit, 0)
        return carry

    lax.fori_loop(0, (used_units + upc - 1) // upc, sort_chunk, 0)

    def drain_unit(uu, c):
        unit_copy(0, 0).wait()
        return c

    lax.fori_loop(0, used_units, drain_unit, 0)

    @pl.when(j == pl.num_programs(0) - 1)
    def _():
        zero_ref[...] = jnp.zeros(zero_ref.shape, BF16)

        def pad_expert(e, carry):
            def pad_copy(uu):
                return pltpu.make_async_copy(zero_ref, xs_ref.at[_unit_rows(pst_ref[e] + uu)], sem)

            def start(uu, c):
                pad_copy(uu).start()
                return c

            def wait(uu, c):
                pad_copy(uu).wait()
                return c

            lax.fori_loop(0, pnn_ref[e], start, 0)
            lax.fori_loop(0, pnn_ref[e], wait, 0)
            return carry

        lax.fori_loop(0, N_EXPERTS, pad_expert, 0)


def _dispatch(u2, gates_t, plan):
    t, d = u2.shape
    n_tiles = t // MOE_TILE
    xw = d + MOE_GW
    tri = jnp.asarray(np.triu(np.ones((MOE_TILE, MOE_TILE), np.float32), k=1), BF16)
    grid_spec = pltpu.PrefetchScalarGridSpec(
        num_scalar_prefetch=4,
        grid=(n_tiles,),
        in_specs=[pl.BlockSpec((MOE_TILE, d), lambda j, *_: (j, 0)),
                  pl.BlockSpec((N_EXPERTS, MOE_TILE), lambda j, *_: (0, j)),
                  pl.BlockSpec((1, N_EXPERTS, LANES), lambda j, *_: (j, 0, 0)),
                  pl.BlockSpec((MOE_TILE, MOE_TILE), lambda j, *_: (0, 0))],
        out_specs=[pl.BlockSpec(memory_space=pl.ANY),
                   pl.BlockSpec((1, MOE_TILE, LANES), lambda j, *_: (j, 0, 0))],
        scratch_shapes=[pltpu.VMEM((MOE_RCAP, xw), BF16), pltpu.VMEM((MOE_TILE, xw), BF16),
                        pltpu.VMEM((MOE_UNIT, xw), BF16), pltpu.SemaphoreType.DMA(())])
    return pl.pallas_call(
        _dispatch_kernel,
        grid_spec=grid_spec,
        out_shape=[jax.ShapeDtypeStruct((plan["n_blocks"] * MOE_BM, xw), BF16),
                   jax.ShapeDtypeStruct((n_tiles, MOE_TILE, LANES), F32)],
        compiler_params=_cparams(("arbitrary",), 56),
        name="dispatch",
    )(plan["unit_dst"], plan["used"], plan["padstart"], plan["padn"], u2, gates_t, plan["loff_v"], tri)


def _experts_kernel(be_ref, sb_ref, br_ref, nx_ref, nbu_ref, x_ref, wg_hbm, wu_hbm, wd_hbm, y_ref,
                    wg_st, wu_st, wd_st, wgb_ref, wub_ref, wdb_ref, sems):
    b = pl.program_id(0)

    def weight_copies(e):
        return (pltpu.make_async_copy(wg_hbm.at[e], wg_st, sems.at[0]),
                pltpu.make_async_copy(wu_hbm.at[e], wu_st, sems.at[1]),
                pltpu.make_async_copy(wd_hbm.at[e], wd_st, sems.at[2]))

    @pl.when(b < nbu_ref[0])
    def _():
        e = be_ref[b]

        @pl.when(b == 0)
        def _():
            for cp in weight_copies(e):
                cp.start()

        @pl.when(jnp.logical_or(b == 0, e != be_ref[jnp.maximum(b - 1, 0)]))
        def _():
            for cp in weight_copies(e):
                cp.wait()
            wgb_ref[...] = wg_st[...].astype(BF16)
            wub_ref[...] = wu_st[...].astype(BF16)
            wdb_ref[...] = wd_st[...].astype(BF16)

            @pl.when(nx_ref[b] >= 0)
            def _():
                for cp in weight_copies(nx_ref[b]):
                    cp.start()

        for rows in range(MOE_SUB, MOE_BM + 1, MOE_SUB):
            @pl.when(br_ref[b] == rows)
            def _(rows=rows):
                x = x_ref[:rows, :D_MODEL]
                g3 = ((x_ref[:rows, D_MODEL:D_MODEL + LANES].astype(F32)
                       + x_ref[:rows, D_MODEL + LANES:D_MODEL + 2 * LANES].astype(F32))
                      + x_ref[:rows, D_MODEL + 2 * LANES:].astype(F32))
                lane = lax.broadcasted_iota(I32, g3.shape, 1)
                gate = jnp.sum(jnp.where(lane == e, g3, 0.0), axis=1, keepdims=True)
                hid = (_silu(_dot(x, wgb_ref[...])) * _dot(x, wub_ref[...]) * gate).astype(BF16)
                y_ref[:rows, :] = _dot(hid, wdb_ref[...]).astype(y_ref.dtype)


def _experts(xs, plan, wg, wu, wd):
    ns, xw = xs.shape
    _, d, f = wg.shape
    step = lambda b, nbu: jnp.minimum(b, nbu[0] - 1)
    rows_map = lambda b, be, sb, br, nx, nbu: (sb[step(b, nbu)], 0)
    hbm = pl.BlockSpec(memory_space=pl.ANY)
    grid_spec = pltpu.PrefetchScalarGridSpec(
        num_scalar_prefetch=5,
        grid=(ns // MOE_BM,),
        in_specs=[pl.BlockSpec((MOE_BM, xw), rows_map), hbm, hbm, hbm],
        out_specs=pl.BlockSpec((MOE_BM, d), rows_map),
        scratch_shapes=[pltpu.VMEM((d, f), F32), pltpu.VMEM((d, f), F32), pltpu.VMEM((f, d), F32),
                        pltpu.VMEM((d, f), BF16), pltpu.VMEM((d, f), BF16), pltpu.VMEM((f, d), BF16),
                        pltpu.SemaphoreType.DMA((3,))])
    return pl.pallas_call(
        _experts_kernel,
        grid_spec=grid_spec,
        out_shape=jax.ShapeDtypeStruct((ns, d), BF16),
        compiler_params=_cparams(("arbitrary",), 48),
        name="experts",
    )(plan["step_e"], plan["step_blk"], plan["step_rows"], plan["step_next_e"], plan["nb_used"], xs, wg, wu, wd)


def _final_kernel(ud_ref, used_ref, lkt_ref, u_ref, wg_ref, wu_ref, wd_ref, x1_ref, g2_ref, lg_ref, lb_ref,
                  ys_ref, o_ref, ybuf_ref, sems):
    j = pl.program_id(0)
    used_units = used_ref[j]
    upc = MOE_RCH // MOE_UNIT

    @pl.when(j == 0)
    def _():
        ybuf_ref[...] = jnp.zeros(ybuf_ref.shape, ybuf_ref.dtype)

    def unit_copy(src_unit, dst_unit, chunk):
        return pltpu.make_async_copy(ys_ref.at[_unit_rows(src_unit)], ybuf_ref.at[_unit_rows(dst_unit)],
                                     sems.at[chunk])

    n_chunks = (used_units + upc - 1) // upc

    def fetch_chunk(rc, carry):
        def fetch_unit(uu, c):
            unit = rc * upc + uu
            unit_copy(ud_ref[j * (MOE_RCAP // MOE_UNIT) + unit], unit, rc).start()
            return c

        lax.fori_loop(0, jnp.clip(used_units - rc * upc, 0, upc), fetch_unit, 0)
        return carry

    lax.fori_loop(0, n_chunks, fetch_chunk, 0)

    u = u_ref[...]
    hid = (_silu(_dot(u, wg_ref[...])) * _dot(u, wu_ref[...])).astype(BF16)
    o_ref[...] = _dot(hid, wd_ref[...])

    lkt = lkt_ref[0]

    def combine_chunk(rc, carry):
        def wait_unit(uu, c):
            unit_copy(0, 0, rc).wait()
            return c

        lax.fori_loop(0, jnp.clip(used_units - rc * upc, 0, upc), wait_unit, 0)
        r0 = pl.multiple_of(rc * MOE_RCH, MOE_RCH)
        row_id = (lax.broadcasted_iota(I32, (MOE_TILE, MOE_RCH), 1) + r0).astype(F32)
        o_ref[...] += _dot(_one_hot_rows(row_id, lkt, 1), ybuf_ref[pl.ds(r0, MOE_RCH), :])
        return carry

    lax.fori_loop(0, n_chunks, combine_chunk, 0)
    h = ALPHA * x1_ref[...] + g2_ref[0] * o_ref[...]
    o_ref[...] = _layer_norm_rows(h, lg_ref[...], lb_ref[...])


def _final(u2, wg_s, wu_s, wd_s, ys, lkt, plan, x1, gate2, ln_g, ln_b, seq):
    t, d = x1.shape
    tm = MOE_TILE
    tpb = seq // tm
    f = wg_s.shape[1]
    once = pl.Buffered(1)
    rows = lambda: pl.BlockSpec((tm, d), lambda j, *_: (j, 0))
    full = lambda shape: pl.BlockSpec(shape, lambda j, *_: (0,) * len(shape), pipeline_mode=once)
    grid_spec = pltpu.PrefetchScalarGridSpec(
        num_scalar_prefetch=2,
        grid=(t // tm,),
        in_specs=[pl.BlockSpec((1, tm, LANES), lambda j, *_: (j, 0, 0)), rows(),
                  full((d, f)), full((d, f)), full((f, d)), rows(),
                  pl.BlockSpec((1, 1, d), lambda j, *_: (j // tpb, 0, 0)), full((1, d)), full((1, d)),
                  pl.BlockSpec(memory_space=pl.ANY)],
        out_specs=rows(),
        scratch_shapes=[pltpu.VMEM((MOE_RCAP, d), BF16), pltpu.SemaphoreType.DMA((MOE_RCAP // MOE_RCH,))])
    return pl.pallas_call(
        _final_kernel,
        grid_spec=grid_spec,
        out_shape=jax.ShapeDtypeStruct((t, d), F32),
        compiler_params=_cparams(("arbitrary",), 58),
        name="final",
    )(plan["unit_dst"], plan["used"], lkt, u2, wg_s, wu_s, wd_s, x1, gate2, ln_g[None, :], ln_b[None, :], ys)


def _split_hi_lo(w):
    hi = w.astype(BF16)
    lo = (w - hi.astype(F32)).astype(BF16)
    return hi, lo


def _rope_tables(positions, dim, reps):
    inv_freq = 1.0 / (ROPE_THETA ** (jnp.arange(0, dim, 2, dtype=F32) / dim))
    ang = positions.astype(F32).reshape(-1)[:, None] * inv_freq
    cos, sin = jnp.cos(ang), jnp.sin(ang)
    return (jnp.tile(jnp.concatenate([cos, cos], axis=1), (1, reps)),
            jnp.tile(jnp.concatenate([-sin, sin], axis=1), (1, reps)))


def _layer(x2, c, positions, bsz, seq, w_ada, b_ada, w_in, conv_w, conv_b, dt_bias, a_log, d_skip, ssd_norm_w,
           idx_k_ln_g, idx_k_ln_b, w_out, ln1_g, ln1_b, w_router, router_bias, w_gate_e, w_up_e, w_down_e,
           w_gate_s, w_up_s, w_down_s, ln2_g, ln2_b, tables):
    d = D_MODEL
    mod = _ada(c, w_ada, b_ada).reshape(bsz, 6, 1, d)
    shift1, scale1, gate1, shift2, scale2, gate2 = [mod[:, k] for k in range(6)]

    o_z, o_xbc, o_dt, o_q = 0, D_SSM, 2 * D_SSM + 2 * SSM_GROUPS * SSM_STATE, 2 * D_SSM + 2 * SSM_GROUPS * SSM_STATE + SSM_HEADS
    o_k, o_v, o_qi = o_q + D_ATTN, o_q + 2 * D_ATTN, o_q + 3 * D_ATTN
    o_ki = o_qi + IDX_HEADS * IDX_DIM
    o_wi = o_ki + IDX_DIM
    w_t = w_in.T
    w_main = jnp.concatenate([w_t[o_z:o_z + D_SSM], w_t[o_xbc:o_xbc + D_SSM], w_t[o_q:o_ki],
                              w_t[o_xbc + D_SSM:o_dt]], axis=0).astype(BF16)
    padr = lambda w: jnp.pad(w, ((0, LANES - w.shape[0]), (0, 0)))
    w_tail = jnp.concatenate([padr(w_t[o_dt:o_q]), padr(w_t[o_ki:o_wi]), padr(w_t[o_wi:])], axis=0)
    wt_hi, wt_lo = _split_hi_lo(w_tail)

    proj, tail = _inproj(x2, scale1, shift1, w_main, wt_hi, wt_lo, seq)
    y_ssm = _ssm(proj, tail, conv_w, conv_b, dt_bias, a_log, d_skip, ssd_norm_w, bsz, seq)
    q_r, k_r, v_t, qi_r, ki0, ki1, w_s = _prep(proj, tail, *tables, idx_k_ln_g, idx_k_ln_b)
    y_att = _attn(q_r, k_r, v_t, qi_r, ki0, ki1, w_s, bsz, seq)

    wo = w_out.astype(BF16)
    wr_hi, wr_lo = _split_hi_lo(w_router.T)
    rbias = jnp.broadcast_to(router_bias[:, None], (N_EXPERTS, LANES))
    x1, u2, gates_t = _outproj(y_ssm, y_att, wo[:D_SSM], wo[D_SSM:], x2, gate1, scale2, shift2, ln1_g, ln1_b,
                               wr_hi, wr_lo, rbias, seq)
    plan = _moe_plan(gates_t, x2.shape[0] // MOE_TILE)
    xs, lkt = _dispatch(u2, gates_t, plan)
    ys = _experts(xs, plan, w_gate_e, w_up_e, w_down_e)
    return _final(u2, w_gate_s.astype(BF16), w_up_s.astype(BF16), w_down_s.astype(BF16), ys, lkt, plan, x1, gate2,
                  ln2_g, ln2_b, seq)


def kernel(x, c, positions, w_ada, b_ada, w_in, conv_w, conv_b, dt_bias, a_log, d_skip, ssd_norm_w, idx_k_ln_g, idx_k_ln_b, w_out, ln1_g, ln1_b, w_router, router_bias, w_gate_e, w_up_e, w_down_e, w_gate_s, w_up_s, w_down_s, ln2_g, ln2_b):
    bsz, seq, d = x.shape
    tables = _rope_tables(positions, ATTN_HEAD_DIM, 1) + _rope_tables(positions, IDX_DIM, 2)
    x2 = x.reshape(bsz * seq, d)
    for l in range(w_ada.shape[0]):
        x2 = _layer(x2, c, positions, bsz, seq, w_ada[l], b_ada[l], w_in[l], conv_w[l], conv_b[l], dt_bias[l],
                    a_log[l], d_skip[l], ssd_norm_w[l], idx_k_ln_g[l], idx_k_ln_b[l], w_out[l], ln1_g[l], ln1_b[l],
                    w_router[l], router_bias[l], w_gate_e[l], w_up_e[l], w_down_e[l], w_gate_s[l], w_up_s[l],
                    w_down_s[l], ln2_g[l], ln2_b[l], tables)
    return x2.reshape(bsz, seq, d)
```

```python
import functools
import math

import jax
import jax.numpy as jnp
import numpy as np
from jax import lax
from jax.experimental import pallas as pl
from jax.experimental.pallas import tpu as pltpu

F32 = jnp.float32
BF16 = jnp.bfloat16
I32 = jnp.int32
HIGHEST = lax.Precision.HIGHEST

D_MODEL = 2048
D_SSM = 1024
D_ATTN = 1024
SSM_HEAD_DIM = 64
SSM_HEADS = 16
SSM_GROUPS = 2
SSM_STATE = 128
CONV_WIDTH = 4
SSD_CHUNK = 128
ATTN_HEAD_DIM = 128
ATTN_HEADS = 8
IDX_HEADS = 16
IDX_DIM = 64
INDEX_TOPK = 256
ROPE_THETA = 10000.0
N_EXPERTS = 64
N_EXPERT_GROUPS = 8
EXPERTS_PER_GROUP = 8
TOPK_GROUPS = 4
TOP_K = 8
D_EXPERT = 512
D_SHARED = 512
ROUTED_SCALE = 2.5
DEPTH = 1
ALPHA = (2.0 * DEPTH) ** 0.25
LN_EPS = 1e-5

LANES = 128
SUBLANES = 8
VMEM_BYTES_V7X = 64 * 1024 * 1024
INT_MIN = -(2 ** 31)
KEY_LOWEST_FINITE = INT_MIN + 0x00800000

COL_Z, COL_XS, COL_Q, COL_K, COL_V, COL_QI, COL_BC = 0, 1024, 2048, 3072, 4096, 5120, 6144
N_MAIN = 6656
N_TAIL = LANES
TAIL_WI = SSM_HEADS
TAIL_KI = LANES - IDX_DIM


def _cparams(sem, vmem_mb):
    return pltpu.CompilerParams(dimension_semantics=sem, vmem_limit_bytes=vmem_mb * 1024 * 1024)


def _silu(v):
    return 0.5 * v * (1.0 + jnp.tanh(0.5 * v))


def _dot(a, b, precision=None):
    return jnp.dot(a, b, preferred_element_type=F32, precision=precision)


def _dot_nt(a, b, precision=None):
    return lax.dot_general(a, b, (((1,), (1,)), ((), ())), preferred_element_type=F32, precision=precision)


def _split3(x):
    hi = x.astype(BF16)
    r1 = x - hi.astype(F32)
    mid = r1.astype(BF16)
    return hi, mid, (r1 - mid.astype(F32)).astype(BF16)


def _layer_norm_rows(h, g, b):
    mu = jnp.mean(h, axis=-1, keepdims=True)
    d = h - mu
    var = jnp.mean(d * d, axis=-1, keepdims=True)
    return d * lax.rsqrt(var + LN_EPS) * g + b


def _ada_kernel(cb_ref, w_ref, b_ref, o_ref):
    tn = w_ref.shape[1]
    for bi in range(cb_ref.shape[0]):
        cb = _silu(cb_ref[bi])
        cols = [jnp.sum(w_ref[:, j * LANES:(j + 1) * LANES] * cb, axis=0, keepdims=True)
                for j in range(tn // LANES)]
        o_ref[bi:bi + 1, :] = jnp.concatenate(cols, axis=1) + b_ref[...]


def _ada(c, w_ada, b_ada):
    bsz, d = c.shape
    n = w_ada.shape[1]
    tn = 1024
    cb = jnp.broadcast_to(c[:, :, None], (bsz, d, LANES))
    return pl.pallas_call(
        _ada_kernel,
        grid=(n // tn,),
        in_specs=[pl.BlockSpec((bsz, d, LANES), lambda j: (0, 0, 0)),
                  pl.BlockSpec((d, tn), lambda j: (0, j)),
                  pl.BlockSpec((1, tn), lambda j: (0, j))],
        out_specs=pl.BlockSpec((bsz, tn), lambda j: (0, j)),
        out_shape=jax.ShapeDtypeStruct((bsz, n), F32),
        compiler_params=_cparams(("parallel",), 40),
        name="ada",
    )(cb, w_ada, b_ada.reshape(1, n))


def _inproj_kernel(x_ref, sc_ref, sh_ref, w_ref, wth_ref, wtl_ref, o_ref, t_ref, u_ref):
    @pl.when(pl.program_id(1) == 0)
    def _():
        u = x_ref[...] * (1.0 + sc_ref[0]) + sh_ref[0]
        uh = u.astype(BF16)
        ul = (u - uh.astype(F32)).astype(BF16)
        u_ref[...] = uh
        t_ref[...] = (_dot_nt(uh, wth_ref[...]) + _dot_nt(uh, wtl_ref[...]) + _dot_nt(ul, wth_ref[...]))

    o_ref[...] = _dot_nt(u_ref[...], w_ref[...])


def _inproj(x2, scale1, shift1, w_main, wt_hi, wt_lo, seq):
    t, d = x2.shape
    tm, tn = 1024, 512
    tpb = seq // tm
    return pl.pallas_call(
        _inproj_kernel,
        grid=(t // tm, N_MAIN // tn),
        in_specs=[pl.BlockSpec((tm, d), lambda i, j: (i, 0)),
                  pl.BlockSpec((1, 1, d), lambda i, j: (i // tpb, 0, 0)),
                  pl.BlockSpec((1, 1, d), lambda i, j: (i // tpb, 0, 0)),
                  pl.BlockSpec((tn, d), lambda i, j: (j, 0)),
                  pl.BlockSpec((N_TAIL, d), lambda i, j: (0, 0)),
                  pl.BlockSpec((N_TAIL, d), lambda i, j: (0, 0))],
        out_specs=[pl.BlockSpec((tm, tn), lambda i, j: (i, j)),
                   pl.BlockSpec((tm, N_TAIL), lambda i, j: (i, 0))],
        out_shape=[jax.ShapeDtypeStruct((t, N_MAIN), F32),
                   jax.ShapeDtypeStruct((t, N_TAIL), F32)],
        scratch_shapes=[pltpu.VMEM((tm, d), BF16)],
        compiler_params=_cparams(("parallel", "arbitrary"), 48),
        name="inproj",
    )(x2, scale1, shift1, w_main, wt_hi, wt_lo)


def _ssm_kernel(z_ref, xs_ref, bc_ref, dt_ref, cwx_ref, cbx_ref, cwb_ref, cbb_ref, dtb_ref, alog_ref,
                dsk_ref, nw_ref, e_ref, e2_ref, tril_ref, o_ref, px_ref, pb_ref, st_ref):
    q = SSD_CHUNK
    hpg = SSM_HEADS // SSM_GROUPS
    gw = hpg * SSM_HEAD_DIM

    @pl.when(pl.program_id(1) == 0)
    def _():
        px_ref[0:SUBLANES, :] = jnp.zeros((SUBLANES, px_ref.shape[1]), F32)
        pb_ref[0:SUBLANES, :] = jnp.zeros((SUBLANES, pb_ref.shape[1]), F32)
        st_ref[...] = jnp.zeros(st_ref.shape, F32)

    def conv_silu(raw_ref, pad_ref, w_ref, b_ref):
        pad_ref[SUBLANES:SUBLANES + q, :] = raw_ref[...]
        acc = b_ref[...] + w_ref[0:1, :] * pad_ref[SUBLANES - 3:SUBLANES - 3 + q, :]
        for k in range(1, CONV_WIDTH):
            acc = acc + w_ref[k:k + 1, :] * pad_ref[SUBLANES - 3 + k:SUBLANES - 3 + k + q, :]
        pad_ref[0:SUBLANES, :] = raw_ref[q - SUBLANES:q, :]
        return _silu(acc)

    xs = conv_silu(xs_ref, px_ref, cwx_ref, cbx_ref)
    bc = conv_silu(bc_ref, pb_ref, cwb_ref, cbb_ref)

    dtr = dt_ref[...] + dtb_ref[...]
    dt = jnp.maximum(dtr, 0.0) + jnp.log(1.0 + jnp.exp(-jnp.abs(dtr)))
    log_a = dt * (-jnp.exp(alog_ref[...]))
    cs = sum(_dot(tril_ref[...], part) for part in _split3(log_a))
    cs_parts = _split3(cs)
    cs_e = sum(_dot(part, e_ref[...]) for part in cs_parts)
    dt_e = sum(_dot(part, e_ref[...]) for part in _split3(dt))
    cs_col = sum(_dot(part, e2_ref[...]) for part in cs_parts)
    cs_t = cs.T
    cs_last = cs_e[q - 1:q, :]

    xdt = xs * dt_e
    rows = lax.broadcasted_iota(I32, (q, q), 0)
    cols = lax.broadcasted_iota(I32, (q, q), 1)
    causal = rows >= cols
    first_half = lax.broadcasted_iota(I32, (q, LANES), 1) < SSM_HEAD_DIM

    y_pairs = []
    for g in range(SSM_GROUPS):
        b_g = bc[:, g * SSM_STATE:(g + 1) * SSM_STATE]
        c_g = bc[:, (SSM_GROUPS + g) * SSM_STATE:(SSM_GROUPS + g + 1) * SSM_STATE]
        cb = _dot_nt(c_g.astype(BF16), b_g.astype(BF16))
        for hp in range(hpg // 2):
            pair = g * (hpg // 2) + hp
            x_pair = xdt[:, pair * LANES:(pair + 1) * LANES].astype(BF16)
            ys = []
            for sub in range(2):
                h = 2 * pair + sub
                seg = cs_col[:, h * LANES:(h + 1) * LANES] - cs_t[h:h + 1, :]
                dec = jnp.exp(jnp.where(causal, seg, -jnp.inf))
                ys.append(_dot((cb * dec).astype(BF16), x_pair))
            y_pairs.append(jnp.where(first_half, ys[0], ys[1]))
    y_diag = jnp.concatenate(y_pairs, axis=1)

    xw = xdt * jnp.exp(cs_last - cs_e)
    y_off, new_states = [], []
    for g in range(SSM_GROUPS):
        b_g = bc[:, g * SSM_STATE:(g + 1) * SSM_STATE]
        c_g = bc[:, (SSM_GROUPS + g) * SSM_STATE:(SSM_GROUPS + g + 1) * SSM_STATE]
        h_in = st_ref[:, g * gw:(g + 1) * gw]
        y_off.append(_dot(c_g.astype(BF16), h_in.astype(BF16)))
        new_states.append(_dot(b_g.T.astype(BF16), xw[:, g * gw:(g + 1) * gw].astype(BF16)))
    y_off = jnp.concatenate(y_off, axis=1) * jnp.exp(cs_e)
    st_ref[...] = jnp.exp(cs_last) * st_ref[...] + jnp.concatenate(new_states, axis=1)

    y = y_diag + y_off + dsk_ref[...] * xs
    yf = y * _silu(z_ref[...])
    ms = jnp.mean(yf * yf, axis=-1, keepdims=True)
    o_ref[...] = (yf * lax.rsqrt(ms + LN_EPS) * nw_ref[...]).astype(o_ref.dtype)


def _ssm(proj, tail, conv_w, conv_b, dt_bias, a_log, d_skip, ssd_norm_w, bsz, seq):
    t = proj.shape[0]
    q = SSD_CHUNK
    n_c = seq // q
    nbc = 2 * SSM_GROUPS * SSM_STATE
    cw_x, cw_b = conv_w[:, :D_SSM], conv_w[:, D_SSM:]
    cb_x, cb_b = conv_b[None, :D_SSM], conv_b[None, D_SSM:]
    pad16 = lambda v: jnp.pad(v, (0, LANES - SSM_HEADS))[None, :]
    head_of_lane = np.arange(D_SSM) // SSM_HEAD_DIM
    e_mat = jnp.asarray((np.arange(LANES)[:, None] == head_of_lane[None, :]).astype(np.float32), BF16)
    e2_mat = jnp.asarray((np.arange(LANES)[:, None] == (np.arange(SSM_HEADS * LANES) // LANES)[None, :])
                         .astype(np.float32), BF16)
    tril = jnp.asarray(np.tril(np.ones((q, q), np.float32)), BF16)
    row = lambda b, c: b * n_c + c
    full = lambda shape: pl.BlockSpec(shape, lambda b, c: (0,) * len(shape))
    return pl.pallas_call(
        _ssm_kernel,
        grid=(bsz, n_c),
        in_specs=[pl.BlockSpec((q, D_SSM), lambda b, c: (row(b, c), COL_Z // D_SSM)),
                  pl.BlockSpec((q, D_SSM), lambda b, c: (row(b, c), COL_XS // D_SSM)),
                  pl.BlockSpec((q, nbc), lambda b, c: (row(b, c), COL_BC // nbc)),
                  pl.BlockSpec((q, LANES), lambda b, c: (row(b, c), 0)),
                  full((CONV_WIDTH, D_SSM)), full((1, D_SSM)), full((CONV_WIDTH, nbc)), full((1, nbc)),
                  full((1, LANES)), full((1, LANES)), full((1, D_SSM)), full((1, D_SSM)),
                  full((LANES, D_SSM)), full((LANES, SSM_HEADS * LANES)), full((q, q))],
        out_specs=pl.BlockSpec((q, D_SSM), lambda b, c: (row(b, c), 0)),
        out_shape=jax.ShapeDtypeStruct((t, D_SSM), BF16),
        scratch_shapes=[pltpu.VMEM((SUBLANES + q, D_SSM), F32),
                        pltpu.VMEM((SUBLANES + q, nbc), F32),
                        pltpu.VMEM((SSM_STATE, D_SSM), F32)],
        compiler_params=_cparams(("parallel", "arbitrary"), 40),
        name="ssm",
    )(proj, proj, proj, tail, cw_x, cb_x, cw_b, cb_b, pad16(dt_bias), pad16(a_log),
      jnp.repeat(d_skip, SSM_HEAD_DIM)[None, :], ssd_norm_w[None, :], e_mat, e2_mat, tril)


def _prep_kernel(q_ref, k_ref, v_ref, qi_ref, t_ref, ca_ref, sa_ref, ci_ref, si_ref, lg_ref, lb_ref,
                 qo_ref, ko_ref, vo_ref, qio_ref, ki0_ref, ki1_ref, wo_ref):
    ca, sa, ci, si = ca_ref[...], sa_ref[...], ci_ref[...], si_ref[...]
    lane = lax.broadcasted_iota(I32, ca.shape, 1)
    first32 = (lane % IDX_DIM) < (IDX_DIM // 2)
    q_scale = ATTN_HEAD_DIM ** -0.5 * math.log2(math.e)

    def rope_attn(v):
        return v * ca + pltpu.roll(v, ATTN_HEAD_DIM // 2, 1) * sa

    def rope_idx(v):
        rot = jnp.where(first32, pltpu.roll(v, LANES - IDX_DIM // 2, 1), pltpu.roll(v, IDX_DIM // 2, 1))
        return v * ci + rot * si

    for h in range(ATTN_HEADS):
        sl = slice(h * LANES, (h + 1) * LANES)
        qo_ref[:, sl] = (rope_attn(q_ref[:, sl]) * q_scale).astype(BF16)
        ko_ref[:, sl] = rope_attn(k_ref[:, sl]).astype(BF16)
    vo_ref[0] = v_ref[...].T.astype(BF16)
    for p in range(IDX_HEADS * IDX_DIM // LANES):
        sl = slice(p * LANES, (p + 1) * LANES)
        qio_ref[:, sl] = rope_idx(qi_ref[:, sl]).astype(BF16)

    tail = t_ref[...]
    valid = lane < IDX_DIM
    kraw = jnp.where(valid, pltpu.roll(tail, LANES - TAIL_KI, 1), 0.0)
    mu = jnp.sum(kraw, axis=-1, keepdims=True) * (1.0 / IDX_DIM)
    dk = jnp.where(valid, kraw - mu, 0.0)
    var = jnp.sum(dk * dk, axis=-1, keepdims=True) * (1.0 / IDX_DIM)
    kn = jnp.where(valid, dk * lax.rsqrt(var + LN_EPS) * lg_ref[...] + lb_ref[...], 0.0)
    kr = jnp.where(valid, rope_idx(kn), 0.0)
    ki0_ref[...] = kr.astype(BF16)
    ki1_ref[...] = pltpu.roll(kr, IDX_DIM, 1).astype(BF16)
    wo_ref[...] = pltpu.roll(tail, LANES - TAIL_WI, 1) * (IDX_HEADS ** -0.5 * IDX_DIM ** -0.5)


def _prep(proj, tail, cos_a, sin_a, cos_i, sin_i, ln_g, ln_b):
    t = proj.shape[0]
    tm = ATT_KC
    w = D_ATTN
    pad64 = lambda v: jnp.pad(v, (0, LANES - IDX_DIM))[None, :]
    col = lambda c: pl.BlockSpec((tm, w), lambda i: (i, c // w))
    lane_blk = lambda c: pl.BlockSpec((tm, LANES), lambda i: (i, c))
    full = lambda: pl.BlockSpec((1, LANES), lambda i: (0, 0))
    return pl.pallas_call(
        _prep_kernel,
        grid=(t // tm,),
        in_specs=[col(COL_Q), col(COL_K), col(COL_V), col(COL_QI), lane_blk(0),
                  lane_blk(0), lane_blk(0), lane_blk(0), lane_blk(0), full(), full()],
        out_specs=[pl.BlockSpec((tm, w), lambda i: (i, 0))] * 2 + [pl.BlockSpec((1, w, tm), lambda i: (i, 0, 0))]
        + [pl.BlockSpec((tm, w), lambda i: (i, 0))] + [pl.BlockSpec((tm, LANES), lambda i: (i, 0))] * 3,
        out_shape=[jax.ShapeDtypeStruct((t, w), BF16)] * 2 + [jax.ShapeDtypeStruct((t // tm, w, tm), BF16)]
        + [jax.ShapeDtypeStruct((t, w), BF16)]
        + [jax.ShapeDtypeStruct((t, LANES), BF16)] * 2 + [jax.ShapeDtypeStruct((t, LANES), F32)],
        compiler_params=_cparams(("parallel",), 48),
        name="prep",
    )(proj, proj, proj, proj, tail, cos_a, sin_a, cos_i, sin_i, pad64(ln_g), pad64(ln_b))


ATT_QB = 128
ATT_KC = 512
ATT_KS = 256


def _attn_kernel(q_ref, k_ref, vt_ref, qi_ref, ki0_ref, ki1_ref, w_ref, o_ref, sc_ref, acc_ref, *, topk):
    i = pl.program_id(1)
    n_ch = (i * ATT_QB + ATT_QB + ATT_KC - 1) // ATT_KC
    key_off = lax.broadcasted_iota(I32, (ATT_KC, ATT_QB), 0)
    q_pos = i * ATT_QB + lax.broadcasted_iota(I32, (ATT_KC, ATT_QB), 1)
    w_t = w_ref[...].T

    def score_chunk(c, carry):
        start = pl.multiple_of(c * ATT_KC, ATT_KC)
        ki0 = ki0_ref[pl.ds(start, ATT_KC), :]
        ki1 = ki1_ref[pl.ds(start, ATT_KC), :]
        acc = jnp.zeros((ATT_KC, ATT_QB), F32)
        for h in range(IDX_HEADS):
            pair = qi_ref[:, (h // 2) * LANES:(h // 2 + 1) * LANES]
            rel = _dot_nt(ki0 if h % 2 == 0 else ki1, pair)
            acc = acc + jnp.maximum(rel, 0.0) * w_t[h:h + 1, :]
        sc_ref[c] = jnp.where(start + key_off <= q_pos, acc, -jnp.inf)
        return carry

    lax.fori_loop(0, n_ch, score_chunk, 0)

    def key_to_float(key):
        return pltpu.bitcast(jnp.where(key < 0, key ^ jnp.int32(0x7FFFFFFF), key), F32)

    def search_bit(it, ans):
        cand = ans ^ lax.shift_left(jnp.int32(1), jnp.int32(31) - it)
        cand_f = key_to_float(cand)

        def count_chunk(c, cnt):
            part = jnp.where(sc_ref[c] >= cand_f, 1, 0).astype(I32)
            return cnt + jnp.sum(part.reshape(ATT_KC // SUBLANES, SUBLANES, ATT_QB), axis=0)

        cnt = lax.fori_loop(0, n_ch, count_chunk, jnp.zeros((SUBLANES, ATT_QB), I32))
        total = jnp.sum(cnt, axis=0, keepdims=True)
        return jnp.where(total >= topk, cand, ans)

    n_bits = jnp.where((i + 1) * ATT_QB <= topk, 0, 32)
    kth = lax.fori_loop(0, n_bits, search_bit, jnp.full((1, ATT_QB), INT_MIN, I32))
    lowest_finite = jnp.float32(np.finfo(np.float32).min)
    thr = jnp.where(kth < KEY_LOWEST_FINITE, lowest_finite, key_to_float(kth))

    acc_ref[...] = jnp.zeros(acc_ref.shape, F32)

    def att_chunk(c, carry):
        m_all, l_all = carry
        start = pl.multiple_of(c * ATT_KC, ATT_KC)
        for part in range(ATT_KC // ATT_KS):
            ks = slice(part * ATT_KS, (part + 1) * ATT_KS)
            sel = sc_ref[c, ks, :] >= thr
            m_rows, l_rows = [], []
            for h in range(ATTN_HEADS):
                sl = slice(h * LANES, (h + 1) * LANES)
                s = _dot_nt(k_ref[pl.ds(start + part * ATT_KS, ATT_KS), sl], q_ref[:, sl])
                s = jnp.where(sel, s, -jnp.inf)
                m_old = m_all[h:h + 1, :]
                m_new = jnp.maximum(m_old, jnp.max(s, axis=0, keepdims=True))
                p = jnp.exp2(s - m_new)
                alpha = jnp.exp2(m_old - m_new)
                l_rows.append(alpha * l_all[h:h + 1, :] + jnp.sum(p, axis=0, keepdims=True))
                m_rows.append(m_new)
                acc_ref[h] = alpha * acc_ref[h] + _dot(vt_ref[c, sl, ks], p.astype(BF16))
            m_all = jnp.concatenate(m_rows, axis=0)
            l_all = jnp.concatenate(l_rows, axis=0)
        return m_all, l_all

    init = (jnp.full((ATTN_HEADS, ATT_QB), -1e30, F32), jnp.zeros((ATTN_HEADS, ATT_QB), F32))
    _, l_all = lax.fori_loop(0, n_ch, att_chunk, init)
    for h in range(ATTN_HEADS):
        o_ref[:, h * LANES:(h + 1) * LANES] = (acc_ref[h] / l_all[h:h + 1, :]).T.astype(o_ref.dtype)


def _attn(q_r, k_r, v_t, qi_r, ki0, ki1, w_s, bsz, seq):
    t = q_r.shape[0]
    nq = seq // ATT_QB
    n_kc = seq // ATT_KC
    topk = min(INDEX_TOPK, seq // 4)
    once = pl.Buffered(1)
    qblk = lambda width: pl.BlockSpec((ATT_QB, width), lambda b, i: (b * nq + i, 0))
    per_batch = lambda width: pl.BlockSpec((seq, width), lambda b, i: (b, 0), pipeline_mode=once)
    return pl.pallas_call(
        functools.partial(_attn_kernel, topk=topk),
        grid=(bsz, nq),
        in_specs=[qblk(D_ATTN), per_batch(D_ATTN),
                  pl.BlockSpec((n_kc, D_ATTN, ATT_KC), lambda b, i: (b, 0, 0), pipeline_mode=once),
                  qblk(IDX_HEADS * IDX_DIM), per_batch(LANES), per_batch(LANES), qblk(LANES)],
        out_specs=qblk(D_ATTN),
        out_shape=jax.ShapeDtypeStruct((t, D_ATTN), BF16),
        scratch_shapes=[pltpu.VMEM((n_kc, ATT_KC, ATT_QB), F32),
                        pltpu.VMEM((ATTN_HEADS, ATTN_HEAD_DIM, ATT_QB), F32)],
        compiler_params=_cparams(("parallel", "arbitrary"), 48),
        name="attn",
    )(q_r, k_r, v_t, qi_r, ki0, ki1, w_s)


def _first_index_of_max(v, iota, n):
    m = jnp.max(v, axis=0, keepdims=True)
    idx = jnp.min(jnp.where(v == m, iota, n), axis=0, keepdims=True)
    return m, idx


def _route(scores, biased):
    tm = scores.shape[1]
    neg_inf = jnp.float32(-jnp.inf)
    iota8 = lax.broadcasted_iota(I32, (EXPERTS_PER_GROUP, tm), 0)
    group_scores = []
    for g in range(N_EXPERT_GROUPS):
        v = biased[g * EXPERTS_PER_GROUP:(g + 1) * EXPERTS_PER_GROUP, :]
        m1, i1 = _first_index_of_max(v, iota8, EXPERTS_PER_GROUP)
        m2 = jnp.max(jnp.where(iota8 == i1, neg_inf, v), axis=0, keepdims=True)
        group_scores.append(m1 + m2)
    gs = jnp.concatenate(group_scores, axis=0)
    keep = jnp.zeros(gs.shape, jnp.bool_)
    for _ in range(TOPK_GROUPS):
        _, gi = _first_index_of_max(gs, iota8, N_EXPERT_GROUPS)
        hit = iota8 == gi
        keep = jnp.logical_or(keep, hit)
        gs = jnp.where(hit, neg_inf, gs)
    keep_f = jnp.where(keep, 1.0, 0.0)
    masked = jnp.concatenate(
        [jnp.where(keep_f[g:g + 1, :] > 0.0, biased[g * EXPERTS_PER_GROUP:(g + 1) * EXPERTS_PER_GROUP, :], neg_inf)
         for g in range(N_EXPERT_GROUPS)], axis=0)
    iota64 = lax.broadcasted_iota(I32, (N_EXPERTS, tm), 0)
    sel_w = jnp.zeros((N_EXPERTS, tm), F32)
    for _ in range(TOP_K):
        _, ei = _first_index_of_max(masked, iota64, N_EXPERTS)
        hit = iota64 == ei
        sel_w = jnp.where(hit, scores, sel_w)
        masked = jnp.where(hit, neg_inf, masked)
    denom = jnp.sum(sel_w, axis=0, keepdims=True)
    return sel_w / denom * ROUTED_SCALE


def _outproj_kernel(ys_ref, ya_ref, wa_ref, wb_ref, x_ref, g1_ref, sc2_ref, sh2_ref, lg_ref, lb_ref,
                    wrh_ref, wrl_ref, rb_ref, x1_ref, u2_ref, gt_ref):
    mix = _dot(ys_ref[...], wa_ref[...]) + _dot(ya_ref[...], wb_ref[...])
    h = ALPHA * x_ref[...] + g1_ref[0] * mix
    x1 = _layer_norm_rows(h, lg_ref[...], lb_ref[...])
    x1_ref[...] = x1
    u2 = x1 * (1.0 + sc2_ref[0]) + sh2_ref[0]
    uh = u2.astype(BF16)
    ul = (u2 - uh.astype(F32)).astype(BF16)
    u2_ref[...] = uh
    wrh = wrh_ref[...]
    logits = _dot_nt(wrh, uh) + _dot_nt(wrh, ul) + _dot_nt(wrl_ref[...], uh)
    scores = 1.0 / (1.0 + jnp.exp(-logits))
    gt_ref[...] = _route(scores, scores + rb_ref[:, 0:1])


def _outproj(y_ssm, y_att, wo_a, wo_b, x2, gate1, scale2, shift2, ln_g, ln_b, wr_hi, wr_lo, rbias, seq):
    t, d = x2.shape
    tm = 512
    tpb = seq // tm
    half = y_ssm.shape[1]
    rows = lambda width: pl.BlockSpec((tm, width), lambda i: (i, 0))
    full = lambda shape: pl.BlockSpec(shape, lambda i: (0,) * len(shape))
    mod = lambda: pl.BlockSpec((1, 1, d), lambda i: (i // tpb, 0, 0))
    return pl.pallas_call(
        _outproj_kernel,
        grid=(t // tm,),
        in_specs=[rows(half), rows(half), full((half, d)), full((half, d)), rows(d), mod(), mod(), mod(),
                  full((1, d)), full((1, d)), full((N_EXPERTS, d)), full((N_EXPERTS, d)), full((N_EXPERTS, LANES))],
        out_specs=[rows(d), rows(d), pl.BlockSpec((N_EXPERTS, tm), lambda i: (0, i))],
        out_shape=[jax.ShapeDtypeStruct((t, d), F32), jax.ShapeDtypeStruct((t, d), BF16),
                   jax.ShapeDtypeStruct((N_EXPERTS, t), F32)],
        compiler_params=_cparams(("parallel",), 48),
        name="outproj",
    )(y_ssm, y_att, wo_a, wo_b, x2, gate1, scale2, shift2, ln_g[None, :], ln_b[None, :], wr_hi, wr_lo, rbias)


MOE_TILE = 256
MOE_UNIT = 16
MOE_BM = 512
MOE_SUB = 128
MOE_RCAP = TOP_K * MOE_TILE + N_EXPERTS * MOE_UNIT
MOE_RCH = 512
MOE_GW = 3 * LANES
MOE_UNSET = 1e9


def _moe_plan(gates_t, n_tiles):
    upb = MOE_BM // MOE_UNIT
    ups = MOE_SUB // MOE_UNIT
    cnt = jnp.sum((gates_t > 0.0).reshape(N_EXPERTS, n_tiles, MOE_TILE), axis=2, dtype=I32).T
    nun = (cnt + MOE_UNIT - 1) // MOE_UNIT
    lend = jnp.cumsum(nun, axis=1)
    loff = lend - nun
    tot = jnp.sum(nun, axis=0)
    tot_sub = (tot + ups - 1) // ups * ups
    nb = (tot + upb - 1) // upb
    cum_nb = jnp.cumsum(nb)
    first_blk = cum_nb - nb
    base = first_blk * upb
    goff = base[None, :] + jnp.cumsum(nun, axis=0) - nun
    unit = jnp.arange(MOE_RCAP // MOE_UNIT, dtype=I32)
    shift = goff - loff
    step_up = jnp.concatenate([shift[:, :1], shift[:, 1:] - shift[:, :-1]], axis=1)
    started = jnp.concatenate([jnp.ones((n_tiles, unit.shape[0], 1), jnp.bool_),
                               lend[:, None, :-1] <= unit[None, :, None]], axis=2)
    unit_dst = unit[None, :] + jnp.sum(jnp.where(started, step_up[:, None, :], 0), axis=2)
    n_blocks = (TOP_K * n_tiles * MOE_TILE // MOE_UNIT + n_tiles * N_EXPERTS + N_EXPERTS * (upb - 1) + upb - 1) // upb
    step = jnp.arange(n_blocks, dtype=I32)
    step_e = jnp.minimum(jnp.sum((cum_nb[None, :] <= step[:, None]).astype(I32), axis=1), N_EXPERTS - 1)
    of_step = step_e[:, None] == jnp.arange(N_EXPERTS, dtype=I32)[None, :]
    pick = lambda per_expert: jnp.sum(jnp.where(of_step, per_expert[None, :], 0), axis=1)
    nb_e = jnp.maximum(pick(nb), 1)
    first_e = pick(first_blk)
    local = (step - first_e + nb_e - 1) % nb_e
    step_rows = jnp.clip(pick(tot_sub) - local * upb, 0, upb) * MOE_UNIT
    end = pick(cum_nb)
    next_e = jnp.sum((cum_nb[None, :] <= end[:, None]).astype(I32), axis=1)
    step_next_e = jnp.where(end < cum_nb[-1], jnp.minimum(next_e, N_EXPERTS - 1), -1)
    loff_v = jnp.broadcast_to((loff * MOE_UNIT).astype(F32)[:, :, None], (n_tiles, N_EXPERTS, LANES))
    flat = lambda v: v.reshape(-1).astype(I32)
    return dict(unit_dst=flat(unit_dst), used=flat(lend[:, -1]), padstart=flat(base + tot), padn=flat(tot_sub - tot),
                step_e=flat(step_e), step_blk=flat(first_e + local), step_rows=flat(step_rows),
                step_next_e=flat(step_next_e),
                nb_used=cum_nb[-1:].astype(I32), loff_v=loff_v, n_blocks=n_blocks)


def _unit_rows(unit):
    return pl.ds(pl.multiple_of(unit * MOE_UNIT, MOE_UNIT), MOE_UNIT)


def _one_hot_rows(row_id, targets, axis):
    p = jnp.zeros(row_id.shape, F32)
    for k in range(TOP_K):
        tgt = targets[k:k + 1, :] if axis == 0 else targets[:, k:k + 1]
        p = jnp.where(row_id == tgt, 1.0, p)
    return p.astype(BF16)


def _dispatch_kernel(ud_ref, used_ref, pst_ref, pnn_ref, u_ref, g_ref, lv_ref, tri_ref,
                     xs_ref, lkt_ref, sorted_ref, xaug_ref, zero_ref, sem):
    j = pl.program_id(0)
    g = g_ref[...]
    sel = g > 0.0
    rank = _dot(jnp.where(sel, 1.0, 0.0).astype(BF16), tri_ref[...])
    loff = jnp.concatenate([lv_ref[0]] * (MOE_TILE // LANES), axis=1)
    rem = jnp.where(sel, loff + rank, MOE_UNSET)
    rows = []
    for _ in range(TOP_K):
        cur = jnp.min(rem, axis=0, keepdims=True)
        rows.append(cur)
        rem = jnp.where(rem == cur, MOE_UNSET, rem)
    lk = jnp.concatenate(rows, axis=0)
    lkt_ref[0] = jnp.concatenate([lk, jnp.full((LANES - TOP_K, MOE_TILE), MOE_UNSET, F32)], axis=0).T

    gt = jnp.concatenate([g, jnp.zeros_like(g)], axis=0).T
    hi = gt.astype(BF16)
    r1 = gt - hi.astype(F32)
    mid = r1.astype(BF16)
    xaug_ref[:, :D_MODEL] = u_ref[...]
    xaug_ref[:, D_MODEL:D_MODEL + LANES] = hi
    xaug_ref[:, D_MODEL + LANES:D_MODEL + 2 * LANES] = mid
    xaug_ref[:, D_MODEL + 2 * LANES:] = (r1 - mid.astype(F32)).astype(BF16)

    used_units = used_ref[j]
    upc = MOE_RCH // MOE_UNIT

    def unit_copy(src_unit, dst_unit):
        return pltpu.make_async_copy(sorted_ref.at[_unit_rows(src_unit)], xs_ref.at[_unit_rows(dst_unit)], sem)

    def sort_chunk(rc, carry):
        r0 = pl.multiple_of(rc * MOE_RCH, MOE_RCH)
        row_id = (lax.broadcasted_iota(I32, (MOE_RCH, MOE_TILE), 0) + r0).astype(F32)
        sorted_ref[pl.ds(r0, MOE_RCH), :] = _dot(_one_hot_rows(row_id, lk, 0), xaug_ref[...]).astype(BF16)

        def send_unit(uu, c):
            unit = rc * upc + uu
            unit_copy(unit, ud_ref[j * (MOE_RCAP // MOE_UNIT) + unit]).start()
            return c

        lax.fori_loop(0, jnp.clip(used_units - rc * upc, 0, upc), send_unit, 0)
        return carry

    lax.fori_loop(0, (used_units + upc - 1) // upc, sort_chunk, 0)

    def drain_unit(uu, c):
        unit_copy(0, 0).wait()
        return c

    lax.fori_loop(0, used_units, drain_unit, 0)

    @pl.when(j == pl.num_programs(0) - 1)
    def _():
        zero_ref[...] = jnp.zeros(zero_ref.shape, BF16)

        def pad_expert(e, carry):
            def pad_copy(uu):
                return pltpu.make_async_copy(zero_ref, xs_ref.at[_unit_rows(pst_ref[e] + uu)], sem)

            def start(uu, c):
                pad_copy(uu).start()
                return c

            def wait(uu, c):
                pad_copy(uu).wait()
                return c

            lax.fori_loop(0, pnn_ref[e], start, 0)
            lax.fori_loop(0, pnn_ref[e], wait, 0)
            return carry

        lax.fori_loop(0, N_EXPERTS, pad_expert, 0)


def _dispatch(u2, gates_t, plan):
    t, d = u2.shape
    n_tiles = t // MOE_TILE
    xw = d + MOE_GW
    tri = jnp.asarray(np.triu(np.ones((MOE_TILE, MOE_TILE), np.float32), k=1), BF16)
    grid_spec = pltpu.PrefetchScalarGridSpec(
        num_scalar_prefetch=4,
        grid=(n_tiles,),
        in_specs=[pl.BlockSpec((MOE_TILE, d), lambda j, *_: (j, 0)),
                  pl.BlockSpec((N_EXPERTS, MOE_TILE), lambda j, *_: (0, j)),
                  pl.BlockSpec((1, N_EXPERTS, LANES), lambda j, *_: (j, 0, 0)),
                  pl.BlockSpec((MOE_TILE, MOE_TILE), lambda j, *_: (0, 0))],
        out_specs=[pl.BlockSpec(memory_space=pl.ANY),
                   pl.BlockSpec((1, MOE_TILE, LANES), lambda j, *_: (j, 0, 0))],
        scratch_shapes=[pltpu.VMEM((MOE_RCAP, xw), BF16), pltpu.VMEM((MOE_TILE, xw), BF16),
                        pltpu.VMEM((MOE_UNIT, xw), BF16), pltpu.SemaphoreType.DMA(())])
    return pl.pallas_call(
        _dispatch_kernel,
        grid_spec=grid_spec,
        out_shape=[jax.ShapeDtypeStruct((plan["n_blocks"] * MOE_BM, xw), BF16),
                   jax.ShapeDtypeStruct((n_tiles, MOE_TILE, LANES), F32)],
        compiler_params=_cparams(("arbitrary",), 56),
        name="dispatch",
    )(plan["unit_dst"], plan["used"], plan["padstart"], plan["padn"], u2, gates_t, plan["loff_v"], tri)


def _experts_kernel(be_ref, sb_ref, br_ref, nx_ref, nbu_ref, x_ref, wg_hbm, wu_hbm, wd_hbm, y_ref,
                    wg_st, wu_st, wd_st, wgb_ref, wub_ref, wdb_ref, sems):
    b = pl.program_id(0)

    def weight_copies(e):
        return (pltpu.make_async_copy(wg_hbm.at[e], wg_st, sems.at[0]),
                pltpu.make_async_copy(wu_hbm.at[e], wu_st, sems.at[1]),
                pltpu.make_async_copy(wd_hbm.at[e], wd_st, sems.at[2]))

    @pl.when(b < nbu_ref[0])
    def _():
        e = be_ref[b]

        @pl.when(b == 0)
        def _():
            for cp in weight_copies(e):
                cp.start()

        @pl.when(jnp.logical_or(b == 0, e != be_ref[jnp.maximum(b - 1, 0)]))
        def _():
            for cp in weight_copies(e):
                cp.wait()
            wgb_ref[...] = wg_st[...].astype(BF16)
            wub_ref[...] = wu_st[...].astype(BF16)
            wdb_ref[...] = wd_st[...].astype(BF16)

            @pl.when(nx_ref[b] >= 0)
            def _():
                for cp in weight_copies(nx_ref[b]):
                    cp.start()

        for rows in range(MOE_SUB, MOE_BM + 1, MOE_SUB):
            @pl.when(br_ref[b] == rows)
            def _(rows=rows):
                x = x_ref[:rows, :D_MODEL]
                g3 = ((x_ref[:rows, D_MODEL:D_MODEL + LANES].astype(F32)
                       + x_ref[:rows, D_MODEL + LANES:D_MODEL + 2 * LANES].astype(F32))
                      + x_ref[:rows, D_MODEL + 2 * LANES:].astype(F32))
                lane = lax.broadcasted_iota(I32, g3.shape, 1)
                gate = jnp.sum(jnp.where(lane == e, g3, 0.0), axis=1, keepdims=True)
                hid = (_silu(_dot(x, wgb_ref[...])) * _dot(x, wub_ref[...]) * gate).astype(BF16)
                y_ref[:rows, :] = _dot(hid, wdb_ref[...]).astype(y_ref.dtype)


def _experts(xs, plan, wg, wu, wd):
    ns, xw = xs.shape
    _, d, f = wg.shape
    step = lambda b, nbu: jnp.minimum(b, nbu[0] - 1)
    rows_map = lambda b, be, sb, br, nx, nbu: (sb[step(b, nbu)], 0)
    hbm = pl.BlockSpec(memory_space=pl.ANY)
    grid_spec = pltpu.PrefetchScalarGridSpec(
        num_scalar_prefetch=5,
        grid=(ns // MOE_BM,),
        in_specs=[pl.BlockSpec((MOE_BM, xw), rows_map), hbm, hbm, hbm],
        out_specs=pl.BlockSpec((MOE_BM, d), rows_map),
        scratch_shapes=[pltpu.VMEM((d, f), F32), pltpu.VMEM((d, f), F32), pltpu.VMEM((f, d), F32),
                        pltpu.VMEM((d, f), BF16), pltpu.VMEM((d, f), BF16), pltpu.VMEM((f, d), BF16),
                        pltpu.SemaphoreType.DMA((3,))])
    return pl.pallas_call(
        _experts_kernel,
        grid_spec=grid_spec,
        out_shape=jax.ShapeDtypeStruct((ns, d), BF16),
        compiler_params=_cparams(("arbitrary",), 48),
        name="experts",
    )(plan["step_e"], plan["step_blk"], plan["step_rows"], plan["step_next_e"], plan["nb_used"], xs, wg, wu, wd)


def _final_kernel(ud_ref, used_ref, lkt_ref, u_ref, wg_ref, wu_ref, wd_ref, x1_ref, g2_ref, lg_ref, lb_ref,
                  ys_ref, o_ref, ybuf_ref, sems):
    j = pl.program_id(0)
    used_units = used_ref[j]
    upc = MOE_RCH // MOE_UNIT

    @pl.when(j == 0)
    def _():
        ybuf_ref[...] = jnp.zeros(ybuf_ref.shape, ybuf_ref.dtype)

    def unit_copy(src_unit, dst_unit, chunk):
        return pltpu.make_async_copy(ys_ref.at[_unit_rows(src_unit)], ybuf_ref.at[_unit_rows(dst_unit)],
                                     sems.at[chunk])

    n_chunks = (used_units + upc - 1) // upc

    def fetch_chunk(rc, carry):
        def fetch_unit(uu, c):
            unit = rc * upc + uu
            unit_copy(ud_ref[j * (MOE_RCAP // MOE_UNIT) + unit], unit, rc).start()
            return c

        lax.fori_loop(0, jnp.clip(used_units - rc * upc, 0, upc), fetch_unit, 0)
        return carry

    lax.fori_loop(0, n_chunks, fetch_chunk, 0)

    u = u_ref[...]
    hid = (_silu(_dot(u, wg_ref[...])) * _dot(u, wu_ref[...])).astype(BF16)
    o_ref[...] = _dot(hid, wd_ref[...])

    lkt = lkt_ref[0]

    def combine_chunk(rc, carry):
        def wait_unit(uu, c):
            unit_copy(0, 0, rc).wait()
            return c

        lax.fori_loop(0, jnp.clip(used_units - rc * upc, 0, upc), wait_unit, 0)
        r0 = pl.multiple_of(rc * MOE_RCH, MOE_RCH)
        row_id = (lax.broadcasted_iota(I32, (MOE_TILE, MOE_RCH), 1) + r0).astype(F32)
        o_ref[...] += _dot(_one_hot_rows(row_id, lkt, 1), ybuf_ref[pl.ds(r0, MOE_RCH), :])
        return carry

    lax.fori_loop(0, n_chunks, combine_chunk, 0)
    h = ALPHA * x1_ref[...] + g2_ref[0] * o_ref[...]
    o_ref[...] = _layer_norm_rows(h, lg_ref[...], lb_ref[...])


def _final(u2, wg_s, wu_s, wd_s, ys, lkt, plan, x1, gate2, ln_g, ln_b, seq):
    t, d = x1.shape
    tm = MOE_TILE
    tpb = seq // tm
    f = wg_s.shape[1]
    once = pl.Buffered(1)
    rows = lambda: pl.BlockSpec((tm, d), lambda j, *_: (j, 0))
    full = lambda shape: pl.BlockSpec(shape, lambda j, *_: (0,) * len(shape), pipeline_mode=once)
    grid_spec = pltpu.PrefetchScalarGridSpec(
        num_scalar_prefetch=2,
        grid=(t // tm,),
        in_specs=[pl.BlockSpec((1, tm, LANES), lambda j, *_: (j, 0, 0)), rows(),
                  full((d, f)), full((d, f)), full((f, d)), rows(),
                  pl.BlockSpec((1, 1, d), lambda j, *_: (j // tpb, 0, 0)), full((1, d)), full((1, d)),
                  pl.BlockSpec(memory_space=pl.ANY)],
        out_specs=rows(),
        scratch_shapes=[pltpu.VMEM((MOE_RCAP, d), BF16), pltpu.SemaphoreType.DMA((MOE_RCAP // MOE_RCH,))])
    return pl.pallas_call(
        _final_kernel,
        grid_spec=grid_spec,
        out_shape=jax.ShapeDtypeStruct((t, d), F32),
        compiler_params=_cparams(("arbitrary",), 58),
        name="final",
    )(plan["unit_dst"], plan["used"], lkt, u2, wg_s, wu_s, wd_s, x1, gate2, ln_g[None, :], ln_b[None, :], ys)


def _split_hi_lo(w):
    hi = w.astype(BF16)
    lo = (w - hi.astype(F32)).astype(BF16)
    return hi, lo


def _rope_tables(positions, dim, reps):
    inv_freq = 1.0 / (ROPE_THETA ** (jnp.arange(0, dim, 2, dtype=F32) / dim))
    ang = positions.astype(F32).reshape(-1)[:, None] * inv_freq
    cos, sin = jnp.cos(ang), jnp.sin(ang)
    return (jnp.tile(jnp.concatenate([cos, cos], axis=1), (1, reps)),
            jnp.tile(jnp.concatenate([-sin, sin], axis=1), (1, reps)))


def _layer(x2, c, positions, bsz, seq, w_ada, b_ada, w_in, conv_w, conv_b, dt_bias, a_log, d_skip, ssd_norm_w,
           idx_k_ln_g, idx_k_ln_b, w_out, ln1_g, ln1_b, w_router, router_bias, w_gate_e, w_up_e, w_down_e,
           w_gate_s, w_up_s, w_down_s, ln2_g, ln2_b, tables):
    d = D_MODEL
    mod = _ada(c, w_ada, b_ada).reshape(bsz, 6, 1, d)
    shift1, scale1, gate1, shift2, scale2, gate2 = [mod[:, k] for k in range(6)]

    o_z, o_xbc, o_dt, o_q = 0, D_SSM, 2 * D_SSM + 2 * SSM_GROUPS * SSM_STATE, 2 * D_SSM + 2 * SSM_GROUPS * SSM_STATE + SSM_HEADS
    o_k, o_v, o_qi = o_q + D_ATTN, o_q + 2 * D_ATTN, o_q + 3 * D_ATTN
    o_ki = o_qi + IDX_HEADS * IDX_DIM
    o_wi = o_ki + IDX_DIM
    w_t = w_in.T
    w_main = jnp.concatenate([w_t[o_z:o_z + D_SSM], w_t[o_xbc:o_xbc + D_SSM], w_t[o_q:o_ki],
                              w_t[o_xbc + D_SSM:o_dt]], axis=0).astype(BF16)
    w_tail = jnp.concatenate([w_t[o_dt:o_q], w_t[o_wi:], jnp.zeros((TAIL_KI - TAIL_WI - IDX_HEADS, d), F32),
                              w_t[o_ki:o_wi]], axis=0)
    wt_hi, wt_lo = _split_hi_lo(w_tail)

    proj, tail = _inproj(x2, scale1, shift1, w_main, wt_hi, wt_lo, seq)
    y_ssm = _ssm(proj, tail, conv_w, conv_b, dt_bias, a_log, d_skip, ssd_norm_w, bsz, seq)
    q_r, k_r, v_t, qi_r, ki0, ki1, w_s = _prep(proj, tail, *tables, idx_k_ln_g, idx_k_ln_b)
    y_att = _attn(q_r, k_r, v_t, qi_r, ki0, ki1, w_s, bsz, seq)

    wo = w_out.astype(BF16)
    wr_hi, wr_lo = _split_hi_lo(w_router.T)
    rbias = jnp.broadcast_to(router_bias[:, None], (N_EXPERTS, LANES))
    x1, u2, gates_t = _outproj(y_ssm, y_att, wo[:D_SSM], wo[D_SSM:], x2, gate1, scale2, shift2, ln1_g, ln1_b,
                               wr_hi, wr_lo, rbias, seq)
    plan = _moe_plan(gates_t, x2.shape[0] // MOE_TILE)
    xs, lkt = _dispatch(u2, gates_t, plan)
    ys = _experts(xs, plan, w_gate_e, w_up_e, w_down_e)
    return _final(u2, w_gate_s.astype(BF16), w_up_s.astype(BF16), w_down_s.astype(BF16), ys, lkt, plan, x1, gate2,
                  ln2_g, ln2_b, seq)


def kernel(x, c, positions, w_ada, b_ada, w_in, conv_w, conv_b, dt_bias, a_log, d_skip, ssd_norm_w, idx_k_ln_g, idx_k_ln_b, w_out, ln1_g, ln1_b, w_router, router_bias, w_gate_e, w_up_e, w_down_e, w_gate_s, w_up_s, w_down_s, ln2_g, ln2_b):
    bsz, seq, d = x.shape
    tables = _rope_tables(positions, ATTN_HEAD_DIM, 1) + _rope_tables(positions, IDX_DIM, 2)
    x2 = x.reshape(bsz * seq, d)
    for l in range(w_ada.shape[0]):
        x2 = _layer(x2, c, positions, bsz, seq, w_ada[l], b_ada[l], w_in[l], conv_w[l], conv_b[l], dt_bias[l],
                    a_log[l], d_skip[l], ssd_norm_w[l], idx_k_ln_g[l], idx_k_ln_b[l], w_out[l], ln1_g[l], ln1_b[l],
                    w_router[l], router_bias[l], w_gate_e[l], w_up_e[l], w_down_e[l], w_gate_s[l], w_up_s[l],
                    w_down_s[l], ln2_g[l], ln2_b[l], tables)
    return x2.reshape(bsz, seq, d)
```

```python
import functools
import math

import jax
import jax.numpy as jnp
import numpy as np
from jax import lax
from jax.experimental import pallas as pl
from jax.experimental.pallas import tpu as pltpu

F32 = jnp.float32
BF16 = jnp.bfloat16
I32 = jnp.int32
HIGHEST = lax.Precision.HIGHEST

D_MODEL = 2048
D_SSM = 1024
D_ATTN = 1024
SSM_HEAD_DIM = 64
SSM_HEADS = 16
SSM_GROUPS = 2
SSM_STATE = 128
CONV_WIDTH = 4
SSD_CHUNK = 128
ATTN_HEAD_DIM = 128
ATTN_HEADS = 8
IDX_HEADS = 16
IDX_DIM = 64
INDEX_TOPK = 256
ROPE_THETA = 10000.0
N_EXPERTS = 64
N_EXPERT_GROUPS = 8
EXPERTS_PER_GROUP = 8
TOPK_GROUPS = 4
TOP_K = 8
D_EXPERT = 512
D_SHARED = 512
ROUTED_SCALE = 2.5
DEPTH = 1
ALPHA = (2.0 * DEPTH) ** 0.25
LN_EPS = 1e-5

LANES = 128
SUBLANES = 8
VMEM_BYTES_V7X = 64 * 1024 * 1024
INT_MIN = -(2 ** 31)
KEY_LOWEST_FINITE = INT_MIN + 0x00800000

COL_Z, COL_XS, COL_Q, COL_K, COL_V, COL_QI, COL_BC = 0, 1024, 2048, 3072, 4096, 5120, 6144
N_MAIN = 6656
N_TAIL = LANES
TAIL_WI = SSM_HEADS
TAIL_KI = LANES - IDX_DIM


def _cparams(sem, vmem_mb):
    return pltpu.CompilerParams(dimension_semantics=sem, vmem_limit_bytes=vmem_mb * 1024 * 1024)


def _silu(v):
    return 0.5 * v * (1.0 + jnp.tanh(0.5 * v))


def _dot(a, b, precision=None):
    return jnp.dot(a, b, preferred_element_type=F32, precision=precision)


def _dot_nt(a, b, precision=None):
    return lax.dot_general(a, b, (((1,), (1,)), ((), ())), preferred_element_type=F32, precision=precision)


def _split3(x):
    hi = x.astype(BF16)
    r1 = x - hi.astype(F32)
    mid = r1.astype(BF16)
    return hi, mid, (r1 - mid.astype(F32)).astype(BF16)


def _layer_norm_rows(h, g, b):
    mu = jnp.mean(h, axis=-1, keepdims=True)
    d = h - mu
    var = jnp.mean(d * d, axis=-1, keepdims=True)
    return d * lax.rsqrt(var + LN_EPS) * g + b


def _ada_kernel(cb_ref, w_ref, b_ref, o_ref):
    tn = w_ref.shape[1]
    for bi in range(cb_ref.shape[0]):
        cb = _silu(cb_ref[bi])
        cols = [jnp.sum(w_ref[:, j * LANES:(j + 1) * LANES] * cb, axis=0, keepdims=True)
                for j in range(tn // LANES)]
        o_ref[bi:bi + 1, :] = jnp.concatenate(cols, axis=1) + b_ref[...]


def _ada(c, w_ada, b_ada):
    bsz, d = c.shape
    n = w_ada.shape[1]
    tn = 1024
    cb = jnp.broadcast_to(c[:, :, None], (bsz, d, LANES))
    return pl.pallas_call(
        _ada_kernel,
        grid=(n // tn,),
        in_specs=[pl.BlockSpec((bsz, d, LANES), lambda j: (0, 0, 0)),
                  pl.BlockSpec((d, tn), lambda j: (0, j)),
                  pl.BlockSpec((1, tn), lambda j: (0, j))],
        out_specs=pl.BlockSpec((bsz, tn), lambda j: (0, j)),
        out_shape=jax.ShapeDtypeStruct((bsz, n), F32),
        compiler_params=_cparams(("parallel",), 40),
        name="ada",
    )(cb, w_ada, b_ada.reshape(1, n))


def _inproj_kernel(x_ref, sc_ref, sh_ref, w_ref, wth_ref, wtl_ref, o_ref, t_ref, u_ref):
    @pl.when(pl.program_id(1) == 0)
    def _():
        u = x_ref[...] * (1.0 + sc_ref[0]) + sh_ref[0]
        uh = u.astype(BF16)
        ul = (u - uh.astype(F32)).astype(BF16)
        u_ref[...] = uh
        t_ref[...] = (_dot_nt(uh, wth_ref[...]) + _dot_nt(uh, wtl_ref[...]) + _dot_nt(ul, wth_ref[...]))

    o_ref[...] = _dot_nt(u_ref[...], w_ref[...])


def _inproj_weight_row(j, tn):
    n_lead = (COL_Q - COL_Z) // tn
    n_mid = (COL_BC - COL_Q) // tn
    mid0 = COL_Q + 2 * SSM_GROUPS * SSM_STATE + SSM_HEADS
    return jnp.where(j < n_lead, j * tn,
                     jnp.where(j < n_lead + n_mid, mid0 + (j - n_lead) * tn, COL_Q + (j - n_lead - n_mid) * tn))


def _inproj(x2, scale1, shift1, w_main, wt_hi, wt_lo, seq):
    t, d = x2.shape
    tm, tn = 1024, 512
    tpb = seq // tm
    return pl.pallas_call(
        _inproj_kernel,
        grid=(t // tm, N_MAIN // tn),
        in_specs=[pl.BlockSpec((tm, d), lambda i, j: (i, 0)),
                  pl.BlockSpec((1, 1, d), lambda i, j: (i // tpb, 0, 0)),
                  pl.BlockSpec((1, 1, d), lambda i, j: (i // tpb, 0, 0)),
                  pl.BlockSpec((pl.Element(tn), pl.Element(d)), lambda i, j: (pl.multiple_of(_inproj_weight_row(j, tn), 16), 0)),
                  pl.BlockSpec((N_TAIL, d), lambda i, j: (0, 0)),
                  pl.BlockSpec((N_TAIL, d), lambda i, j: (0, 0))],
        out_specs=[pl.BlockSpec((tm, tn), lambda i, j: (i, j)),
                   pl.BlockSpec((tm, N_TAIL), lambda i, j: (i, 0))],
        out_shape=[jax.ShapeDtypeStruct((t, N_MAIN), F32),
                   jax.ShapeDtypeStruct((t, N_TAIL), F32)],
        scratch_shapes=[pltpu.VMEM((tm, d), BF16)],
        compiler_params=_cparams(("parallel", "arbitrary"), 48),
        name="inproj",
    )(x2, scale1, shift1, w_main, wt_hi, wt_lo)


def _ssm_kernel(z_ref, xs_ref, bc_ref, dt_ref, cwx_ref, cbx_ref, cwb_ref, cbb_ref, dtb_ref, alog_ref,
                dsk_ref, nw_ref, e_ref, e2_ref, tril_ref, o_ref, px_ref, pb_ref, st_ref):
    q = SSD_CHUNK
    hpg = SSM_HEADS // SSM_GROUPS
    gw = hpg * SSM_HEAD_DIM

    @pl.when(pl.program_id(1) == 0)
    def _():
        px_ref[0:SUBLANES, :] = jnp.zeros((SUBLANES, px_ref.shape[1]), F32)
        pb_ref[0:SUBLANES, :] = jnp.zeros((SUBLANES, pb_ref.shape[1]), F32)
        st_ref[...] = jnp.zeros(st_ref.shape, F32)

    def conv_silu(raw_ref, pad_ref, w_ref, b_ref):
        pad_ref[SUBLANES:SUBLANES + q, :] = raw_ref[...]
        acc = b_ref[...] + w_ref[0:1, :] * pad_ref[SUBLANES - 3:SUBLANES - 3 + q, :]
        for k in range(1, CONV_WIDTH):
            acc = acc + w_ref[k:k + 1, :] * pad_ref[SUBLANES - 3 + k:SUBLANES - 3 + k + q, :]
        pad_ref[0:SUBLANES, :] = raw_ref[q - SUBLANES:q, :]
        return _silu(acc)

    xs = conv_silu(xs_ref, px_ref, cwx_ref, cbx_ref)
    bc = conv_silu(bc_ref, pb_ref, cwb_ref, cbb_ref)

    dtr = dt_ref[...] + dtb_ref[...]
    dt = jnp.maximum(dtr, 0.0) + jnp.log(1.0 + jnp.exp(-jnp.abs(dtr)))
    log_a = dt * (-jnp.exp(alog_ref[...]))
    cs = sum(_dot(tril_ref[...], part) for part in _split3(log_a))
    cs_parts = _split3(cs)
    cs_e = sum(_dot(part, e_ref[...]) for part in cs_parts)
    dt_e = sum(_dot(part, e_ref[...]) for part in _split3(dt))
    cs_col = sum(_dot(part, e2_ref[...]) for part in cs_parts)
    cs_t = cs.T
    cs_last = cs_e[q - 1:q, :]

    xdt = xs * dt_e
    rows = lax.broadcasted_iota(I32, (q, q), 0)
    cols = lax.broadcasted_iota(I32, (q, q), 1)
    causal = rows >= cols
    first_half = lax.broadcasted_iota(I32, (q, LANES), 1) < SSM_HEAD_DIM

    y_pairs = []
    for g in range(SSM_GROUPS):
        b_g = bc[:, g * SSM_STATE:(g + 1) * SSM_STATE]
        c_g = bc[:, (SSM_GROUPS + g) * SSM_STATE:(SSM_GROUPS + g + 1) * SSM_STATE]
        cb = _dot_nt(c_g.astype(BF16), b_g.astype(BF16))
        for hp in range(hpg // 2):
            pair = g * (hpg // 2) + hp
            x_pair = xdt[:, pair * LANES:(pair + 1) * LANES].astype(BF16)
            ys = []
            for sub in range(2):
                h = 2 * pair + sub
                seg = cs_col[:, h * LANES:(h + 1) * LANES] - cs_t[h:h + 1, :]
                dec = jnp.exp(jnp.where(causal, seg, -jnp.inf))
                ys.append(_dot((cb * dec).astype(BF16), x_pair))
            y_pairs.append(jnp.where(first_half, ys[0], ys[1]))
    y_diag = jnp.concatenate(y_pairs, axis=1)

    xw = xdt * jnp.exp(cs_last - cs_e)
    y_off, new_states = [], []
    for g in range(SSM_GROUPS):
        b_g = bc[:, g * SSM_STATE:(g + 1) * SSM_STATE]
        c_g = bc[:, (SSM_GROUPS + g) * SSM_STATE:(SSM_GROUPS + g + 1) * SSM_STATE]
        h_in = st_ref[:, g * gw:(g + 1) * gw]
        y_off.append(_dot(c_g.astype(BF16), h_in.astype(BF16)))
        new_states.append(_dot(b_g.T.astype(BF16), xw[:, g * gw:(g + 1) * gw].astype(BF16)))
    y_off = jnp.concatenate(y_off, axis=1) * jnp.exp(cs_e)
    st_ref[...] = jnp.exp(cs_last) * st_ref[...] + jnp.concatenate(new_states, axis=1)

    y = y_diag + y_off + dsk_ref[...] * xs
    yf = y * _silu(z_ref[...])
    ms = jnp.mean(yf * yf, axis=-1, keepdims=True)
    o_ref[...] = (yf * lax.rsqrt(ms + LN_EPS) * nw_ref[...]).astype(o_ref.dtype)


def _ssm(proj, tail, conv_w, conv_b, dt_bias, a_log, d_skip, ssd_norm_w, bsz, seq):
    t = proj.shape[0]
    q = SSD_CHUNK
    n_c = seq // q
    nbc = 2 * SSM_GROUPS * SSM_STATE
    cw_x, cw_b = conv_w[:, :D_SSM], conv_w[:, D_SSM:]
    cb_x, cb_b = conv_b[None, :D_SSM], conv_b[None, D_SSM:]
    pad16 = lambda v: jnp.pad(v, (0, LANES - SSM_HEADS))[None, :]
    head_of_lane = np.arange(D_SSM) // SSM_HEAD_DIM
    e_mat = jnp.asarray((np.arange(LANES)[:, None] == head_of_lane[None, :]).astype(np.float32), BF16)
    e2_mat = jnp.asarray((np.arange(LANES)[:, None] == (np.arange(SSM_HEADS * LANES) // LANES)[None, :])
                         .astype(np.float32), BF16)
    tril = jnp.asarray(np.tril(np.ones((q, q), np.float32)), BF16)
    row = lambda b, c: b * n_c + c
    full = lambda shape: pl.BlockSpec(shape, lambda b, c: (0,) * len(shape))
    return pl.pallas_call(
        _ssm_kernel,
        grid=(bsz, n_c),
        in_specs=[pl.BlockSpec((q, D_SSM), lambda b, c: (row(b, c), COL_Z // D_SSM)),
                  pl.BlockSpec((q, D_SSM), lambda b, c: (row(b, c), COL_XS // D_SSM)),
                  pl.BlockSpec((q, nbc), lambda b, c: (row(b, c), COL_BC // nbc)),
                  pl.BlockSpec((q, LANES), lambda b, c: (row(b, c), 0)),
                  full((CONV_WIDTH, D_SSM)), full((1, D_SSM)), full((CONV_WIDTH, nbc)), full((1, nbc)),
                  full((1, LANES)), full((1, LANES)), full((1, D_SSM)), full((1, D_SSM)),
                  full((LANES, D_SSM)), full((LANES, SSM_HEADS * LANES)), full((q, q))],
        out_specs=pl.BlockSpec((q, D_SSM), lambda b, c: (row(b, c), 0)),
        out_shape=jax.ShapeDtypeStruct((t, D_SSM), BF16),
        scratch_shapes=[pltpu.VMEM((SUBLANES + q, D_SSM), F32),
                        pltpu.VMEM((SUBLANES + q, nbc), F32),
                        pltpu.VMEM((SSM_STATE, D_SSM), F32)],
        compiler_params=_cparams(("parallel", "arbitrary"), 40),
        name="ssm",
    )(proj, proj, proj, tail, cw_x, cb_x, cw_b, cb_b, pad16(dt_bias), pad16(a_log),
      jnp.repeat(d_skip, SSM_HEAD_DIM)[None, :], ssd_norm_w[None, :], e_mat, e2_mat, tril)


def _prep_kernel(q_ref, k_ref, v_ref, qi_ref, t_ref, ca_ref, sa_ref, ci_ref, si_ref, lg_ref, lb_ref,
                 qo_ref, ko_ref, vo_ref, qio_ref, ki0_ref, ki1_ref, wo_ref):
    ca, sa, ci, si = ca_ref[...], sa_ref[...], ci_ref[...], si_ref[...]
    lane = lax.broadcasted_iota(I32, ca.shape, 1)
    first32 = (lane % IDX_DIM) < (IDX_DIM // 2)
    q_scale = ATTN_HEAD_DIM ** -0.5 * math.log2(math.e)

    def rope_attn(v):
        return v * ca + pltpu.roll(v, ATTN_HEAD_DIM // 2, 1) * sa

    def rope_idx(v):
        rot = jnp.where(first32, pltpu.roll(v, LANES - IDX_DIM // 2, 1), pltpu.roll(v, IDX_DIM // 2, 1))
        return v * ci + rot * si

    for h in range(ATTN_HEADS):
        sl = slice(h * LANES, (h + 1) * LANES)
        qo_ref[:, sl] = (rope_attn(q_ref[:, sl]) * q_scale).astype(BF16)
        ko_ref[:, sl] = rope_attn(k_ref[:, sl]).astype(BF16)
    vo_ref[0] = v_ref[...].T.astype(BF16)
    for p in range(IDX_HEADS * IDX_DIM // LANES):
        sl = slice(p * LANES, (p + 1) * LANES)
        qio_ref[:, sl] = rope_idx(qi_ref[:, sl]).astype(BF16)

    tail = t_ref[...]
    valid = lane < IDX_DIM
    kraw = jnp.where(valid, pltpu.roll(tail, LANES - TAIL_KI, 1), 0.0)
    mu = jnp.sum(kraw, axis=-1, keepdims=True) * (1.0 / IDX_DIM)
    dk = jnp.where(valid, kraw - mu, 0.0)
    var = jnp.sum(dk * dk, axis=-1, keepdims=True) * (1.0 / IDX_DIM)
    kn = jnp.where(valid, dk * lax.rsqrt(var + LN_EPS) * lg_ref[...] + lb_ref[...], 0.0)
    kr = jnp.where(valid, rope_idx(kn), 0.0)
    ki0_ref[...] = kr.astype(BF16)
    ki1_ref[...] = pltpu.roll(kr, IDX_DIM, 1).astype(BF16)
    wo_ref[...] = pltpu.roll(tail, LANES - TAIL_WI, 1) * (IDX_HEADS ** -0.5 * IDX_DIM ** -0.5)


def _prep(proj, tail, cos_a, sin_a, cos_i, sin_i, ln_g, ln_b):
    t = proj.shape[0]
    tm = ATT_KC
    w = D_ATTN
    pad64 = lambda v: jnp.pad(v, (0, LANES - IDX_DIM))[None, :]
    col = lambda c: pl.BlockSpec((tm, w), lambda i: (i, c // w))
    lane_blk = lambda c: pl.BlockSpec((tm, LANES), lambda i: (i, c))
    full = lambda: pl.BlockSpec((1, LANES), lambda i: (0, 0))
    return pl.pallas_call(
        _prep_kernel,
        grid=(t // tm,),
        in_specs=[col(COL_Q), col(COL_K), col(COL_V), col(COL_QI), lane_blk(0),
                  lane_blk(0), lane_blk(0), lane_blk(0), lane_blk(0), full(), full()],
        out_specs=[pl.BlockSpec((tm, w), lambda i: (i, 0))] * 2 + [pl.BlockSpec((1, w, tm), lambda i: (i, 0, 0))]
        + [pl.BlockSpec((tm, w), lambda i: (i, 0))] + [pl.BlockSpec((tm, LANES), lambda i: (i, 0))] * 3,
        out_shape=[jax.ShapeDtypeStruct((t, w), BF16)] * 2 + [jax.ShapeDtypeStruct((t // tm, w, tm), BF16)]
        + [jax.ShapeDtypeStruct((t, w), BF16)]
        + [jax.ShapeDtypeStruct((t, LANES), BF16)] * 2 + [jax.ShapeDtypeStruct((t, LANES), F32)],
        compiler_params=_cparams(("parallel",), 48),
        name="prep",
    )(proj, proj, proj, proj, tail, cos_a, sin_a, cos_i, sin_i, pad64(ln_g), pad64(ln_b))


ATT_QB = 128
ATT_KC = 512
ATT_KS = 256


def _attn_kernel(q_ref, k_ref, vt_ref, qi_ref, ki0_ref, ki1_ref, w_ref, o_ref, sc_ref, acc_ref, *, topk):
    i = pl.program_id(1)
    n_ch = (i * ATT_QB + ATT_QB + ATT_KC - 1) // ATT_KC
    key_off = lax.broadcasted_iota(I32, (ATT_KC, ATT_QB), 0)
    q_pos = i * ATT_QB + lax.broadcasted_iota(I32, (ATT_KC, ATT_QB), 1)
    w_t = w_ref[...].T

    def score_chunk(c, carry):
        start = pl.multiple_of(c * ATT_KC, ATT_KC)
        ki0 = ki0_ref[pl.ds(start, ATT_KC), :]
        ki1 = ki1_ref[pl.ds(start, ATT_KC), :]
        acc = jnp.zeros((ATT_KC, ATT_QB), F32)
        for h in range(IDX_HEADS):
            pair = qi_ref[:, (h // 2) * LANES:(h // 2 + 1) * LANES]
            rel = _dot_nt(ki0 if h % 2 == 0 else ki1, pair)
            acc = acc + jnp.maximum(rel, 0.0) * w_t[h:h + 1, :]
        sc_ref[c] = jnp.where(start + key_off <= q_pos, acc, -jnp.inf)
        return carry

    lax.fori_loop(0, n_ch, score_chunk, 0)

    def key_to_float(key):
        return pltpu.bitcast(jnp.where(key < 0, key ^ jnp.int32(0x7FFFFFFF), key), F32)

    def search_bit(it, ans):
        cand = ans ^ lax.shift_left(jnp.int32(1), jnp.int32(31) - it)
        cand_f = key_to_float(cand)

        def count_chunk(c, cnt):
            part = jnp.where(sc_ref[c] >= cand_f, 1, 0).astype(I32)
            return cnt + jnp.sum(part.reshape(ATT_KC // SUBLANES, SUBLANES, ATT_QB), axis=0)

        cnt = lax.fori_loop(0, n_ch, count_chunk, jnp.zeros((SUBLANES, ATT_QB), I32))
        total = jnp.sum(cnt, axis=0, keepdims=True)
        return jnp.where(total >= topk, cand, ans)

    n_bits = jnp.where((i + 1) * ATT_QB <= topk, 0, 32)
    kth = lax.fori_loop(0, n_bits, search_bit, jnp.full((1, ATT_QB), INT_MIN, I32))
    lowest_finite = jnp.float32(np.finfo(np.float32).min)
    thr = jnp.where(kth < KEY_LOWEST_FINITE, lowest_finite, key_to_float(kth))

    acc_ref[...] = jnp.zeros(acc_ref.shape, F32)

    def att_chunk(c, carry):
        m_all, l_all = carry
        start = pl.multiple_of(c * ATT_KC, ATT_KC)
        for part in range(ATT_KC // ATT_KS):
            ks = slice(part * ATT_KS, (part + 1) * ATT_KS)
            sel = sc_ref[c, ks, :] >= thr
            m_rows, l_rows = [], []
            for h in range(ATTN_HEADS):
                sl = slice(h * LANES, (h + 1) * LANES)
                s = _dot_nt(k_ref[pl.ds(start + part * ATT_KS, ATT_KS), sl], q_ref[:, sl])
                s = jnp.where(sel, s, -jnp.inf)
                m_old = m_all[h:h + 1, :]
                m_new = jnp.maximum(m_old, jnp.max(s, axis=0, keepdims=True))
                p = jnp.exp2(s - m_new)
                alpha = jnp.exp2(m_old - m_new)
                l_rows.append(alpha * l_all[h:h + 1, :] + jnp.sum(p, axis=0, keepdims=True))
                m_rows.append(m_new)
                acc_ref[h] = alpha * acc_ref[h] + _dot(vt_ref[c, sl, ks], p.astype(BF16))
            m_all = jnp.concatenate(m_rows, axis=0)
            l_all = jnp.concatenate(l_rows, axis=0)
        return m_all, l_all

    init = (jnp.full((ATTN_HEADS, ATT_QB), -1e30, F32), jnp.zeros((ATTN_HEADS, ATT_QB), F32))
    _, l_all = lax.fori_loop(0, n_ch, att_chunk, init)
    for h in range(ATTN_HEADS):
        o_ref[:, h * LANES:(h + 1) * LANES] = (acc_ref[h] / l_all[h:h + 1, :]).T.astype(o_ref.dtype)


def _attn(q_r, k_r, v_t, qi_r, ki0, ki1, w_s, bsz, seq):
    t = q_r.shape[0]
    nq = seq // ATT_QB
    n_kc = seq // ATT_KC
    topk = min(INDEX_TOPK, seq // 4)
    once = pl.Buffered(1)
    qblk = lambda width: pl.BlockSpec((ATT_QB, width), lambda b, i: (b * nq + i, 0))
    per_batch = lambda width: pl.BlockSpec((seq, width), lambda b, i: (b, 0), pipeline_mode=once)
    return pl.pallas_call(
        functools.partial(_attn_kernel, topk=topk),
        grid=(bsz, nq),
        in_specs=[qblk(D_ATTN), per_batch(D_ATTN),
                  pl.BlockSpec((n_kc, D_ATTN, ATT_KC), lambda b, i: (b, 0, 0), pipeline_mode=once),
                  qblk(IDX_HEADS * IDX_DIM), per_batch(LANES), per_batch(LANES), qblk(LANES)],
        out_specs=qblk(D_ATTN),
        out_shape=jax.ShapeDtypeStruct((t, D_ATTN), BF16),
        scratch_shapes=[pltpu.VMEM((n_kc, ATT_KC, ATT_QB), F32),
                        pltpu.VMEM((ATTN_HEADS, ATTN_HEAD_DIM, ATT_QB), F32)],
        compiler_params=_cparams(("parallel", "arbitrary"), 48),
        name="attn",
    )(q_r, k_r, v_t, qi_r, ki0, ki1, w_s)


def _first_index_of_max(v, iota, n):
    m = jnp.max(v, axis=0, keepdims=True)
    idx = jnp.min(jnp.where(v == m, iota, n), axis=0, keepdims=True)
    return m, idx


def _route(scores, biased):
    tm = scores.shape[1]
    neg_inf = jnp.float32(-jnp.inf)
    iota8 = lax.broadcasted_iota(I32, (EXPERTS_PER_GROUP, tm), 0)
    group_scores = []
    for g in range(N_EXPERT_GROUPS):
        v = biased[g * EXPERTS_PER_GROUP:(g + 1) * EXPERTS_PER_GROUP, :]
        m1, i1 = _first_index_of_max(v, iota8, EXPERTS_PER_GROUP)
        m2 = jnp.max(jnp.where(iota8 == i1, neg_inf, v), axis=0, keepdims=True)
        group_scores.append(m1 + m2)
    gs = jnp.concatenate(group_scores, axis=0)
    keep = jnp.zeros(gs.shape, jnp.bool_)
    for _ in range(TOPK_GROUPS):
        _, gi = _first_index_of_max(gs, iota8, N_EXPERT_GROUPS)
        hit = iota8 == gi
        keep = jnp.logical_or(keep, hit)
        gs = jnp.where(hit, neg_inf, gs)
    keep_f = jnp.where(keep, 1.0, 0.0)
    masked = jnp.concatenate(
        [jnp.where(keep_f[g:g + 1, :] > 0.0, biased[g * EXPERTS_PER_GROUP:(g + 1) * EXPERTS_PER_GROUP, :], neg_inf)
         for g in range(N_EXPERT_GROUPS)], axis=0)
    iota64 = lax.broadcasted_iota(I32, (N_EXPERTS, tm), 0)
    sel_w = jnp.zeros((N_EXPERTS, tm), F32)
    for _ in range(TOP_K):
        _, ei = _first_index_of_max(masked, iota64, N_EXPERTS)
        hit = iota64 == ei
        sel_w = jnp.where(hit, scores, sel_w)
        masked = jnp.where(hit, neg_inf, masked)
    denom = jnp.sum(sel_w, axis=0, keepdims=True)
    return sel_w / denom * ROUTED_SCALE


def _outproj_kernel(ys_ref, ya_ref, wa_ref, wb_ref, x_ref, g1_ref, sc2_ref, sh2_ref, lg_ref, lb_ref,
                    wrh_ref, wrl_ref, rb_ref, x1_ref, u2_ref, gt_ref):
    mix = _dot(ys_ref[...], wa_ref[...]) + _dot(ya_ref[...], wb_ref[...])
    h = ALPHA * x_ref[...] + g1_ref[0] * mix
    x1 = _layer_norm_rows(h, lg_ref[...], lb_ref[...])
    x1_ref[...] = x1
    u2 = x1 * (1.0 + sc2_ref[0]) + sh2_ref[0]
    uh = u2.astype(BF16)
    ul = (u2 - uh.astype(F32)).astype(BF16)
    u2_ref[...] = uh
    wrh = wrh_ref[...]
    logits = _dot_nt(wrh, uh) + _dot_nt(wrh, ul) + _dot_nt(wrl_ref[...], uh)
    scores = 1.0 / (1.0 + jnp.exp(-logits))
    gt_ref[...] = _route(scores, scores + rb_ref[:, 0:1])


def _outproj(y_ssm, y_att, wo_a, wo_b, x2, gate1, scale2, shift2, ln_g, ln_b, wr_hi, wr_lo, rbias, seq):
    t, d = x2.shape
    tm = 512
    tpb = seq // tm
    half = y_ssm.shape[1]
    rows = lambda width: pl.BlockSpec((tm, width), lambda i: (i, 0))
    full = lambda shape: pl.BlockSpec(shape, lambda i: (0,) * len(shape))
    mod = lambda: pl.BlockSpec((1, 1, d), lambda i: (i // tpb, 0, 0))
    return pl.pallas_call(
        _outproj_kernel,
        grid=(t // tm,),
        in_specs=[rows(half), rows(half), full((half, d)), full((half, d)), rows(d), mod(), mod(), mod(),
                  full((1, d)), full((1, d)), full((N_EXPERTS, d)), full((N_EXPERTS, d)), full((N_EXPERTS, LANES))],
        out_specs=[rows(d), rows(d), pl.BlockSpec((N_EXPERTS, tm), lambda i: (0, i))],
        out_shape=[jax.ShapeDtypeStruct((t, d), F32), jax.ShapeDtypeStruct((t, d), BF16),
                   jax.ShapeDtypeStruct((N_EXPERTS, t), F32)],
        compiler_params=_cparams(("parallel",), 48),
        name="outproj",
    )(y_ssm, y_att, wo_a, wo_b, x2, gate1, scale2, shift2, ln_g[None, :], ln_b[None, :], wr_hi, wr_lo, rbias)


MOE_TILE = 256
MOE_UNIT = 16
MOE_BM = 512
MOE_SUB = 128
MOE_RCAP = TOP_K * MOE_TILE + N_EXPERTS * MOE_UNIT
MOE_RCH = 512
MOE_GW = 3 * LANES
MOE_UNSET = 1e9


def _moe_plan(gates_t, n_tiles):
    upb = MOE_BM // MOE_UNIT
    ups = MOE_SUB // MOE_UNIT
    cnt = jnp.sum((gates_t > 0.0).reshape(N_EXPERTS, n_tiles, MOE_TILE), axis=2, dtype=I32).T
    nun = (cnt + MOE_UNIT - 1) // MOE_UNIT
    lend = jnp.cumsum(nun, axis=1)
    loff = lend - nun
    tot = jnp.sum(nun, axis=0)
    tot_sub = (tot + ups - 1) // ups * ups
    nb = (tot + upb - 1) // upb
    cum_nb = jnp.cumsum(nb)
    first_blk = cum_nb - nb
    base = first_blk * upb
    goff = base[None, :] + jnp.cumsum(nun, axis=0) - nun
    unit = jnp.arange(MOE_RCAP // MOE_UNIT, dtype=I32)
    shift = goff - loff
    step_up = jnp.concatenate([shift[:, :1], shift[:, 1:] - shift[:, :-1]], axis=1)
    started = jnp.concatenate([jnp.ones((n_tiles, unit.shape[0], 1), jnp.bool_),
                               lend[:, None, :-1] <= unit[None, :, None]], axis=2)
    unit_dst = unit[None, :] + jnp.sum(jnp.where(started, step_up[:, None, :], 0), axis=2)
    n_blocks = (TOP_K * n_tiles * MOE_TILE // MOE_UNIT + n_tiles * N_EXPERTS + N_EXPERTS * (upb - 1) + upb - 1) // upb
    step = jnp.arange(n_blocks, dtype=I32)
    step_e = jnp.minimum(jnp.sum((cum_nb[None, :] <= step[:, None]).astype(I32), axis=1), N_EXPERTS - 1)
    of_step = step_e[:, None] == jnp.arange(N_EXPERTS, dtype=I32)[None, :]
    pick = lambda per_expert: jnp.sum(jnp.where(of_step, per_expert[None, :], 0), axis=1)
    nb_e = jnp.maximum(pick(nb), 1)
    first_e = pick(first_blk)
    local = (step - first_e + nb_e - 1) % nb_e
    step_rows = jnp.clip(pick(tot_sub) - local * upb, 0, upb) * MOE_UNIT
    end = pick(cum_nb)
    next_e = jnp.sum((cum_nb[None, :] <= end[:, None]).astype(I32), axis=1)
    step_next_e = jnp.where(end < cum_nb[-1], jnp.minimum(next_e, N_EXPERTS - 1), -1)
    loff_v = jnp.broadcast_to((loff * MOE_UNIT).astype(F32)[:, :, None], (n_tiles, N_EXPERTS, LANES))
    flat = lambda v: v.reshape(-1).astype(I32)
    return dict(unit_dst=flat(unit_dst), used=flat(lend[:, -1]), padstart=flat(base + tot), padn=flat(tot_sub - tot),
                step_e=flat(step_e), step_blk=flat(first_e + local), step_rows=flat(step_rows),
                step_next_e=flat(step_next_e),
                nb_used=cum_nb[-1:].astype(I32), loff_v=loff_v, n_blocks=n_blocks)


def _unit_rows(unit):
    return pl.ds(pl.multiple_of(unit * MOE_UNIT, MOE_UNIT), MOE_UNIT)


def _one_hot_rows(row_id, targets, axis):
    p = jnp.zeros(row_id.shape, F32)
    for k in range(TOP_K):
        tgt = targets[k:k + 1, :] if axis == 0 else targets[:, k:k + 1]
        p = jnp.where(row_id == tgt, 1.0, p)
    return p.astype(BF16)


def _dispatch_kernel(ud_ref, used_ref, pst_ref, pnn_ref, u_ref, g_ref, lv_ref, tri_ref,
                     xs_ref, lkt_ref, sorted_ref, xaug_ref, zero_ref, sem):
    j = pl.program_id(0)
    g = g_ref[...]
    sel = g > 0.0
    rank = _dot(jnp.where(sel, 1.0, 0.0).astype(BF16), tri_ref[...])
    loff = jnp.concatenate([lv_ref[0]] * (MOE_TILE // LANES), axis=1)
    rem = jnp.where(sel, loff + rank, MOE_UNSET)
    rows = []
    for _ in range(TOP_K):
        cur = jnp.min(rem, axis=0, keepdims=True)
        rows.append(cur)
        rem = jnp.where(rem == cur, MOE_UNSET, rem)
    lk = jnp.concatenate(rows, axis=0)
    lkt_ref[0] = jnp.concatenate([lk, jnp.full((LANES - TOP_K, MOE_TILE), MOE_UNSET, F32)], axis=0).T

    gt = jnp.concatenate([g, jnp.zeros_like(g)], axis=0).T
    hi = gt.astype(BF16)
    r1 = gt - hi.astype(F32)
    mid = r1.astype(BF16)
    xaug_ref[:, :D_MODEL] = u_ref[...]
    xaug_ref[:, D_MODEL:D_MODEL + LANES] = hi
    xaug_ref[:, D_MODEL + LANES:D_MODEL + 2 * LANES] = mid
    xaug_ref[:, D_MODEL + 2 * LANES:] = (r1 - mid.astype(F32)).astype(BF16)

    used_units = used_ref[j]
    upc = MOE_RCH // MOE_UNIT

    def unit_copy(src_unit, dst_unit):
        return pltpu.make_async_copy(sorted_ref.at[_unit_rows(src_unit)], xs_ref.at[_unit_rows(dst_unit)], sem)

    def sort_chunk(rc, carry):
        r0 = pl.multiple_of(rc * MOE_RCH, MOE_RCH)
        row_id = (lax.broadcasted_iota(I32, (MOE_RCH, MOE_TILE), 0) + r0).astype(F32)
        sorted_ref[pl.ds(r0, MOE_RCH), :] = _dot(_one_hot_rows(row_id, lk, 0), xaug_ref[...]).astype(BF16)

        def send_unit(uu, c):
            unit = rc * upc + uu
            unit_copy(unit, ud_ref[j * (MOE_RCAP // MOE_UNIT) + unit]).start()
            return c

        lax.fori_loop(0, jnp.clip(used_units - rc * upc, 0, upc), send_unit, 0)
        return carry

    lax.fori_loop(0, (used_units + upc - 1) // upc, sort_chunk, 0)

    def drain_unit(uu, c):
        unit_copy(0, 0).wait()
        return c

    lax.fori_loop(0, used_units, drain_unit, 0)

    @pl.when(j == pl.num_programs(0) - 1)
    def _():
        zero_ref[...] = jnp.zeros(zero_ref.shape, BF16)

        def pad_expert(e, carry):
            def pad_copy(uu):
                return pltpu.make_async_copy(zero_ref, xs_ref.at[_unit_rows(pst_ref[e] + uu)], sem)

            def start(uu, c):
                pad_copy(uu).start()
                return c

            def wait(uu, c):
                pad_copy(uu).wait()
                return c

            lax.fori_loop(0, pnn_ref[e], start, 0)
            lax.fori_loop(0, pnn_ref[e], wait, 0)
            return carry

        lax.fori_loop(0, N_EXPERTS, pad_expert, 0)


def _dispatch(u2, gates_t, plan):
    t, d = u2.shape
    n_tiles = t // MOE_TILE
    xw = d + MOE_GW
    tri = jnp.asarray(np.triu(np.ones((MOE_TILE, MOE_TILE), np.float32), k=1), BF16)
    grid_spec = pltpu.PrefetchScalarGridSpec(
        num_scalar_prefetch=4,
        grid=(n_tiles,),
        in_specs=[pl.BlockSpec((MOE_TILE, d), lambda j, *_: (j, 0)),
                  pl.BlockSpec((N_EXPERTS, MOE_TILE), lambda j, *_: (0, j)),
                  pl.BlockSpec((1, N_EXPERTS, LANES), lambda j, *_: (j, 0, 0)),
                  pl.BlockSpec((MOE_TILE, MOE_TILE), lambda j, *_: (0, 0))],
        out_specs=[pl.BlockSpec(memory_space=pl.ANY),
                   pl.BlockSpec((1, MOE_TILE, LANES), lambda j, *_: (j, 0, 0))],
        scratch_shapes=[pltpu.VMEM((MOE_RCAP, xw), BF16), pltpu.VMEM((MOE_TILE, xw), BF16),
                        pltpu.VMEM((MOE_UNIT, xw), BF16), pltpu.SemaphoreType.DMA(())])
    return pl.pallas_call(
        _dispatch_kernel,
        grid_spec=grid_spec,
        out_shape=[jax.ShapeDtypeStruct((plan["n_blocks"] * MOE_BM, xw), BF16),
                   jax.ShapeDtypeStruct((n_tiles, MOE_TILE, LANES), F32)],
        compiler_params=_cparams(("arbitrary",), 56),
        name="dispatch",
    )(plan["unit_dst"], plan["used"], plan["padstart"], plan["padn"], u2, gates_t, plan["loff_v"], tri)


def _experts_kernel(be_ref, sb_ref, br_ref, nx_ref, nbu_ref, x_ref, wg_hbm, wu_hbm, wd_hbm, y_ref,
                    wg_st, wu_st, wd_st, wgb_ref, wub_ref, wdb_ref, sems):
    b = pl.program_id(0)

    def weight_copies(e):
        return (pltpu.make_async_copy(wg_hbm.at[e], wg_st, sems.at[0]),
                pltpu.make_async_copy(wu_hbm.at[e], wu_st, sems.at[1]),
                pltpu.make_async_copy(wd_hbm.at[e], wd_st, sems.at[2]))

    @pl.when(b < nbu_ref[0])
    def _():
        e = be_ref[b]

        @pl.when(b == 0)
        def _():
            for cp in weight_copies(e):
                cp.start()

        @pl.when(jnp.logical_or(b == 0, e != be_ref[jnp.maximum(b - 1, 0)]))
        def _():
            for cp in weight_copies(e):
                cp.wait()
            wgb_ref[...] = wg_st[...].astype(BF16)
            wub_ref[...] = wu_st[...].astype(BF16)
            wdb_ref[...] = wd_st[...].astype(BF16)

            @pl.when(nx_ref[b] >= 0)
            def _():
                for cp in weight_copies(nx_ref[b]):
                    cp.start()

        for rows in range(MOE_SUB, MOE_BM + 1, MOE_SUB):
            @pl.when(br_ref[b] == rows)
            def _(rows=rows):
                x = x_ref[:rows, :D_MODEL]
                g3 = ((x_ref[:rows, D_MODEL:D_MODEL + LANES].astype(F32)
                       + x_ref[:rows, D_MODEL + LANES:D_MODEL + 2 * LANES].astype(F32))
                      + x_ref[:rows, D_MODEL + 2 * LANES:].astype(F32))
                lane = lax.broadcasted_iota(I32, g3.shape, 1)
                gate = jnp.sum(jnp.where(lane == e, g3, 0.0), axis=1, keepdims=True)
                hid = (_silu(_dot(x, wgb_ref[...])) * _dot(x, wub_ref[...]) * gate).astype(BF16)
                y_ref[:rows, :] = _dot(hid, wdb_ref[...]).astype(y_ref.dtype)


def _experts(xs, plan, wg, wu, wd):
    ns, xw = xs.shape
    _, d, f = wg.shape
    step = lambda b, nbu: jnp.minimum(b, nbu[0] - 1)
    rows_map = lambda b, be, sb, br, nx, nbu: (sb[step(b, nbu)], 0)
    hbm = pl.BlockSpec(memory_space=pl.ANY)
    grid_spec = pltpu.PrefetchScalarGridSpec(
        num_scalar_prefetch=5,
        grid=(ns // MOE_BM,),
        in_specs=[pl.BlockSpec((MOE_BM, xw), rows_map), hbm, hbm, hbm],
        out_specs=pl.BlockSpec((MOE_BM, d), rows_map),
        scratch_shapes=[pltpu.VMEM((d, f), F32), pltpu.VMEM((d, f), F32), pltpu.VMEM((f, d), F32),
                        pltpu.VMEM((d, f), BF16), pltpu.VMEM((d, f), BF16), pltpu.VMEM((f, d), BF16),
                        pltpu.SemaphoreType.DMA((3,))])
    return pl.pallas_call(
        _experts_kernel,
        grid_spec=grid_spec,
        out_shape=jax.ShapeDtypeStruct((ns, d), BF16),
        compiler_params=_cparams(("arbitrary",), 48),
        name="experts",
    )(plan["step_e"], plan["step_blk"], plan["step_rows"], plan["step_next_e"], plan["nb_used"], xs, wg, wu, wd)


def _final_kernel(ud_ref, used_ref, lkt_ref, u_ref, wg_ref, wu_ref, wd_ref, x1_ref, g2_ref, lg_ref, lb_ref,
                  ys_ref, o_ref, ybuf_ref, sems):
    j = pl.program_id(0)
    used_units = used_ref[j]
    upc = MOE_RCH // MOE_UNIT

    @pl.when(j == 0)
    def _():
        ybuf_ref[...] = jnp.zeros(ybuf_ref.shape, ybuf_ref.dtype)

    def unit_copy(src_unit, dst_unit, chunk):
        return pltpu.make_async_copy(ys_ref.at[_unit_rows(src_unit)], ybuf_ref.at[_unit_rows(dst_unit)],
                                     sems.at[chunk])

    n_chunks = (used_units + upc - 1) // upc

    def fetch_chunk(rc, carry):
        def fetch_unit(uu, c):
            unit = rc * upc + uu
            unit_copy(ud_ref[j * (MOE_RCAP // MOE_UNIT) + unit], unit, rc).start()
            return c

        lax.fori_loop(0, jnp.clip(used_units - rc * upc, 0, upc), fetch_unit, 0)
        return carry

    lax.fori_loop(0, n_chunks, fetch_chunk, 0)

    u = u_ref[...]
    hid = (_silu(_dot(u, wg_ref[...])) * _dot(u, wu_ref[...])).astype(BF16)
    o_ref[...] = _dot(hid, wd_ref[...])

    lk = lkt_ref[0].T

    def combine_chunk(rc, carry):
        def wait_unit(uu, c):
            unit_copy(0, 0, rc).wait()
            return c

        lax.fori_loop(0, jnp.clip(used_units - rc * upc, 0, upc), wait_unit, 0)
        r0 = pl.multiple_of(rc * MOE_RCH, MOE_RCH)
        row_id = (lax.broadcasted_iota(I32, (MOE_RCH, MOE_TILE), 0) + r0).astype(F32)
        o_ref[...] += lax.dot_general(_one_hot_rows(row_id, lk, 0), ybuf_ref[pl.ds(r0, MOE_RCH), :],
                                      (((0,), (0,)), ((), ())), preferred_element_type=F32)
        return carry

    lax.fori_loop(0, n_chunks, combine_chunk, 0)
    h = ALPHA * x1_ref[...] + g2_ref[0] * o_ref[...]
    o_ref[...] = _layer_norm_rows(h, lg_ref[...], lb_ref[...])


def _final(u2, wg_s, wu_s, wd_s, ys, lkt, plan, x1, gate2, ln_g, ln_b, seq):
    t, d = x1.shape
    tm = MOE_TILE
    tpb = seq // tm
    f = wg_s.shape[1]
    once = pl.Buffered(1)
    rows = lambda: pl.BlockSpec((tm, d), lambda j, *_: (j, 0))
    full = lambda shape: pl.BlockSpec(shape, lambda j, *_: (0,) * len(shape), pipeline_mode=once)
    grid_spec = pltpu.PrefetchScalarGridSpec(
        num_scalar_prefetch=2,
        grid=(t // tm,),
        in_specs=[pl.BlockSpec((1, tm, LANES), lambda j, *_: (j, 0, 0)), rows(),
                  full((d, f)), full((d, f)), full((f, d)), rows(),
                  pl.BlockSpec((1, 1, d), lambda j, *_: (j // tpb, 0, 0)), full((1, d)), full((1, d)),
                  pl.BlockSpec(memory_space=pl.ANY)],
        out_specs=rows(),
        scratch_shapes=[pltpu.VMEM((MOE_RCAP, d), BF16), pltpu.SemaphoreType.DMA((MOE_RCAP // MOE_RCH,))])
    return pl.pallas_call(
        _final_kernel,
        grid_spec=grid_spec,
        out_shape=jax.ShapeDtypeStruct((t, d), F32),
        compiler_params=_cparams(("arbitrary",), 58),
        name="final",
    )(plan["unit_dst"], plan["used"], lkt, u2, wg_s, wu_s, wd_s, x1, gate2, ln_g[None, :], ln_b[None, :], ys)


def _split_hi_lo(w):
    hi = w.astype(BF16)
    lo = (w - hi.astype(F32)).astype(BF16)
    return hi, lo


def _rope_tables(positions, dim, reps):
    inv_freq = 1.0 / (ROPE_THETA ** (jnp.arange(0, dim, 2, dtype=F32) / dim))
    ang = positions.astype(F32).reshape(-1)[:, None] * inv_freq
    cos, sin = jnp.cos(ang), jnp.sin(ang)
    return (jnp.tile(jnp.concatenate([cos, cos], axis=1), (1, reps)),
            jnp.tile(jnp.concatenate([-sin, sin], axis=1), (1, reps)))


def _layer(x2, c, positions, bsz, seq, w_ada, b_ada, w_in, conv_w, conv_b, dt_bias, a_log, d_skip, ssd_norm_w,
           idx_k_ln_g, idx_k_ln_b, w_out, ln1_g, ln1_b, w_router, router_bias, w_gate_e, w_up_e, w_down_e,
           w_gate_s, w_up_s, w_down_s, ln2_g, ln2_b, tables):
    d = D_MODEL
    mod = _ada(c, w_ada, b_ada).reshape(bsz, 6, 1, d)
    shift1, scale1, gate1, shift2, scale2, gate2 = [mod[:, k] for k in range(6)]

    o_z, o_xbc, o_dt, o_q = 0, D_SSM, 2 * D_SSM + 2 * SSM_GROUPS * SSM_STATE, 2 * D_SSM + 2 * SSM_GROUPS * SSM_STATE + SSM_HEADS
    o_k, o_v, o_qi = o_q + D_ATTN, o_q + 2 * D_ATTN, o_q + 3 * D_ATTN
    o_ki = o_qi + IDX_HEADS * IDX_DIM
    o_wi = o_ki + IDX_DIM
    w_t = w_in.T
    w_main = w_t.astype(BF16)
    w_tail = jnp.concatenate([w_t[o_dt:o_q], w_t[o_wi:], jnp.zeros((TAIL_KI - TAIL_WI - IDX_HEADS, d), F32),
                              w_t[o_ki:o_wi]], axis=0)
    wt_hi, wt_lo = _split_hi_lo(w_tail)

    proj, tail = _inproj(x2, scale1, shift1, w_main, wt_hi, wt_lo, seq)
    y_ssm = _ssm(proj, tail, conv_w, conv_b, dt_bias, a_log, d_skip, ssd_norm_w, bsz, seq)
    q_r, k_r, v_t, qi_r, ki0, ki1, w_s = _prep(proj, tail, *tables, idx_k_ln_g, idx_k_ln_b)
    y_att = _attn(q_r, k_r, v_t, qi_r, ki0, ki1, w_s, bsz, seq)

    wo = w_out.astype(BF16)
    wr_hi, wr_lo = _split_hi_lo(w_router.T)
    rbias = jnp.broadcast_to(router_bias[:, None], (N_EXPERTS, LANES))
    x1, u2, gates_t = _outproj(y_ssm, y_att, wo[:D_SSM], wo[D_SSM:], x2, gate1, scale2, shift2, ln1_g, ln1_b,
                               wr_hi, wr_lo, rbias, seq)
    plan = _moe_plan(gates_t, x2.shape[0] // MOE_TILE)
    xs, lkt = _dispatch(u2, gates_t, plan)
    ys = _experts(xs, plan, w_gate_e, w_up_e, w_down_e)
    return _final(u2, w_gate_s.astype(BF16), w_up_s.astype(BF16), w_down_s.astype(BF16), ys, lkt, plan, x1, gate2,
                  ln2_g, ln2_b, seq)


def kernel(x, c, positions, w_ada, b_ada, w_in, conv_w, conv_b, dt_bias, a_log, d_skip, ssd_norm_w, idx_k_ln_g, idx_k_ln_b, w_out, ln1_g, ln1_b, w_router, router_bias, w_gate_e, w_up_e, w_down_e, w_gate_s, w_up_s, w_down_s, ln2_g, ln2_b):
    bsz, seq, d = x.shape
    tables = _rope_tables(positions, ATTN_HEAD_DIM, 1) + _rope_tables(positions, IDX_DIM, 2)
    x2 = x.reshape(bsz * seq, d)
    for l in range(w_ada.shape[0]):
        x2 = _layer(x2, c, positions, bsz, seq, w_ada[l], b_ada[l], w_in[l], conv_w[l], conv_b[l], dt_bias[l],
                    a_log[l], d_skip[l], ssd_norm_w[l], idx_k_ln_g[l], idx_k_ln_b[l], w_out[l], ln1_g[l], ln1_b[l],
                    w_router[l], router_bias[l], w_gate_e[l], w_up_e[l], w_down_e[l], w_gate_s[l], w_up_s[l],
                    w_down_s[l], ln2_g[l], ln2_b[l], tables)
    return x2.reshape(bsz, seq, d)
```

```python
import functools
import math

import jax
import jax.numpy as jnp
import numpy as np
from jax import lax
from jax.experimental import pallas as pl
from jax.experimental.pallas import tpu as pltpu

F32 = jnp.float32
BF16 = jnp.bfloat16
I32 = jnp.int32

D_MODEL = 2048
D_SSM = 1024
D_ATTN = 1024
SSM_HEAD_DIM = 64
SSM_HEADS = 16
SSM_GROUPS = 2
SSM_STATE = 128
CONV_WIDTH = 4
SSD_CHUNK = 128
ATTN_HEAD_DIM = 128
ATTN_HEADS = 8
IDX_HEADS = 16
IDX_DIM = 64
INDEX_TOPK = 256
ROPE_THETA = 10000.0
N_EXPERTS = 64
N_EXPERT_GROUPS = 8
EXPERTS_PER_GROUP = 8
TOPK_GROUPS = 4
TOP_K = 8
D_EXPERT = 512
D_SHARED = 512
ROUTED_SCALE = 2.5
DEPTH = 1
ALPHA = (2.0 * DEPTH) ** 0.25
LN_EPS = 1e-5

LANES = 128
SUBLANES = 8
VMEM_BYTES_V7X = 64 * 1024 * 1024
INT_MIN = -(2 ** 31)
KEY_LOWEST_FINITE = INT_MIN + 0x00800000

COL_Z, COL_XS, COL_Q, COL_K, COL_V, COL_QI, COL_BC = 0, 1024, 2048, 3072, 4096, 5120, 6144
N_MAIN = 6656
N_TAIL = LANES
TAIL_WI = SSM_HEADS
TAIL_KI = LANES - IDX_DIM


def _cparams(sem, vmem_mb):
    return pltpu.CompilerParams(dimension_semantics=sem, vmem_limit_bytes=vmem_mb * 1024 * 1024)


def _silu(v):
    return 0.5 * v * (1.0 + jnp.tanh(0.5 * v))


def _dot(a, b, precision=None):
    return jnp.dot(a, b, preferred_element_type=F32, precision=precision)


def _dot_nt(a, b, precision=None):
    return lax.dot_general(a, b, (((1,), (1,)), ((), ())), preferred_element_type=F32, precision=precision)


def _split3(x):
    hi = x.astype(BF16)
    r1 = x - hi.astype(F32)
    mid = r1.astype(BF16)
    return hi, mid, (r1 - mid.astype(F32)).astype(BF16)


def _layer_norm_rows(h, g, b):
    mu = jnp.mean(h, axis=-1, keepdims=True)
    d = h - mu
    var = jnp.mean(d * d, axis=-1, keepdims=True)
    return d * lax.rsqrt(var + LN_EPS) * g + b


def _ada_kernel(cb_ref, w_ref, b_ref, o_ref):
    tn = w_ref.shape[1]
    for bi in range(cb_ref.shape[0]):
        cb = _silu(cb_ref[bi])
        cols = [jnp.sum(w_ref[:, j * LANES:(j + 1) * LANES] * cb, axis=0, keepdims=True)
                for j in range(tn // LANES)]
        o_ref[bi:bi + 1, :] = jnp.concatenate(cols, axis=1) + b_ref[...]


def _ada(c, w_ada, b_ada):
    bsz, d = c.shape
    n = w_ada.shape[1]
    tn = 1024
    cb = jnp.broadcast_to(c[:, :, None], (bsz, d, LANES))
    return pl.pallas_call(
        _ada_kernel,
        grid=(n // tn,),
        in_specs=[pl.BlockSpec((bsz, d, LANES), lambda j: (0, 0, 0)),
                  pl.BlockSpec((d, tn), lambda j: (0, j)),
                  pl.BlockSpec((1, tn), lambda j: (0, j))],
        out_specs=pl.BlockSpec((bsz, tn), lambda j: (0, j)),
        out_shape=jax.ShapeDtypeStruct((bsz, n), F32),
        compiler_params=_cparams(("parallel",), 40),
        name="ada",
    )(cb, w_ada, b_ada.reshape(1, n))


def _inproj_kernel(x_ref, sc_ref, sh_ref, w_ref, wth_ref, wtl_ref, o_ref, t_ref, u_ref):
    @pl.when(pl.program_id(1) == 0)
    def _():
        u = x_ref[...] * (1.0 + sc_ref[0]) + sh_ref[0]
        uh = u.astype(BF16)
        ul = (u - uh.astype(F32)).astype(BF16)
        u_ref[...] = uh
        t_ref[...] = (_dot_nt(uh, wth_ref[...]) + _dot_nt(uh, wtl_ref[...]) + _dot_nt(ul, wth_ref[...]))

    o_ref[...] = _dot_nt(u_ref[...], w_ref[...].astype(BF16))


def _inproj_weight_row(j, tn):
    n_lead = (COL_Q - COL_Z) // tn
    n_mid = (COL_BC - COL_Q) // tn
    mid0 = COL_Q + 2 * SSM_GROUPS * SSM_STATE + SSM_HEADS
    return jnp.where(j < n_lead, j * tn,
                     jnp.where(j < n_lead + n_mid, mid0 + (j - n_lead) * tn, COL_Q + (j - n_lead - n_mid) * tn))


def _inproj(x2, scale1, shift1, w_main, wt_hi, wt_lo, seq):
    t, d = x2.shape
    tm, tn = 1024, 512
    tpb = seq // tm
    return pl.pallas_call(
        _inproj_kernel,
        grid=(t // tm, N_MAIN // tn),
        in_specs=[pl.BlockSpec((tm, d), lambda i, j: (i, 0)),
                  pl.BlockSpec((1, 1, d), lambda i, j: (i // tpb, 0, 0)),
                  pl.BlockSpec((1, 1, d), lambda i, j: (i // tpb, 0, 0)),
                  pl.BlockSpec((pl.Element(tn), pl.Element(d)), lambda i, j: (pl.multiple_of(_inproj_weight_row(j, tn), 16), 0)),
                  pl.BlockSpec((N_TAIL, d), lambda i, j: (0, 0)),
                  pl.BlockSpec((N_TAIL, d), lambda i, j: (0, 0))],
        out_specs=[pl.BlockSpec((tm, tn), lambda i, j: (i, j)),
                   pl.BlockSpec((tm, N_TAIL), lambda i, j: (i, 0))],
        out_shape=[jax.ShapeDtypeStruct((t, N_MAIN), F32),
                   jax.ShapeDtypeStruct((t, N_TAIL), F32)],
        scratch_shapes=[pltpu.VMEM((tm, d), BF16)],
        compiler_params=_cparams(("parallel", "arbitrary"), 48),
        name="inproj",
    )(x2, scale1, shift1, w_main, wt_hi, wt_lo)


def _ssm_kernel(z_ref, xs_ref, bc_ref, dt_ref, cwx_ref, cbx_ref, cwb_ref, cbb_ref, dtb_ref, alog_ref,
                dsk_ref, nw_ref, e_ref, e2_ref, tril_ref, o_ref, px_ref, pb_ref, st_ref):
    q = SSD_CHUNK
    hpg = SSM_HEADS // SSM_GROUPS
    gw = hpg * SSM_HEAD_DIM

    @pl.when(pl.program_id(1) == 0)
    def _():
        px_ref[0:SUBLANES, :] = jnp.zeros((SUBLANES, px_ref.shape[1]), F32)
        pb_ref[0:SUBLANES, :] = jnp.zeros((SUBLANES, pb_ref.shape[1]), F32)
        st_ref[...] = jnp.zeros(st_ref.shape, F32)

    def conv_silu(raw_ref, pad_ref, w_ref, b_ref):
        pad_ref[SUBLANES:SUBLANES + q, :] = raw_ref[...]
        acc = b_ref[...] + w_ref[0:1, :] * pad_ref[SUBLANES - 3:SUBLANES - 3 + q, :]
        for k in range(1, CONV_WIDTH):
            acc = acc + w_ref[k:k + 1, :] * pad_ref[SUBLANES - 3 + k:SUBLANES - 3 + k + q, :]
        pad_ref[0:SUBLANES, :] = raw_ref[q - SUBLANES:q, :]
        return _silu(acc)

    xs = conv_silu(xs_ref, px_ref, cwx_ref, cbx_ref)
    bc = conv_silu(bc_ref, pb_ref, cwb_ref, cbb_ref)

    dtr = dt_ref[...] + dtb_ref[...]
    dt = jnp.maximum(dtr, 0.0) + jnp.log(1.0 + jnp.exp(-jnp.abs(dtr)))
    log_a = dt * (-jnp.exp(alog_ref[...]))
    cs = sum(_dot(tril_ref[...], part) for part in _split3(log_a))
    cs_parts = _split3(cs)
    cs_e = sum(_dot(part, e_ref[...]) for part in cs_parts)
    dt_e = sum(_dot(part, e_ref[...]) for part in _split3(dt))
    cs_col = sum(_dot(part, e2_ref[...]) for part in cs_parts)
    cs_t = cs.T
    cs_last = cs_e[q - 1:q, :]

    xdt = xs * dt_e
    rows = lax.broadcasted_iota(I32, (q, q), 0)
    cols = lax.broadcasted_iota(I32, (q, q), 1)
    causal = rows >= cols
    first_half = lax.broadcasted_iota(I32, (q, LANES), 1) < SSM_HEAD_DIM

    y_pairs = []
    for g in range(SSM_GROUPS):
        b_g = bc[:, g * SSM_STATE:(g + 1) * SSM_STATE]
        c_g = bc[:, (SSM_GROUPS + g) * SSM_STATE:(SSM_GROUPS + g + 1) * SSM_STATE]
        cb = _dot_nt(c_g.astype(BF16), b_g.astype(BF16))
        for hp in range(hpg // 2):
            pair = g * (hpg // 2) + hp
            x_pair = xdt[:, pair * LANES:(pair + 1) * LANES].astype(BF16)
            ys = []
            for sub in range(2):
                h = 2 * pair + sub
                seg = cs_col[:, h * LANES:(h + 1) * LANES] - cs_t[h:h + 1, :]
                dec = jnp.exp(jnp.where(causal, seg, -jnp.inf))
                ys.append(_dot((cb * dec).astype(BF16), x_pair))
            y_pairs.append(jnp.where(first_half, ys[0], ys[1]))
    y_diag = jnp.concatenate(y_pairs, axis=1)

    xw = xdt * jnp.exp(cs_last - cs_e)
    y_off, new_states = [], []
    for g in range(SSM_GROUPS):
        b_g = bc[:, g * SSM_STATE:(g + 1) * SSM_STATE]
        c_g = bc[:, (SSM_GROUPS + g) * SSM_STATE:(SSM_GROUPS + g + 1) * SSM_STATE]
        h_in = st_ref[:, g * gw:(g + 1) * gw]
        y_off.append(_dot(c_g.astype(BF16), h_in.astype(BF16)))
        new_states.append(_dot(b_g.T.astype(BF16), xw[:, g * gw:(g + 1) * gw].astype(BF16)))
    y_off = jnp.concatenate(y_off, axis=1) * jnp.exp(cs_e)
    st_ref[...] = jnp.exp(cs_last) * st_ref[...] + jnp.concatenate(new_states, axis=1)

    y = y_diag + y_off + dsk_ref[...] * xs
    yf = y * _silu(z_ref[...])
    ms = jnp.mean(yf * yf, axis=-1, keepdims=True)
    o_ref[...] = (yf * lax.rsqrt(ms + LN_EPS) * nw_ref[...]).astype(o_ref.dtype)


def _ssm(proj, tail, conv_w, conv_b, dt_bias, a_log, d_skip, ssd_norm_w, bsz, seq):
    t = proj.shape[0]
    q = SSD_CHUNK
    n_c = seq // q
    nbc = 2 * SSM_GROUPS * SSM_STATE
    cw_x, cw_b = conv_w[:, :D_SSM], conv_w[:, D_SSM:]
    cb_x, cb_b = conv_b[None, :D_SSM], conv_b[None, D_SSM:]
    pad16 = lambda v: jnp.pad(v, (0, LANES - SSM_HEADS))[None, :]
    head_of_lane = np.arange(D_SSM) // SSM_HEAD_DIM
    e_mat = jnp.asarray((np.arange(LANES)[:, None] == head_of_lane[None, :]).astype(np.float32), BF16)
    e2_mat = jnp.asarray((np.arange(LANES)[:, None] == (np.arange(SSM_HEADS * LANES) // LANES)[None, :])
                         .astype(np.float32), BF16)
    tril = jnp.asarray(np.tril(np.ones((q, q), np.float32)), BF16)
    row = lambda b, c: b * n_c + c
    full = lambda shape: pl.BlockSpec(shape, lambda b, c: (0,) * len(shape))
    return pl.pallas_call(
        _ssm_kernel,
        grid=(bsz, n_c),
        in_specs=[pl.BlockSpec((q, D_SSM), lambda b, c: (row(b, c), COL_Z // D_SSM)),
                  pl.BlockSpec((q, D_SSM), lambda b, c: (row(b, c), COL_XS // D_SSM)),
                  pl.BlockSpec((q, nbc), lambda b, c: (row(b, c), COL_BC // nbc)),
                  pl.BlockSpec((q, LANES), lambda b, c: (row(b, c), 0)),
                  full((CONV_WIDTH, D_SSM)), full((1, D_SSM)), full((CONV_WIDTH, nbc)), full((1, nbc)),
                  full((1, LANES)), full((1, LANES)), full((1, D_SSM)), full((1, D_SSM)),
                  full((LANES, D_SSM)), full((LANES, SSM_HEADS * LANES)), full((q, q))],
        out_specs=pl.BlockSpec((q, D_SSM), lambda b, c: (row(b, c), 0)),
        out_shape=jax.ShapeDtypeStruct((t, D_SSM), BF16),
        scratch_shapes=[pltpu.VMEM((SUBLANES + q, D_SSM), F32),
                        pltpu.VMEM((SUBLANES + q, nbc), F32),
                        pltpu.VMEM((SSM_STATE, D_SSM), F32)],
        compiler_params=_cparams(("parallel", "arbitrary"), 40),
        name="ssm",
    )(proj, proj, proj, tail, cw_x, cb_x, cw_b, cb_b, pad16(dt_bias), pad16(a_log),
      jnp.repeat(d_skip, SSM_HEAD_DIM)[None, :], ssd_norm_w[None, :], e_mat, e2_mat, tril)


def _prep_kernel(q_ref, k_ref, v_ref, qi_ref, t_ref, ca_ref, sa_ref, ci_ref, si_ref, lg_ref, lb_ref,
                 qo_ref, ko_ref, vo_ref, qio_ref, ki0_ref, ki1_ref, wo_ref):
    ca, sa, ci, si = ca_ref[...], sa_ref[...], ci_ref[...], si_ref[...]
    lane = lax.broadcasted_iota(I32, ca.shape, 1)
    first32 = (lane % IDX_DIM) < (IDX_DIM // 2)
    q_scale = ATTN_HEAD_DIM ** -0.5 * math.log2(math.e)

    def rope_attn(v):
        return v * ca + pltpu.roll(v, ATTN_HEAD_DIM // 2, 1) * sa

    def rope_idx(v):
        rot = jnp.where(first32, pltpu.roll(v, LANES - IDX_DIM // 2, 1), pltpu.roll(v, IDX_DIM // 2, 1))
        return v * ci + rot * si

    for h in range(ATTN_HEADS):
        sl = slice(h * LANES, (h + 1) * LANES)
        qo_ref[:, sl] = (rope_attn(q_ref[:, sl]) * q_scale).astype(BF16)
        ko_ref[:, sl] = rope_attn(k_ref[:, sl]).astype(BF16)
    vo_ref[0] = v_ref[...].T.astype(BF16)
    for p in range(IDX_HEADS * IDX_DIM // LANES):
        sl = slice(p * LANES, (p + 1) * LANES)
        qio_ref[:, sl] = rope_idx(qi_ref[:, sl]).astype(BF16)

    tail = t_ref[...]
    valid = lane < IDX_DIM
    kraw = jnp.where(valid, pltpu.roll(tail, LANES - TAIL_KI, 1), 0.0)
    mu = jnp.sum(kraw, axis=-1, keepdims=True) * (1.0 / IDX_DIM)
    dk = jnp.where(valid, kraw - mu, 0.0)
    var = jnp.sum(dk * dk, axis=-1, keepdims=True) * (1.0 / IDX_DIM)
    kn = jnp.where(valid, dk * lax.rsqrt(var + LN_EPS) * lg_ref[...] + lb_ref[...], 0.0)
    kr = jnp.where(valid, rope_idx(kn), 0.0)
    ki0_ref[...] = kr.astype(BF16)
    ki1_ref[...] = pltpu.roll(kr, IDX_DIM, 1).astype(BF16)
    wo_ref[...] = pltpu.roll(tail, LANES - TAIL_WI, 1) * (IDX_HEADS ** -0.5 * IDX_DIM ** -0.5)


def _prep(proj, tail, cos_a, sin_a, cos_i, sin_i, ln_g, ln_b):
    t = proj.shape[0]
    tm = ATT_KC
    w = D_ATTN
    pad64 = lambda v: jnp.pad(v, (0, LANES - IDX_DIM))[None, :]
    col = lambda c: pl.BlockSpec((tm, w), lambda i: (i, c // w))
    lane_blk = lambda c: pl.BlockSpec((tm, LANES), lambda i: (i, c))
    full = lambda: pl.BlockSpec((1, LANES), lambda i: (0, 0))
    return pl.pallas_call(
        _prep_kernel,
        grid=(t // tm,),
        in_specs=[col(COL_Q), col(COL_K), col(COL_V), col(COL_QI), lane_blk(0),
                  lane_blk(0), lane_blk(0), lane_blk(0), lane_blk(0), full(), full()],
        out_specs=[pl.BlockSpec((tm, w), lambda i: (i, 0))] * 2 + [pl.BlockSpec((1, w, tm), lambda i: (i, 0, 0))]
        + [pl.BlockSpec((tm, w), lambda i: (i, 0))] + [pl.BlockSpec((tm, LANES), lambda i: (i, 0))] * 3,
        out_shape=[jax.ShapeDtypeStruct((t, w), BF16)] * 2 + [jax.ShapeDtypeStruct((t // tm, w, tm), BF16)]
        + [jax.ShapeDtypeStruct((t, w), BF16)]
        + [jax.ShapeDtypeStruct((t, LANES), BF16)] * 2 + [jax.ShapeDtypeStruct((t, LANES), F32)],
        compiler_params=_cparams(("parallel",), 48),
        name="prep",
    )(proj, proj, proj, proj, tail, cos_a, sin_a, cos_i, sin_i, pad64(ln_g), pad64(ln_b))


ATT_QB = 128
ATT_KC = 512
ATT_KS = 256


def _attn_kernel(q_ref, k_ref, vt_ref, qi_ref, ki0_ref, ki1_ref, w_ref, o_ref, sc_ref, acc_ref, *, topk):
    i = pl.program_id(1)
    n_ch = (i * ATT_QB + ATT_QB + ATT_KC - 1) // ATT_KC
    key_off = lax.broadcasted_iota(I32, (ATT_KC, ATT_QB), 0)
    q_pos = i * ATT_QB + lax.broadcasted_iota(I32, (ATT_KC, ATT_QB), 1)
    w_t = w_ref[...].T

    def score_chunk(c, carry):
        start = pl.multiple_of(c * ATT_KC, ATT_KC)
        ki0 = ki0_ref[pl.ds(start, ATT_KC), :]
        ki1 = ki1_ref[pl.ds(start, ATT_KC), :]
        acc = jnp.zeros((ATT_KC, ATT_QB), F32)
        for h in range(IDX_HEADS):
            pair = qi_ref[:, (h // 2) * LANES:(h // 2 + 1) * LANES]
            rel = _dot_nt(ki0 if h % 2 == 0 else ki1, pair)
            acc = acc + jnp.maximum(rel, 0.0) * w_t[h:h + 1, :]
        sc_ref[c] = jnp.where(start + key_off <= q_pos, acc, -jnp.inf)
        return carry

    lax.fori_loop(0, n_ch, score_chunk, 0)

    def key_to_float(key):
        return pltpu.bitcast(jnp.where(key < 0, key ^ jnp.int32(0x7FFFFFFF), key), F32)

    def search_bit(it, ans):
        cand = ans ^ lax.shift_left(jnp.int32(1), jnp.int32(31) - it)
        cand_f = key_to_float(cand)

        def count_chunk(c, cnt):
            part = jnp.where(sc_ref[c] >= cand_f, 1, 0).astype(I32)
            return cnt + jnp.sum(part.reshape(ATT_KC // SUBLANES, SUBLANES, ATT_QB), axis=0)

        cnt = lax.fori_loop(0, n_ch, count_chunk, jnp.zeros((SUBLANES, ATT_QB), I32))
        total = jnp.sum(cnt, axis=0, keepdims=True)
        return jnp.where(total >= topk, cand, ans)

    n_bits = jnp.where((i + 1) * ATT_QB <= topk, 0, 32)
    kth = lax.fori_loop(0, n_bits, search_bit, jnp.full((1, ATT_QB), INT_MIN, I32))
    lowest_finite = jnp.float32(np.finfo(np.float32).min)
    thr = jnp.where(kth < KEY_LOWEST_FINITE, lowest_finite, key_to_float(kth))

    acc_ref[...] = jnp.zeros(acc_ref.shape, F32)

    def att_chunk(c, carry):
        m_all, l_all = carry
        start = pl.multiple_of(c * ATT_KC, ATT_KC)
        for part in range(ATT_KC // ATT_KS):
            ks = slice(part * ATT_KS, (part + 1) * ATT_KS)
            sel = sc_ref[c, ks, :] >= thr
            m_rows, l_rows = [], []
            for h in range(ATTN_HEADS):
                sl = slice(h * LANES, (h + 1) * LANES)
                s = _dot_nt(k_ref[pl.ds(start + part * ATT_KS, ATT_KS), sl], q_ref[:, sl])
                s = jnp.where(sel, s, -jnp.inf)
                m_old = m_all[h:h + 1, :]
                m_new = jnp.maximum(m_old, jnp.max(s, axis=0, keepdims=True))
                p = jnp.exp2(s - m_new)
                alpha = jnp.exp2(m_old - m_new)
                l_rows.append(alpha * l_all[h:h + 1, :] + jnp.sum(p, axis=0, keepdims=True))
                m_rows.append(m_new)
                acc_ref[h] = alpha * acc_ref[h] + _dot(vt_ref[c, sl, ks], p.astype(BF16))
            m_all = jnp.concatenate(m_rows, axis=0)
            l_all = jnp.concatenate(l_rows, axis=0)
        return m_all, l_all

    init = (jnp.full((ATTN_HEADS, ATT_QB), -1e30, F32), jnp.zeros((ATTN_HEADS, ATT_QB), F32))
    _, l_all = lax.fori_loop(0, n_ch, att_chunk, init)
    for h in range(ATTN_HEADS):
        o_ref[:, h * LANES:(h + 1) * LANES] = (acc_ref[h] / l_all[h:h + 1, :]).T.astype(o_ref.dtype)


def _attn(q_r, k_r, v_t, qi_r, ki0, ki1, w_s, bsz, seq):
    t = q_r.shape[0]
    nq = seq // ATT_QB
    n_kc = seq // ATT_KC
    topk = min(INDEX_TOPK, seq // 4)
    once = pl.Buffered(1)
    qblk = lambda width: pl.BlockSpec((ATT_QB, width), lambda b, i: (b * nq + i, 0))
    per_batch = lambda width: pl.BlockSpec((seq, width), lambda b, i: (b, 0), pipeline_mode=once)
    return pl.pallas_call(
        functools.partial(_attn_kernel, topk=topk),
        grid=(bsz, nq),
        in_specs=[qblk(D_ATTN), per_batch(D_ATTN),
                  pl.BlockSpec((n_kc, D_ATTN, ATT_KC), lambda b, i: (b, 0, 0), pipeline_mode=once),
                  qblk(IDX_HEADS * IDX_DIM), per_batch(LANES), per_batch(LANES), qblk(LANES)],
        out_specs=qblk(D_ATTN),
        out_shape=jax.ShapeDtypeStruct((t, D_ATTN), BF16),
        scratch_shapes=[pltpu.VMEM((n_kc, ATT_KC, ATT_QB), F32),
                        pltpu.VMEM((ATTN_HEADS, ATTN_HEAD_DIM, ATT_QB), F32)],
        compiler_params=_cparams(("parallel", "arbitrary"), 48),
        name="attn",
    )(q_r, k_r, v_t, qi_r, ki0, ki1, w_s)


def _first_index_of_max(v, iota, n):
    m = jnp.max(v, axis=0, keepdims=True)
    idx = jnp.min(jnp.where(v == m, iota, n), axis=0, keepdims=True)
    return m, idx


def _route(scores, biased):
    tm = scores.shape[1]
    neg_inf = jnp.float32(-jnp.inf)
    iota8 = lax.broadcasted_iota(I32, (EXPERTS_PER_GROUP, tm), 0)
    group_scores = []
    for g in range(N_EXPERT_GROUPS):
        v = biased[g * EXPERTS_PER_GROUP:(g + 1) * EXPERTS_PER_GROUP, :]
        m1, i1 = _first_index_of_max(v, iota8, EXPERTS_PER_GROUP)
        m2 = jnp.max(jnp.where(iota8 == i1, neg_inf, v), axis=0, keepdims=True)
        group_scores.append(m1 + m2)
    gs = jnp.concatenate(group_scores, axis=0)
    keep = jnp.zeros(gs.shape, jnp.bool_)
    for _ in range(TOPK_GROUPS):
        _, gi = _first_index_of_max(gs, iota8, N_EXPERT_GROUPS)
        hit = iota8 == gi
        keep = jnp.logical_or(keep, hit)
        gs = jnp.where(hit, neg_inf, gs)
    keep_f = jnp.where(keep, 1.0, 0.0)
    masked = jnp.concatenate(
        [jnp.where(keep_f[g:g + 1, :] > 0.0, biased[g * EXPERTS_PER_GROUP:(g + 1) * EXPERTS_PER_GROUP, :], neg_inf)
         for g in range(N_EXPERT_GROUPS)], axis=0)
    iota64 = lax.broadcasted_iota(I32, (N_EXPERTS, tm), 0)
    sel_w = jnp.zeros((N_EXPERTS, tm), F32)
    for _ in range(TOP_K):
        _, ei = _first_index_of_max(masked, iota64, N_EXPERTS)
        hit = iota64 == ei
        sel_w = jnp.where(hit, scores, sel_w)
        masked = jnp.where(hit, neg_inf, masked)
    denom = jnp.sum(sel_w, axis=0, keepdims=True)
    return sel_w / denom * ROUTED_SCALE


def _outproj_kernel(ys_ref, ya_ref, wa_ref, wb_ref, x_ref, g1_ref, sc2_ref, sh2_ref, lg_ref, lb_ref,
                    wrh_ref, wrl_ref, rb_ref, x1_ref, u2_ref, gt_ref):
    mix = _dot(ys_ref[...], wa_ref[...]) + _dot(ya_ref[...], wb_ref[...])
    h = ALPHA * x_ref[...] + g1_ref[0] * mix
    x1 = _layer_norm_rows(h, lg_ref[...], lb_ref[...])
    x1_ref[...] = x1
    u2 = x1 * (1.0 + sc2_ref[0]) + sh2_ref[0]
    uh = u2.astype(BF16)
    ul = (u2 - uh.astype(F32)).astype(BF16)
    u2_ref[...] = uh
    wrh = wrh_ref[...]
    logits = _dot_nt(wrh, uh) + _dot_nt(wrh, ul) + _dot_nt(wrl_ref[...], uh)
    scores = 1.0 / (1.0 + jnp.exp(-logits))
    gt_ref[...] = _route(scores, scores + rb_ref[:, 0:1])


def _outproj(y_ssm, y_att, wo_a, wo_b, x2, gate1, scale2, shift2, ln_g, ln_b, wr_hi, wr_lo, rbias, seq):
    t, d = x2.shape
    tm = 512
    tpb = seq // tm
    half = y_ssm.shape[1]
    rows = lambda width: pl.BlockSpec((tm, width), lambda i: (i, 0))
    full = lambda shape: pl.BlockSpec(shape, lambda i: (0,) * len(shape))
    mod = lambda: pl.BlockSpec((1, 1, d), lambda i: (i // tpb, 0, 0))
    return pl.pallas_call(
        _outproj_kernel,
        grid=(t // tm,),
        in_specs=[rows(half), rows(half), full((half, d)), full((half, d)), rows(d), mod(), mod(), mod(),
                  full((1, d)), full((1, d)), full((N_EXPERTS, d)), full((N_EXPERTS, d)), full((N_EXPERTS, LANES))],
        out_specs=[rows(d), rows(d), pl.BlockSpec((N_EXPERTS, tm), lambda i: (0, i))],
        out_shape=[jax.ShapeDtypeStruct((t, d), F32), jax.ShapeDtypeStruct((t, d), BF16),
                   jax.ShapeDtypeStruct((N_EXPERTS, t), F32)],
        compiler_params=_cparams(("parallel",), 48),
        name="outproj",
    )(y_ssm, y_att, wo_a, wo_b, x2, gate1, scale2, shift2, ln_g[None, :], ln_b[None, :], wr_hi, wr_lo, rbias)


MOE_TILE = 256
MOE_UNIT = 16
MOE_BM = 512
MOE_SUB = 64
MOE_RCAP = TOP_K * MOE_TILE + N_EXPERTS * MOE_UNIT
MOE_RCH = 512
MOE_GW = 3 * LANES
MOE_UNSET = 1e9


def _moe_plan(gates_t, n_tiles):
    upb = MOE_BM // MOE_UNIT
    ups = MOE_SUB // MOE_UNIT
    cnt = jnp.sum((gates_t > 0.0).reshape(N_EXPERTS, n_tiles, MOE_TILE), axis=2, dtype=I32).T
    nun = (cnt + MOE_UNIT - 1) // MOE_UNIT
    lend = jnp.cumsum(nun, axis=1)
    loff = lend - nun
    tot = jnp.sum(nun, axis=0)
    tot_sub = (tot + ups - 1) // ups * ups
    nb = (tot + upb - 1) // upb
    cum_nb = jnp.cumsum(nb)
    first_blk = cum_nb - nb
    base = first_blk * upb
    goff = base[None, :] + jnp.cumsum(nun, axis=0) - nun
    unit = jnp.arange(MOE_RCAP // MOE_UNIT, dtype=I32)
    shift = goff - loff
    step_up = jnp.concatenate([shift[:, :1], shift[:, 1:] - shift[:, :-1]], axis=1)
    started = jnp.concatenate([jnp.ones((n_tiles, unit.shape[0], 1), jnp.bool_),
                               lend[:, None, :-1] <= unit[None, :, None]], axis=2)
    unit_dst = unit[None, :] + jnp.sum(jnp.where(started, step_up[:, None, :], 0), axis=2)
    n_blocks = (TOP_K * n_tiles * MOE_TILE // MOE_UNIT + n_tiles * N_EXPERTS + N_EXPERTS * (upb - 1) + upb - 1) // upb
    step = jnp.arange(n_blocks, dtype=I32)
    step_e = jnp.minimum(jnp.sum((cum_nb[None, :] <= step[:, None]).astype(I32), axis=1), N_EXPERTS - 1)
    of_step = step_e[:, None] == jnp.arange(N_EXPERTS, dtype=I32)[None, :]
    pick = lambda per_expert: jnp.sum(jnp.where(of_step, per_expert[None, :], 0), axis=1)
    nb_e = jnp.maximum(pick(nb), 1)
    first_e = pick(first_blk)
    local = (step - first_e + nb_e - 1) % nb_e
    step_rows = jnp.clip(pick(tot_sub) - local * upb, 0, upb) * MOE_UNIT
    end = pick(cum_nb)
    next_e = jnp.sum((cum_nb[None, :] <= end[:, None]).astype(I32), axis=1)
    step_next_e = jnp.where(end < cum_nb[-1], jnp.minimum(next_e, N_EXPERTS - 1), -1)
    loff_v = jnp.broadcast_to((loff * MOE_UNIT).astype(F32)[:, :, None], (n_tiles, N_EXPERTS, LANES))
    flat = lambda v: v.reshape(-1).astype(I32)
    return dict(unit_dst=flat(unit_dst), used=flat(lend[:, -1]), padstart=flat(base + tot), padn=flat(tot_sub - tot),
                step_e=flat(step_e), step_blk=flat(first_e + local), step_rows=flat(step_rows),
                step_next_e=flat(step_next_e),
                nb_used=cum_nb[-1:].astype(I32), loff_v=loff_v, n_blocks=n_blocks)


def _unit_rows(unit):
    return pl.ds(pl.multiple_of(unit * MOE_UNIT, MOE_UNIT), MOE_UNIT)


def _for_each(n, body):
    def pair(p, carry):
        body(2 * p)
        body(2 * p + 1)
        return carry

    lax.fori_loop(0, lax.shift_right_logical(n, 1), pair, 0)

    @pl.when((n & 1) == 1)
    def _():
        body(n - 1)


def _one_hot_rows(row_id, targets):
    p = jnp.zeros(row_id.shape, F32)
    for k in range(TOP_K):
        p = jnp.where(row_id == targets[k:k + 1, :], 1.0, p)
    return p.astype(BF16)


def _dispatch_kernel(ud_ref, used_ref, pst_ref, pnn_ref, u_ref, g_ref, lv_ref, tri_ref,
                     xs_ref, lk_ref, sorted_ref, xaug_ref, zero_ref, sem):
    j = pl.program_id(0)
    g = g_ref[...]
    sel = g > 0.0
    rank = _dot(jnp.where(sel, 1.0, 0.0).astype(BF16), tri_ref[...])
    loff = jnp.concatenate([lv_ref[0]] * (MOE_TILE // LANES), axis=1)
    rem = jnp.where(sel, loff + rank, MOE_UNSET)
    rows = []
    for _ in range(TOP_K):
        cur = jnp.min(rem, axis=0, keepdims=True)
        rows.append(cur)
        rem = jnp.where(rem == cur, MOE_UNSET, rem)
    lk = jnp.concatenate(rows, axis=0)
    lk_ref[0] = lk

    gt = jnp.concatenate([g, jnp.zeros_like(g)], axis=0).T
    hi = gt.astype(BF16)
    r1 = gt - hi.astype(F32)
    mid = r1.astype(BF16)
    xaug_ref[:, :D_MODEL] = u_ref[...]
    xaug_ref[:, D_MODEL:D_MODEL + LANES] = hi
    xaug_ref[:, D_MODEL + LANES:D_MODEL + 2 * LANES] = mid
    xaug_ref[:, D_MODEL + 2 * LANES:] = (r1 - mid.astype(F32)).astype(BF16)

    used_units = used_ref[j]
    upc = MOE_RCH // MOE_UNIT

    def unit_copy(src_unit, dst_unit):
        return pltpu.make_async_copy(sorted_ref.at[_unit_rows(src_unit)], xs_ref.at[_unit_rows(dst_unit)], sem)

    def sort_chunk(rc, carry):
        r0 = pl.multiple_of(rc * MOE_RCH, MOE_RCH)
        row_id = (lax.broadcasted_iota(I32, (MOE_RCH, MOE_TILE), 0) + r0).astype(F32)
        sorted_ref[pl.ds(r0, MOE_RCH), :] = _dot(_one_hot_rows(row_id, lk), xaug_ref[...]).astype(BF16)

        def send_unit(uu):
            unit = rc * upc + uu
            unit_copy(unit, ud_ref[j * (MOE_RCAP // MOE_UNIT) + unit]).start()

        _for_each(jnp.clip(used_units - rc * upc, 0, upc), send_unit)
        return carry

    lax.fori_loop(0, (used_units + upc - 1) // upc, sort_chunk, 0)

    _for_each(used_units, lambda uu: unit_copy(0, 0).wait())

    @pl.when(j == pl.num_programs(0) - 1)
    def _():
        zero_ref[...] = jnp.zeros(zero_ref.shape, BF16)

        def pad_expert(e, carry):
            def pad_copy(uu):
                return pltpu.make_async_copy(zero_ref, xs_ref.at[_unit_rows(pst_ref[e] + uu)], sem)

            def start(uu, c):
                pad_copy(uu).start()
                return c

            def wait(uu, c):
                pad_copy(uu).wait()
                return c

            lax.fori_loop(0, pnn_ref[e], start, 0)
            lax.fori_loop(0, pnn_ref[e], wait, 0)
            return carry

        lax.fori_loop(0, N_EXPERTS, pad_expert, 0)


def _dispatch(u2, gates_t, plan):
    t, d = u2.shape
    n_tiles = t // MOE_TILE
    xw = d + MOE_GW
    tri = jnp.asarray(np.triu(np.ones((MOE_TILE, MOE_TILE), np.float32), k=1), BF16)
    grid_spec = pltpu.PrefetchScalarGridSpec(
        num_scalar_prefetch=4,
        grid=(n_tiles,),
        in_specs=[pl.BlockSpec((MOE_TILE, d), lambda j, *_: (j, 0)),
                  pl.BlockSpec((N_EXPERTS, MOE_TILE), lambda j, *_: (0, j)),
                  pl.BlockSpec((1, N_EXPERTS, LANES), lambda j, *_: (j, 0, 0)),
                  pl.BlockSpec((MOE_TILE, MOE_TILE), lambda j, *_: (0, 0))],
        out_specs=[pl.BlockSpec(memory_space=pl.ANY),
                   pl.BlockSpec((1, TOP_K, MOE_TILE), lambda j, *_: (j, 0, 0))],
        scratch_shapes=[pltpu.VMEM((MOE_RCAP, xw), BF16), pltpu.VMEM((MOE_TILE, xw), BF16),
                        pltpu.VMEM((MOE_UNIT, xw), BF16), pltpu.SemaphoreType.DMA(())])
    return pl.pallas_call(
        _dispatch_kernel,
        grid_spec=grid_spec,
        out_shape=[jax.ShapeDtypeStruct((plan["n_blocks"] * MOE_BM, xw), BF16),
                   jax.ShapeDtypeStruct((n_tiles, TOP_K, MOE_TILE), F32)],
        compiler_params=_cparams(("arbitrary",), 56),
        name="dispatch",
    )(plan["unit_dst"], plan["used"], plan["padstart"], plan["padn"], u2, gates_t, plan["loff_v"], tri)


def _experts_kernel(be_ref, sb_ref, br_ref, nx_ref, nbu_ref, x_ref, wg_hbm, wu_hbm, wd_hbm, y_ref,
                    wg_st, wu_st, wd_st, wgb_ref, wub_ref, wdb_ref, sems):
    b = pl.program_id(0)

    def weight_copies(e):
        return (pltpu.make_async_copy(wg_hbm.at[e], wg_st, sems.at[0]),
                pltpu.make_async_copy(wu_hbm.at[e], wu_st, sems.at[1]),
                pltpu.make_async_copy(wd_hbm.at[e], wd_st, sems.at[2]))

    @pl.when(b < nbu_ref[0])
    def _():
        e = be_ref[b]

        @pl.when(b == 0)
        def _():
            for cp in weight_copies(e):
                cp.start()

        @pl.when(jnp.logical_or(b == 0, e != be_ref[jnp.maximum(b - 1, 0)]))
        def _():
            for cp in weight_copies(e):
                cp.wait()
            wgb_ref[...] = wg_st[...].astype(BF16)
            wub_ref[...] = wu_st[...].astype(BF16)
            wdb_ref[...] = wd_st[...].astype(BF16)

            @pl.when(nx_ref[b] >= 0)
            def _():
                for cp in weight_copies(nx_ref[b]):
                    cp.start()

        for rows in range(MOE_SUB, MOE_BM + 1, MOE_SUB):
            @pl.when(br_ref[b] == rows)
            def _(rows=rows):
                x = x_ref[:rows, :D_MODEL]
                g3 = ((x_ref[:rows, D_MODEL:D_MODEL + LANES].astype(F32)
                       + x_ref[:rows, D_MODEL + LANES:D_MODEL + 2 * LANES].astype(F32))
                      + x_ref[:rows, D_MODEL + 2 * LANES:].astype(F32))
                lane = lax.broadcasted_iota(I32, g3.shape, 1)
                gate = jnp.sum(jnp.where(lane == e, g3, 0.0), axis=1, keepdims=True)
                hid = (_silu(_dot(x, wgb_ref[...])) * _dot(x, wub_ref[...]) * gate).astype(BF16)
                y_ref[:rows, :] = _dot(hid, wdb_ref[...]).astype(y_ref.dtype)


def _experts(xs, plan, wg, wu, wd):
    ns, xw = xs.shape
    _, d, f = wg.shape
    step = lambda b, nbu: jnp.minimum(b, nbu[0] - 1)
    rows_map = lambda b, be, sb, br, nx, nbu: (sb[step(b, nbu)], 0)
    hbm = pl.BlockSpec(memory_space=pl.ANY)
    grid_spec = pltpu.PrefetchScalarGridSpec(
        num_scalar_prefetch=5,
        grid=(ns // MOE_BM,),
        in_specs=[pl.BlockSpec((MOE_BM, xw), rows_map), hbm, hbm, hbm],
        out_specs=pl.BlockSpec((MOE_BM, d), rows_map),
        scratch_shapes=[pltpu.VMEM((d, f), F32), pltpu.VMEM((d, f), F32), pltpu.VMEM((f, d), F32),
                        pltpu.VMEM((d, f), BF16), pltpu.VMEM((d, f), BF16), pltpu.VMEM((f, d), BF16),
                        pltpu.SemaphoreType.DMA((3,))])
    return pl.pallas_call(
        _experts_kernel,
        grid_spec=grid_spec,
        out_shape=jax.ShapeDtypeStruct((ns, d), BF16),
        compiler_params=_cparams(("arbitrary",), 48),
        name="experts",
    )(plan["step_e"], plan["step_blk"], plan["step_rows"], plan["step_next_e"], plan["nb_used"], xs, wg, wu, wd)


def _final_kernel(ud_ref, used_ref, lk_ref, u_ref, wg_ref, wu_ref, wd_ref, x1_ref, g2_ref, lg_ref, lb_ref,
                  ys_ref, o_ref, ybuf_ref, sems):
    j = pl.program_id(0)
    used_units = used_ref[j]
    upc = MOE_RCH // MOE_UNIT

    @pl.when(j == 0)
    def _():
        ybuf_ref[...] = jnp.zeros(ybuf_ref.shape, ybuf_ref.dtype)

    def unit_copy(src_unit, dst_unit, chunk):
        return pltpu.make_async_copy(ys_ref.at[_unit_rows(src_unit)], ybuf_ref.at[_unit_rows(dst_unit)],
                                     sems.at[chunk])

    n_chunks = (used_units + upc - 1) // upc

    def fetch_chunk(rc, carry):
        def fetch_unit(uu):
            unit = rc * upc + uu
            unit_copy(ud_ref[j * (MOE_RCAP // MOE_UNIT) + unit], unit, rc).start()

        _for_each(jnp.clip(used_units - rc * upc, 0, upc), fetch_unit)
        return carry

    lax.fori_loop(0, n_chunks, fetch_chunk, 0)

    u = u_ref[...]
    hid = (_silu(_dot(u, wg_ref[...])) * _dot(u, wu_ref[...])).astype(BF16)
    o_ref[...] = _dot(hid, wd_ref[...])

    lk = lk_ref[0]

    def combine_chunk(rc, carry):
        _for_each(jnp.clip(used_units - rc * upc, 0, upc), lambda uu: unit_copy(0, 0, rc).wait())
        r0 = pl.multiple_of(rc * MOE_RCH, MOE_RCH)
        row_id = (lax.broadcasted_iota(I32, (MOE_RCH, MOE_TILE), 0) + r0).astype(F32)
        o_ref[...] += lax.dot_general(_one_hot_rows(row_id, lk), ybuf_ref[pl.ds(r0, MOE_RCH), :],
                                      (((0,), (0,)), ((), ())), preferred_element_type=F32)
        return carry

    lax.fori_loop(0, n_chunks, combine_chunk, 0)
    h = ALPHA * x1_ref[...] + g2_ref[0] * o_ref[...]
    o_ref[...] = _layer_norm_rows(h, lg_ref[...], lb_ref[...])


def _final(u2, wg_s, wu_s, wd_s, ys, lk, plan, x1, gate2, ln_g, ln_b, seq):
    t, d = x1.shape
    tm = MOE_TILE
    tpb = seq // tm
    f = wg_s.shape[1]
    once = pl.Buffered(1)
    rows = lambda: pl.BlockSpec((tm, d), lambda j, *_: (j, 0))
    full = lambda shape: pl.BlockSpec(shape, lambda j, *_: (0,) * len(shape), pipeline_mode=once)
    grid_spec = pltpu.PrefetchScalarGridSpec(
        num_scalar_prefetch=2,
        grid=(t // tm,),
        in_specs=[pl.BlockSpec((1, TOP_K, tm), lambda j, *_: (j, 0, 0)), rows(),
                  full((d, f)), full((d, f)), full((f, d)), rows(),
                  pl.BlockSpec((1, 1, d), lambda j, *_: (j // tpb, 0, 0)), full((1, d)), full((1, d)),
                  pl.BlockSpec(memory_space=pl.ANY)],
        out_specs=rows(),
        scratch_shapes=[pltpu.VMEM((MOE_RCAP, d), BF16), pltpu.SemaphoreType.DMA((MOE_RCAP // MOE_RCH,))])
    return pl.pallas_call(
        _final_kernel,
        grid_spec=grid_spec,
        out_shape=jax.ShapeDtypeStruct((t, d), F32),
        compiler_params=_cparams(("arbitrary",), 58),
        name="final",
    )(plan["unit_dst"], plan["used"], lk, u2, wg_s, wu_s, wd_s, x1, gate2, ln_g[None, :], ln_b[None, :], ys)


def _split_hi_lo(w):
    hi = w.astype(BF16)
    lo = (w - hi.astype(F32)).astype(BF16)
    return hi, lo


def _rope_tables(positions, dim, reps):
    inv_freq = 1.0 / (ROPE_THETA ** (jnp.arange(0, dim, 2, dtype=F32) / dim))
    ang = positions.astype(F32).reshape(-1)[:, None] * inv_freq
    cos, sin = jnp.cos(ang), jnp.sin(ang)
    return (jnp.tile(jnp.concatenate([cos, cos], axis=1), (1, reps)),
            jnp.tile(jnp.concatenate([-sin, sin], axis=1), (1, reps)))


def _layer(x2, c, positions, bsz, seq, w_ada, b_ada, w_in, conv_w, conv_b, dt_bias, a_log, d_skip, ssd_norm_w,
           idx_k_ln_g, idx_k_ln_b, w_out, ln1_g, ln1_b, w_router, router_bias, w_gate_e, w_up_e, w_down_e,
           w_gate_s, w_up_s, w_down_s, ln2_g, ln2_b, tables):
    d = D_MODEL
    mod = _ada(c, w_ada, b_ada).reshape(bsz, 6, 1, d)
    shift1, scale1, gate1, shift2, scale2, gate2 = [mod[:, k] for k in range(6)]

    o_z, o_xbc, o_dt, o_q = 0, D_SSM, 2 * D_SSM + 2 * SSM_GROUPS * SSM_STATE, 2 * D_SSM + 2 * SSM_GROUPS * SSM_STATE + SSM_HEADS
    o_k, o_v, o_qi = o_q + D_ATTN, o_q + 2 * D_ATTN, o_q + 3 * D_ATTN
    o_ki = o_qi + IDX_HEADS * IDX_DIM
    o_wi = o_ki + IDX_DIM
    w_t = w_in.T
    w_main = w_t
    w_tail = jnp.concatenate([w_t[o_dt:o_q], w_t[o_wi:], jnp.zeros((TAIL_KI - TAIL_WI - IDX_HEADS, d), F32),
                              w_t[o_ki:o_wi]], axis=0)
    wt_hi, wt_lo = _split_hi_lo(w_tail)

    proj, tail = _inproj(x2, scale1, shift1, w_main, wt_hi, wt_lo, seq)
    y_ssm = _ssm(proj, tail, conv_w, conv_b, dt_bias, a_log, d_skip, ssd_norm_w, bsz, seq)
    q_r, k_r, v_t, qi_r, ki0, ki1, w_s = _prep(proj, tail, *tables, idx_k_ln_g, idx_k_ln_b)
    y_att = _attn(q_r, k_r, v_t, qi_r, ki0, ki1, w_s, bsz, seq)

    wo = w_out.astype(BF16)
    wr_hi, wr_lo = _split_hi_lo(w_router.T)
    rbias = jnp.broadcast_to(router_bias[:, None], (N_EXPERTS, LANES))
    x1, u2, gates_t = _outproj(y_ssm, y_att, wo[:D_SSM], wo[D_SSM:], x2, gate1, scale2, shift2, ln1_g, ln1_b,
                               wr_hi, wr_lo, rbias, seq)
    plan = _moe_plan(gates_t, x2.shape[0] // MOE_TILE)
    xs, lk = _dispatch(u2, gates_t, plan)
    ys = _experts(xs, plan, w_gate_e, w_up_e, w_down_e)
    return _final(u2, w_gate_s.astype(BF16), w_up_s.astype(BF16), w_down_s.astype(BF16), ys, lk, plan, x1, gate2,
                  ln2_g, ln2_b, seq)


def kernel(x, c, positions, w_ada, b_ada, w_in, conv_w, conv_b, dt_bias, a_log, d_skip, ssd_norm_w, idx_k_ln_g, idx_k_ln_b, w_out, ln1_g, ln1_b, w_router, router_bias, w_gate_e, w_up_e, w_down_e, w_gate_s, w_up_s, w_down_s, ln2_g, ln2_b):
    bsz, seq, d = x.shape
    tables = _rope_tables(positions, ATTN_HEAD_DIM, 1) + _rope_tables(positions, IDX_DIM, 2)
    x2 = x.reshape(bsz * seq, d)
    for l in range(w_ada.shape[0]):
        x2 = _layer(x2, c, positions, bsz, seq, w_ada[l], b_ada[l], w_in[l], conv_w[l], conv_b[l], dt_bias[l],
                    a_log[l], d_skip[l], ssd_norm_w[l], idx_k_ln_g[l], idx_k_ln_b[l], w_out[l], ln1_g[l], ln1_b[l],
                    w_router[l], router_bias[l], w_gate_e[l], w_up_e[l], w_down_e[l], w_gate_s[l], w_up_s[l],
                    w_down_s[l], ln2_g[l], ln2_b[l], tables)
    return x2.reshape(bsz, seq, d)
```
